```python
import jax, jax.numpy as jnp
from jax import lax
import numpy as np

D_MODEL = 1024
BATCH = 4
SEQ = 8192
DEPTH = 4

GRID_W = 64
CTX_LEN = 256
N_MIXERS = 2
N_RET_LAYERS = (DEPTH + 1) // 2
N_MLA_LAYERS = DEPTH // 2
RET_HEADS = 4
RET_DK = D_MODEL // RET_HEADS
RET_DV = 2 * RET_DK
RET_CHUNK = 128
RET_THETA_BASE = 10000.0
RET_IN_COLS = 2 * RET_HEADS * RET_DK + 3 * RET_HEADS * RET_DV
MLA_HEADS = 8
MLA_NOPE = 128
MLA_ROPE = 64
MLA_V = 128
MLA_Q_LORA = 384
MLA_KV_LORA = 256
MLA_IN_COLS = MLA_Q_LORA + MLA_KV_LORA + MLA_ROPE
ROPE_BASE = 10000.0
Q_BLOCK = 128
N_EXPERTS = 32
TOP_K = 4
D_FF = D_MODEL
SWIGLU_ALPHA = 1.702
SWIGLU_LIMIT = 7.0
EPS = 1e-6
GN_EPS = 1e-5

kernel_name = "hybrid_retention_mla_moe_diffusion_trunk"


def rmsnorm(x, g):
    xf = x.astype(jnp.float32)
    xf = xf * lax.rsqrt(jnp.mean(xf * xf, axis=-1, keepdims=True) + EPS)
    return xf.astype(x.dtype) * g


def group_norm_heads(y, g):
    B, L, H, DV = y.shape
    yf = y.astype(jnp.float32)
    mu = jnp.mean(yf, axis=-1, keepdims=True)
    var = jnp.mean(jnp.square(yf - mu), axis=-1, keepdims=True)
    out = (yf - mu) * lax.rsqrt(var + GN_EPS)
    return out.reshape(B, L, H * DV).astype(y.dtype) * g


def modulate(h, shift, scale):
    return h * (1.0 + scale) + shift


def apply_rotary(x, ang):
    cos = jnp.cos(ang).astype(x.dtype)
    sin = jnp.sin(ang).astype(x.dtype)
    x1, x2 = jnp.split(x, 2, axis=-1)
    return jnp.concatenate([x1 * cos - x2 * sin, x1 * sin + x2 * cos], axis=-1)


def chunk_retention(q, k, v, log_g, s0):
    B, L, H, _ = q.shape
    DV = v.shape[-1]
    n = L // RET_CHUNK

    def to_chunks(t):
        return t.reshape(B, n, RET_CHUNK, H, t.shape[-1]).transpose(1, 0, 3, 2, 4)

    pos = jnp.arange(RET_CHUNK, dtype=jnp.float32)
    lg = log_g[:, None]
    q_decay = jnp.exp(lg * (pos + 1.0)).astype(q.dtype)
    k_decay = jnp.exp(lg * (RET_CHUNK - 1.0 - pos)).astype(q.dtype)
    rel = pos[:, None] - pos[None, :]
    d_mask = jnp.where(rel >= 0, jnp.exp(lg[:, :, None] * jnp.maximum(rel, 0.0)), 0.0).astype(q.dtype)
    chunk_decay = jnp.exp(log_g * RET_CHUNK).astype(q.dtype)

    def step(s, qkv):
        qi, ki, vi = qkv
        scores = jnp.einsum('bhid,bhjd->bhij', qi, ki) * d_mask
        o = (jnp.einsum('bhij,bhjv->bhiv', scores, vi)
             + jnp.einsum('bhid,bhdv->bhiv', qi * q_decay[..., None], s))
        s = s * chunk_decay[:, None, None] + jnp.einsum('bhjd,bhjv->bhdv', ki * k_decay[..., None], vi)
        return s, o

    s_final, o = lax.scan(step, s0, (to_chunks(q), to_chunks(k), to_chunks(v)))
    o = o.transpose(1, 0, 3, 2, 4).reshape(B, L, H, DV)
    return o, s_final


def retention_mixer(h_ctx, h_lat, w_in, decay_logit, gn_g, w_out, ang_lat, need_ctx):
    def project(h, ang):
        B, L, _ = h.shape
        p = h @ w_in
        q, k, v, g_f, g_b = jnp.split(p, np.cumsum([RET_HEADS * RET_DK, RET_HEADS * RET_DK,
                                                     RET_HEADS * RET_DV, RET_HEADS * RET_DV]).tolist(), axis=-1)
        q = q.reshape(B, L, RET_HEADS, RET_DK)
        k = k.reshape(B, L, RET_HEADS, RET_DK) * (RET_DK ** -0.5)
        v = v.reshape(B, L, RET_HEADS, RET_DV)
        if ang is not None:
            q = apply_rotary(q, ang[:, None, :])
            k = apply_rotary(k, ang[:, None, :])
        return q, k, v, g_f, g_b

    q_c, k_c, v_c, gf_c, gb_c = project(h_ctx, None)
    q_l, k_l, v_l, gf_l, gb_l = project(h_lat, ang_lat)
    B = h_lat.shape[0]
    log_g = jax.nn.log_sigmoid(decay_logit.astype(jnp.float32))
    s_zero = jnp.zeros((B, RET_HEADS, RET_DK, RET_DV), dtype=q_l.dtype)
    flip = lambda t: t[:, ::-1]

    o_cf, s_cf = chunk_retention(q_c, k_c, v_c, log_g[0], s_zero)
    o_lf, _ = chunk_retention(q_l, k_l, v_l, log_g[0], s_cf)
    o_cb, s_cb = chunk_retention(flip(q_c), flip(k_c), flip(v_c), log_g[1], s_zero)
    o_lb, _ = chunk_retention(flip(q_l), flip(k_l), flip(v_l), log_g[1], s_cb)
    o_cb, o_lb = flip(o_cb), flip(o_lb)

    def merge(o_f, o_b, g_f, g_b):
        y = jax.nn.silu(g_f) * group_norm_heads(o_f, gn_g[0]) + jax.nn.silu(g_b) * group_norm_heads(o_b, gn_g[1])
        return y @ w_out

    y_lat = merge(o_lf, o_lb, gf_l, gb_l)
    y_ctx = merge(o_cf, o_cb, gf_c, gb_c) if need_ctx else None
    return y_ctx, y_lat


def mla_mixer(h_ctx, h_lat, w_in, q_norm, kv_norm, w_uq, w_ukv, w_out, ang_lat, need_ctx):
    def project(h, ang):
        B, L, _ = h.shape
        p = h @ w_in
        c_q, c_kv, k_r = jnp.split(p, [MLA_Q_LORA, MLA_Q_LORA + MLA_KV_LORA], axis=-1)
        c_q = rmsnorm(c_q, q_norm)
        c_kv = rmsnorm(c_kv, kv_norm)
        q = (c_q @ w_uq).reshape(B, L, MLA_HEADS, MLA_NOPE + MLA_ROPE)
        kv = (c_kv @ w_ukv).reshape(B, L, MLA_HEADS, MLA_NOPE + MLA_V)
        q_nope, q_rope = jnp.split(q, [MLA_NOPE], axis=-1)
        k_nope, v = jnp.split(kv, [MLA_NOPE], axis=-1)
        if ang is not None:
            q_rope = apply_rotary(q_rope, ang[:, None, :])
            k_r = apply_rotary(k_r, ang)
        k_r = jnp.broadcast_to(k_r[:, :, None, :], (B, L, MLA_HEADS, MLA_ROPE))
        q = jnp.concatenate([q_nope, q_rope], axis=-1)
        k = jnp.concatenate([k_nope, k_r], axis=-1)
        return q, k, v

    scale = (MLA_NOPE + MLA_ROPE) ** -0.5
    q_c, k_c, v_c = project(h_ctx, None)
    q_l, k_l, v_l = project(h_lat, ang_lat)

    def attend(q, k, v):
        s = jnp.einsum('bqhd,bkhd->bhqk', q, k).astype(jnp.float32) * scale
        p = jax.nn.softmax(s, axis=-1).astype(v.dtype)
        return jnp.einsum('bhqk,bkhd->bqhd', p, v)

    B, L = h_lat.shape[0], h_lat.shape[1]
    k_all = jnp.concatenate([k_c, k_l], axis=1)
    v_all = jnp.concatenate([v_c, v_l], axis=1)
    n_blk = L // Q_BLOCK
    q_blocks = q_l.reshape(B, n_blk, Q_BLOCK, MLA_HEADS, MLA_NOPE + MLA_ROPE).transpose(1, 0, 2, 3, 4)
    o_l = lax.map(lambda qb: attend(qb, k_all, v_all), q_blocks)
    o_l = o_l.transpose(1, 0, 2, 3, 4).reshape(B, L, MLA_HEADS * MLA_V)
    y_lat = o_l @ w_out
    y_ctx = None
    if need_ctx:
        o_c = attend(q_c, k_c, v_c).reshape(B, h_ctx.shape[1], MLA_HEADS * MLA_V)
        y_ctx = o_c @ w_out
    return y_ctx, y_lat


def moe_ffn(h, router_w, router_b, w_up, b_up, w_down, b_down):
    shp = h.shape
    t = h.reshape(-1, shp[-1])
    logits = (t @ router_w + router_b).astype(jnp.float32)
    top_v, top_i = lax.top_k(logits, TOP_K)
    probs = jax.nn.softmax(top_v, axis=-1)
    gates = jnp.sum(jax.nn.one_hot(top_i, N_EXPERTS, dtype=jnp.float32) * probs[..., None], axis=1).astype(t.dtype)
    out = jnp.zeros_like(t)
    for e in range(N_EXPERTS):
        u = t @ w_up[e] + b_up[e]
        glu, lin = jnp.split(u, 2, axis=-1)
        glu = jnp.minimum(glu, SWIGLU_LIMIT)
        lin = jnp.clip(lin, -SWIGLU_LIMIT, SWIGLU_LIMIT)
        act = glu * jax.nn.sigmoid(SWIGLU_ALPHA * glu) * (lin + 1.0)
        out = out + gates[:, e:e + 1] * (act @ w_down[e] + b_down[e])
    return out.reshape(shp)


def setup_inputs(seed: int = 0) -> dict:
    key = jax.random.key(seed)
    ks = jax.random.split(key, 32)
    f32 = jnp.float32
    nrm = lambda k, shp, s: jax.random.normal(k, shp, f32) * s
    D = D_MODEL
    base_logit = jnp.log(2.0 ** (5.0 + jnp.arange(RET_HEADS, dtype=f32)) - 1.0)
    return {
        'x': nrm(ks[0], (BATCH, SEQ, D), 1.0),
        'c': nrm(ks[1], (BATCH, D), 1.0),
        'ctx': nrm(ks[2], (BATCH, CTX_LEN, D), 1.0),
        'c_ctx': nrm(ks[3], (D,), 0.5),
        'ret_w_in': nrm(ks[4], (N_RET_LAYERS, D, RET_IN_COLS), D ** -0.5),
        'ret_decay_logit': base_logit[None, None, :] + nrm(ks[5], (N_RET_LAYERS, 2, RET_HEADS), 0.1),
        'ret_gn': 1.0 + nrm(ks[6], (N_RET_LAYERS, 2, RET_HEADS * RET_DV), 0.02),
        'ret_w_out': nrm(ks[7], (N_RET_LAYERS, RET_HEADS * RET_DV, D), (RET_HEADS * RET_DV) ** -0.5),
        'mla_w_in': nrm(ks[8], (N_MLA_LAYERS, D, MLA_IN_COLS), D ** -0.5),
        'mla_q_norm': 1.0 + nrm(ks[9], (N_MLA_LAYERS, MLA_Q_LORA), 0.02),
        'mla_kv_norm': 1.0 + nrm(ks[10], (N_MLA_LAYERS, MLA_KV_LORA), 0.02),
        'mla_w_uq': nrm(ks[11], (N_MLA_LAYERS, MLA_Q_LORA, MLA_HEADS * (MLA_NOPE + MLA_ROPE)), MLA_Q_LORA ** -0.5),
        'mla_w_ukv': nrm(ks[12], (N_MLA_LAYERS, MLA_KV_LORA, MLA_HEADS * (MLA_NOPE + MLA_V)), MLA_KV_LORA ** -0.5),
        'mla_w_out': nrm(ks[13], (N_MLA_LAYERS, MLA_HEADS * MLA_V, D), (MLA_HEADS * MLA_V) ** -0.5),
        'ada_w': nrm(ks[14], (DEPTH, D, 6 * D), 0.5 * D ** -0.5),
        'ada_b': nrm(ks[15], (DEPTH, 6 * D), 0.02),
        'norm_mix': 1.0 + nrm(ks[16], (DEPTH, D), 0.02),
        'norm_ffn': 1.0 + nrm(ks[17], (DEPTH, D), 0.02),
        'router_w': nrm(ks[18], (DEPTH, D, N_EXPERTS), D ** -0.5),
        'router_b': nrm(ks[19], (DEPTH, N_EXPERTS), 0.01),
        'exp_w_up': nrm(ks[20], (DEPTH, N_EXPERTS, D, 2 * D_FF), D ** -0.5),
        'exp_b_up': nrm(ks[21], (DEPTH, N_EXPERTS, 2 * D_FF), 0.02),
        'exp_w_down': nrm(ks[22], (DEPTH, N_EXPERTS, D_FF, D), D_FF ** -0.5),
        'exp_b_down': nrm(ks[23], (DEPTH, N_EXPERTS, D), 0.02),
        'final_norm': 1.0 + nrm(ks[24], (D,), 0.02),
    }


def reference(x, c, ctx, c_ctx, ret_w_in, ret_decay_logit, ret_gn, ret_w_out, mla_w_in, mla_q_norm,
              mla_kv_norm, mla_w_uq, mla_w_ukv, mla_w_out, ada_w, ada_b, norm_mix, norm_ffn, router_w,
              router_b, exp_w_up, exp_b_up, exp_w_down, exp_b_down, final_norm):
    B, L, D = x.shape
    rows = L // GRID_W
    row = jnp.broadcast_to(jnp.arange(rows, dtype=jnp.float32)[:, None], (rows, GRID_W)).reshape(-1)
    col = jnp.broadcast_to(jnp.arange(GRID_W, dtype=jnp.float32)[None, :], (rows, GRID_W)).reshape(-1)
    n_ax = MLA_ROPE // 4
    ax_freq = ROPE_BASE ** (-jnp.arange(n_ax, dtype=jnp.float32) / n_ax)
    ang_mla = jnp.concatenate([row[:, None] * ax_freq, col[:, None] * ax_freq], axis=-1)
    ret_theta = 1.0 / (RET_THETA_BASE ** jnp.linspace(0.0, 1.0, RET_DK // 2, dtype=jnp.float32))
    ang_ret = jnp.arange(L, dtype=jnp.float32)[:, None] * ret_theta[None, :]

    sc_lat = jax.nn.silu(c)
    sc_ctx = jax.nn.silu(c_ctx)
    x_lat, x_ctx = x, ctx
    for layer in range(DEPTH):
        last = layer == DEPTH - 1
        j = layer // N_MIXERS
        mod_l = jnp.split((sc_lat @ ada_w[layer] + ada_b[layer])[:, None, :], 6, axis=-1)
        mod_c = jnp.split(sc_ctx @ ada_w[layer] + ada_b[layer], 6, axis=-1)
        sh1_l, s1_l, g1_l, sh2_l, s2_l, g2_l = mod_l
        sh1_c, s1_c, g1_c, sh2_c, s2_c, g2_c = mod_c

        h_lat = modulate(rmsnorm(x_lat, norm_mix[layer]), sh1_l, s1_l)
        h_ctx = modulate(rmsnorm(x_ctx, norm_mix[layer]), sh1_c, s1_c)
        if layer % N_MIXERS == 0:
            y_ctx, y_lat = retention_mixer(h_ctx, h_lat, ret_w_in[j], ret_decay_logit[j], ret_gn[j],
                                           ret_w_out[j], ang_ret, not last)
        else:
            y_ctx, y_lat = mla_mixer(h_ctx, h_lat, mla_w_in[j], mla_q_norm[j], mla_kv_norm[j], mla_w_uq[j],
                                     mla_w_ukv[j], mla_w_out[j], ang_mla, not last)
        x_lat = x_lat + g1_l * y_lat

        h_lat = modulate(rmsnorm(x_lat, norm_ffn[layer]), sh2_l, s2_l)
        moe_args = (router_w[layer], router_b[layer], exp_w_up[layer], exp_b_up[layer],
                    exp_w_down[layer], exp_b_down[layer])
        if last:
            x_lat = x_lat + g2_l * moe_ffn(h_lat, *moe_args)
        else:
            x_ctx = x_ctx + g1_c * y_ctx
            h_ctx = modulate(rmsnorm(x_ctx, norm_ffn[layer]), sh2_c, s2_c)
            f = moe_ffn(jnp.concatenate([h_ctx, h_lat], axis=1), *moe_args)
            x_ctx = x_ctx + g2_c * f[:, :CTX_LEN]
            x_lat = x_lat + g2_l * f[:, CTX_LEN:]
    return rmsnorm(x_lat, final_norm)
```

```python
import functools

import jax
import jax.numpy as jnp
import numpy as np
from jax import lax
from jax.experimental import pallas as pl
from jax.experimental.pallas import tpu as pltpu

F32 = jnp.float32
BF16 = jnp.bfloat16
HIGHEST = lax.Precision.HIGHEST

TM = 256
RET_HEADS = 4
RET_DK = 256
RET_DV = 512
RET_CHUNK = 128
RET_THETA_BASE = 10000.0
MLA_HEADS = 8
MLA_NOPE = 128
MLA_ROPE = 64
MLA_V = 128
MLA_Q_LORA = 384
MLA_KV_LORA = 256
MLA_HEAD_PAD = 256
ROPE_BASE = 10000.0
GRID_W = 64
N_EXPERTS = 32
TOP_K = 4
SWIGLU_ALPHA = 1.702
SWIGLU_LIMIT = 7.0
EPS = 1e-6
GN_EPS = 1e-5
ATT_TK = 512
VMEM_LIMIT = 48 * 1024 * 1024


def _cparams(sem):
    return pltpu.CompilerParams(dimension_semantics=sem, vmem_limit_bytes=VMEM_LIMIT)


def _norm_mod(x, g, shift, scale):
    ms = jnp.mean(x * x, axis=-1, keepdims=True)
    xn = x * lax.rsqrt(ms + EPS)
    return (xn * g) * (1.0 + scale) + shift


def _rms(x, g):
    ms = jnp.mean(x * x, axis=-1, keepdims=True)
    return x * lax.rsqrt(ms + EPS) * g


def _mod_row_map(tpb, nb):
    def f(i):
        return jnp.where(i % tpb == 0, nb, i // tpb)
    return f


def _ada_kernel(c_ref, w_ref, b_ref, o_ref):
    s = c_ref[...]
    s = s * jax.nn.sigmoid(s)
    o_ref[...] = jnp.dot(s, w_ref[...], precision=HIGHEST, preferred_element_type=F32) + b_ref[...]


def _ada_mods(c_rows, ada_w, ada_b):
    depth, d, d6 = ada_w.shape
    nt = d6 // d
    out = pl.pallas_call(
        _ada_kernel,
        grid=(depth, nt),
        in_specs=[
            pl.BlockSpec((8, d), lambda l, n: (0, 0)),
            pl.BlockSpec((None, d, d), lambda l, n: (l, 0, n)),
            pl.BlockSpec((None, 1, d), lambda l, n: (l, 0, n)),
        ],
        out_specs=pl.BlockSpec((None, 8, d), lambda l, n: (l, 0, n)),
        out_shape=jax.ShapeDtypeStruct((depth, 8, d6), F32),
        compiler_params=_cparams(("parallel", "parallel")),
        name="ada_mods",
    )(c_rows, ada_w, ada_b.reshape(depth, 1, d6))
    return out.reshape(depth, 8, nt, d)


def _ret_proj_kernel(x_ref, g_ref, mod_ref, w_ref, cos_ref, sin_ref, o_ref):
    n = pl.program_id(0)
    h = _norm_mod(x_ref[...], g_ref[...], mod_ref[0:1, :], mod_ref[1:2, :]).astype(BF16)
    p = jnp.dot(h, w_ref[...], preferred_element_type=F32)

    @pl.when(n == 0)
    def _():
        cos = cos_ref[...]
        sin = sin_ref[...]
        half = RET_DK // 2
        for which in range(2):
            sc = 1.0 if which == 0 else RET_DK ** -0.5
            for hd in range(RET_HEADS):
                base = which * RET_HEADS * RET_DK + hd * RET_DK
                x1 = p[:, base:base + half]
                x2 = p[:, base + half:base + RET_DK]
                o_ref[:, base:base + half] = ((x1 * cos - x2 * sin) * sc).astype(BF16)
                o_ref[:, base + half:base + RET_DK] = ((x1 * sin + x2 * cos) * sc).astype(BF16)

    @pl.when(n != 0)
    def _():
        o_ref[...] = p.astype(BF16)


def _ret_proj(x, g, mods, w_bf, cos_t, sin_t, tpb, nb):
    t, d = x.shape
    ncols = w_bf.shape[1]
    tn = 2 * RET_HEADS * RET_DK
    row = _mod_row_map(tpb, nb)
    return pl.pallas_call(
        _ret_proj_kernel,
        grid=(ncols // tn, t // TM),
        in_specs=[
            pl.BlockSpec((TM, d), lambda n, i: (i, 0)),
            pl.BlockSpec((1, d), lambda n, i: (0, 0)),
            pl.BlockSpec((None, 6, d), lambda n, i: (row(i), 0, 0)),
            pl.BlockSpec((d, tn), lambda n, i: (0, n)),
            pl.BlockSpec((TM, RET_DK // 2), lambda n, i: (i, 0)),
            pl.BlockSpec((TM, RET_DK // 2), lambda n, i: (i, 0)),
        ],
        out_specs=pl.BlockSpec((TM, tn), lambda n, i: (i, n)),
        out_shape=jax.ShapeDtypeStruct((t, ncols), BF16),
        compiler_params=_cparams(("parallel", "parallel")),
        name="ret_proj",
    )(x, g, mods, w_bf, cos_t, sin_t)


def _log_sigmoid(x):
    return jnp.minimum(x, 0.0) - jnp.log1p(jnp.exp(-jnp.abs(x)))


def _ret_direction(q_ref, k_ref, v_ref, o_ref, s_ref, logit, forward):
    c = RET_CHUNK
    lg_row = _log_sigmoid(logit + jnp.zeros((1, c), F32))
    lg = lg_row[:, 0:1]
    i_col = lax.broadcasted_iota(jnp.int32, (c, 1), 0).astype(F32)
    i_mat = lax.broadcasted_iota(jnp.int32, (c, c), 0).astype(F32)
    j_mat = lax.broadcasted_iota(jnp.int32, (c, c), 1).astype(F32)
    if forward:
        rel = i_mat - j_mat
        q_pow = i_col + 1.0
        k_pow = (c - 1.0) - i_col
    else:
        rel = j_mat - i_mat
        q_pow = c - i_col
        k_pow = i_col
    mask = jnp.where(rel >= 0, jnp.exp(lg_row * jnp.maximum(rel, 0.0)), 0.0)
    q_decay = jnp.exp(lg * q_pow)
    k_decay = jnp.exp(lg * k_pow)
    chunk_decay = jnp.exp(lg * float(c))

    q = q_ref[...]
    k = k_ref[...]
    v = v_ref[...]
    s = s_ref[...]
    scores = lax.dot_general(q, k, (((1,), (1,)), ((), ())), preferred_element_type=F32) * mask
    qs = (q.astype(F32) * q_decay).astype(BF16)
    o = (jnp.dot(scores.astype(BF16), v, preferred_element_type=F32)
         + jnp.dot(qs, s.astype(BF16), preferred_element_type=F32))
    ks = (k.astype(F32) * k_decay).astype(BF16)
    s_ref[...] = s * chunk_decay + lax.dot_general(ks, v, (((0,), (0,)), ((), ())),
                                                   preferred_element_type=F32)
    o_ref[...] = o.astype(BF16)


def _ret_core_kernel(dl_ref, qf, kf, vf, qb, kb, vb, of_ref, ob_ref, sf, sb):
    hd = pl.program_id(1)

    @pl.when(pl.program_id(2) == 0)
    def _():
        sf[...] = jnp.zeros_like(sf)
        sb[...] = jnp.zeros_like(sb)

    _ret_direction(qf, kf, vf, of_ref, sf, dl_ref[0, hd], True)
    _ret_direction(qb, kb, vb, ob_ref, sb, dl_ref[1, hd], False)


def _ret_core(p, decay_logit, batch, nb_tok):
    t = p.shape[0]
    nc = nb_tok // RET_CHUNK
    nctx = TM // RET_CHUNK
    kq = RET_HEADS
    kv = 2 * RET_HEADS * RET_DK // RET_DV

    def cf(b, h, s):
        return b * nc + s

    def cb(b, h, s):
        return b * nc + jnp.where(s < nctx, nctx - 1 - s, nc - 1 + nctx - s)

    def spec(width, off, cmap):
        return pl.BlockSpec((RET_CHUNK, width), lambda b, h, s: (cmap(b, h, s), off + h))

    out_f = pl.BlockSpec((RET_CHUNK, RET_DV), lambda b, h, s: (cf(b, h, s), h))
    out_b = pl.BlockSpec((RET_CHUNK, RET_DV), lambda b, h, s: (cb(b, h, s), h))
    return pl.pallas_call(
        _ret_core_kernel,
        grid=(batch, RET_HEADS, nc),
        in_specs=[
            pl.BlockSpec(memory_space=pltpu.SMEM),
            spec(RET_DK, 0, cf), spec(RET_DK, kq, cf), spec(RET_DV, kv, cf),
            spec(RET_DK, 0, cb), spec(RET_DK, kq, cb), spec(RET_DV, kv, cb),
        ],
        out_specs=[out_f, out_b],
        out_shape=[jax.ShapeDtypeStruct((t, RET_HEADS * RET_DV), BF16)] * 2,
        scratch_shapes=[pltpu.VMEM((RET_DK, RET_DV), F32), pltpu.VMEM((RET_DK, RET_DV), F32)],
        compiler_params=_cparams(("parallel", "parallel", "arbitrary")),
        name="ret_core",
    )(decay_logit, p, p, p, p, p, p)


def _group_norm_head(o):
    mu = jnp.mean(o, axis=-1, keepdims=True)
    dlt = o - mu
    var = jnp.mean(dlt * dlt, axis=-1, keepdims=True)
    return dlt * lax.rsqrt(var + GN_EPS)


def _ret_out_kernel(x_ref, mod_ref, of_ref, ob_ref, gf_ref, gb_ref, gn_ref, w_ref, o_ref):
    acc = jnp.zeros(x_ref.shape, F32)
    for hd in range(RET_HEADS):
        sl = slice(hd * RET_DV, (hd + 1) * RET_DV)
        gf = gf_ref[:, sl].astype(F32)
        gb = gb_ref[:, sl].astype(F32)
        yf = _group_norm_head(of_ref[:, sl].astype(F32)) * gn_ref[0:1, sl]
        yb = _group_norm_head(ob_ref[:, sl].astype(F32)) * gn_ref[1:2, sl]
        y = gf * jax.nn.sigmoid(gf) * yf + gb * jax.nn.sigmoid(gb) * yb
        acc = acc + jnp.dot(y.astype(BF16), w_ref[sl, :], preferred_element_type=F32)
    o_ref[...] = x_ref[...] + mod_ref[2:3, :] * acc


def _ret_out(x, mods, o_f, o_b, p, gn, w_bf, tpb, nb):
    t, d = x.shape
    hv = RET_HEADS * RET_DV
    row = _mod_row_map(tpb, nb)
    gf_blk = (2 * RET_HEADS * RET_DK + hv) // hv
    return pl.pallas_call(
        _ret_out_kernel,
        grid=(t // TM,),
        in_specs=[
            pl.BlockSpec((TM, d), lambda i: (i, 0)),
            pl.BlockSpec((None, 6, d), lambda i: (row(i), 0, 0)),
            pl.BlockSpec((TM, hv), lambda i: (i, 0)),
            pl.BlockSpec((TM, hv), lambda i: (i, 0)),
            pl.BlockSpec((TM, hv), lambda i: (i, gf_blk)),
            pl.BlockSpec((TM, hv), lambda i: (i, gf_blk + 1)),
            pl.BlockSpec((2, hv), lambda i: (0, 0)),
            pl.BlockSpec((hv, d), lambda i: (0, 0)),
        ],
        out_specs=pl.BlockSpec((TM, d), lambda i: (i, 0)),
        out_shape=jax.ShapeDtypeStruct((t, d), F32),
        compiler_params=_cparams(("parallel",)),
        name="ret_out",
    )(x, mods, o_f, o_b, p, p, gn, w_bf)


def _mla_proj_kernel(x_ref, g_ref, mod_ref, wcq_ref, wckv_ref, wkr_ref, qn_ref, kvn_ref,
                     wuq_ref, wuk_ref, wuv_ref, c_ref, s1_ref, s2_ref, q_out, k_out, v_out):
    h = _norm_mod(x_ref[...], g_ref[...], mod_ref[0:1, :], mod_ref[1:2, :]).astype(BF16)
    cq = jnp.dot(h, wcq_ref[...], preferred_element_type=F32)
    ckv = jnp.dot(h, wckv_ref[...], preferred_element_type=F32)
    kr = jnp.dot(h, wkr_ref[...], preferred_element_type=F32)
    cqn = _rms(cq, qn_ref[...]).astype(BF16)
    ckvn = _rms(ckv, kvn_ref[...]).astype(BF16)
    q = jnp.dot(cqn, wuq_ref[...], preferred_element_type=F32)
    kn = jnp.dot(ckvn, wuk_ref[...], preferred_element_type=F32)
    v = jnp.dot(ckvn, wuv_ref[...], preferred_element_type=F32)
    cc = c_ref[...]
    s1 = s1_ref[...]
    s2 = s2_ref[...]
    half = MLA_ROPE // 2

    def rot(tile):
        return (tile * cc + pltpu.roll(tile, 128 - half, 1) * s1 + pltpu.roll(tile, half, 1) * s2)

    scale = (MLA_NOPE + MLA_ROPE) ** -0.5
    krr = rot(kr).astype(BF16)
    for hd in range(MLA_HEADS):
        b0 = hd * MLA_HEAD_PAD
        q_out[:, b0:b0 + 128] = (q[:, b0:b0 + 128] * scale).astype(BF16)
        q_out[:, b0 + 128:b0 + 256] = (rot(q[:, b0 + 128:b0 + 256]) * scale).astype(BF16)
        k_out[:, b0:b0 + 128] = kn[:, hd * MLA_NOPE:(hd + 1) * MLA_NOPE].astype(BF16)
        k_out[:, b0 + 128:b0 + 256] = krr
    v_out[...] = v.astype(BF16)


def _mla_proj(x, g, mods, wts, tabs, tpb, nb):
    t, d = x.shape
    row = _mod_row_map(tpb, nb)
    wcq, wckv, wkr, qn, kvn, wuq, wuk, wuv = wts
    full = lambda a: pl.BlockSpec(a.shape, lambda i: (0,) * a.ndim)
    hq = MLA_HEADS * MLA_HEAD_PAD
    hv = MLA_HEADS * MLA_V
    return pl.pallas_call(
        _mla_proj_kernel,
        grid=(t // TM,),
        in_specs=[
            pl.BlockSpec((TM, d), lambda i: (i, 0)),
            pl.BlockSpec((1, d), lambda i: (0, 0)),
            pl.BlockSpec((None, 6, d), lambda i: (row(i), 0, 0)),
            full(wcq), full(wckv), full(wkr), full(qn), full(kvn), full(wuq), full(wuk), full(wuv),
            pl.BlockSpec((TM, 128), lambda i: (i, 0)),
            pl.BlockSpec((TM, 128), lambda i: (i, 0)),
            pl.BlockSpec((TM, 128), lambda i: (i, 0)),
        ],
        out_specs=[
            pl.BlockSpec((TM, hq), lambda i: (i, 0)),
            pl.BlockSpec((TM, hq), lambda i: (i, 0)),
            pl.BlockSpec((TM, hv), lambda i: (i, 0)),
        ],
        out_shape=[
            jax.ShapeDtypeStruct((t, hq), BF16),
            jax.ShapeDtypeStruct((t, hq), BF16),
            jax.ShapeDtypeStruct((t, hv), BF16),
        ],
        compiler_params=_cparams(("parallel",)),
        name="mla_proj",
    )(x, g, mods, wcq, wckv, wkr, qn, kvn, wuq, wuk, wuv, *tabs)


def _attn_kernel(q_ref, k_ref, v_ref, o_ref, *, n_lat_tiles):
    qi = pl.program_id(2)
    q = q_ref[...]

    def step(kt, vt, carry):
        m, l, acc = carry
        s = lax.dot_general(q, kt, (((1,), (1,)), ((), ())), preferred_element_type=F32)
        m_new = jnp.maximum(m, jnp.max(s, axis=-1, keepdims=True))
        a = jnp.exp(m - m_new)
        p = jnp.exp(s - m_new)
        l = a * l + jnp.sum(p, axis=-1, keepdims=True)
        acc = a * acc + jnp.dot(p.astype(BF16), vt, preferred_element_type=F32)
        return m_new, l, acc

    carry = (jnp.full((TM, 1), -jnp.inf, F32), jnp.zeros((TM, 1), F32), jnp.zeros((TM, MLA_V), F32))
    carry = step(k_ref[0:TM, :], v_ref[0:TM, :], carry)

    def body(j, c):
        off = pl.multiple_of(TM + j * ATT_TK, ATT_TK // 2)
        return step(k_ref[pl.ds(off, ATT_TK), :], v_ref[pl.ds(off, ATT_TK), :], c)

    n = jnp.where(qi > 0, n_lat_tiles, 0)
    m, l, acc = lax.fori_loop(0, n, body, carry)
    o_ref[...] = (acc / l).astype(BF16)


def _attention(q, k, v, batch, nb_tok):
    t = q.shape[0]
    tpb = nb_tok // TM
    n_lat_tiles = (nb_tok - TM) // ATT_TK
    return pl.pallas_call(
        functools.partial(_attn_kernel, n_lat_tiles=n_lat_tiles),
        grid=(batch, MLA_HEADS, tpb),
        in_specs=[
            pl.BlockSpec((TM, MLA_HEAD_PAD), lambda b, h, i: (b * tpb + i, h)),
            pl.BlockSpec((nb_tok, MLA_HEAD_PAD), lambda b, h, i: (b, h)),
            pl.BlockSpec((nb_tok, MLA_V), lambda b, h, i: (b, h)),
        ],
        out_specs=pl.BlockSpec((TM, MLA_V), lambda b, h, i: (b * tpb + i, h)),
        out_shape=jax.ShapeDtypeStruct((t, MLA_HEADS * MLA_V), BF16),
        compiler_params=_cparams(("parallel", "parallel", "arbitrary")),
        name="mla_attn",
    )(q, k, v)


def _attn_out_kernel(x_ref, mod_ref, o_ref, w_ref, out_ref):
    y = jnp.dot(o_ref[...], w_ref[...], preferred_element_type=F32)
    out_ref[...] = x_ref[...] + mod_ref[2:3, :] * y


def _attn_out(x, mods, o, w_bf, tpb, nb):
    t, d = x.shape
    row = _mod_row_map(tpb, nb)
    return pl.pallas_call(
        _attn_out_kernel,
        grid=(t // TM,),
        in_specs=[
            pl.BlockSpec((TM, d), lambda i: (i, 0)),
            pl.BlockSpec((None, 6, d), lambda i: (row(i), 0, 0)),
            pl.BlockSpec((TM, o.shape[1]), lambda i: (i, 0)),
            pl.BlockSpec(w_bf.shape, lambda i: (0, 0)),
        ],
        out_specs=pl.BlockSpec((TM, d), lambda i: (i, 0)),
        out_shape=jax.ShapeDtypeStruct((t, d), F32),
        compiler_params=_cparams(("parallel",)),
        name="attn_out",
    )(x, mods, o, w_bf)


def _router_kernel(x_ref, g_ref, mod_ref, rw_ref, rb_ref,
                   h_out, eid_out, gate_out, rank_out, cnt_out, carry):
    @pl.when(pl.program_id(0) == 0)
    def _():
        carry[...] = jnp.zeros_like(carry)

    h = _norm_mod(x_ref[...], g_ref[...], mod_ref[3:4, :], mod_ref[4:5, :])
    h_out[...] = h
    logits = lax.dot_general(rw_ref[...], h, (((1,), (1,)), ((), ())), precision=HIGHEST,
                             preferred_element_type=F32) + rb_ref[...]
    e_iota = lax.broadcasted_iota(jnp.int32, (N_EXPERTS, TM), 0).astype(F32)
    work = logits
    vals, idxs, hots = [], [], []
    for _ in range(TOP_K):
        m = jnp.max(work, axis=0, keepdims=True)
        idx = jnp.min(jnp.where(work == m, e_iota, float(N_EXPERTS)), axis=0, keepdims=True)
        hot = e_iota == idx
        vals.append(m)
        idxs.append(idx)
        hots.append(hot)
        work = jnp.where(hot, -jnp.inf, work)
    exps = [jnp.exp(vv - vals[0]) for vv in vals]
    den = exps[0]
    for e in exps[1:]:
        den = den + e
    mask = jnp.zeros((N_EXPERTS, TM), F32)
    for hot in hots:
        mask = mask + jnp.where(hot, 1.0, 0.0)
    r_i = lax.broadcasted_iota(jnp.int32, (TM, TM), 0)
    c_i = lax.broadcasted_iota(jnp.int32, (TM, TM), 1)
    upper = jnp.where(r_i < c_i, 1.0, 0.0).astype(BF16)
    prefix = jnp.dot(mask.astype(BF16), upper, preferred_element_type=F32) + carry[:, 0:1]
    for kk in range(TOP_K):
        eid_out[kk:kk + 1, :] = idxs[kk].astype(jnp.int32)
        gate_out[kk:kk + 1, :] = exps[kk] / den
        rank = jnp.sum(jnp.where(hots[kk], prefix, 0.0), axis=0, keepdims=True)
        rank_out[kk:kk + 1, :] = rank.astype(jnp.int32)
    new_carry = carry[...] + jnp.sum(mask, axis=1, keepdims=True)
    carry[...] = new_carry
    cnt_out[...] = new_carry


def _router(x, g, mods, rw_t, rb, tpb, nb):
    t, d = x.shape
    row = _mod_row_map(tpb, nb)
    tok_row = lambda dt: jax.ShapeDtypeStruct((TOP_K, t), dt)
    return pl.pallas_call(
        _router_kernel,
        grid=(t // TM,),
        in_specs=[
            pl.BlockSpec((TM, d), lambda i: (i, 0)),
            pl.BlockSpec((1, d), lambda i: (0, 0)),
            pl.BlockSpec((None, 6, d), lambda i: (row(i), 0, 0)),
            pl.BlockSpec((N_EXPERTS, d), lambda i: (0, 0)),
            pl.BlockSpec((N_EXPERTS, 1), lambda i: (0, 0)),
        ],
        out_specs=[
            pl.BlockSpec((TM, d), lambda i: (i, 0)),
            pl.BlockSpec((TOP_K, TM), lambda i: (0, i)),
            pl.BlockSpec((TOP_K, TM), lambda i: (0, i)),
            pl.BlockSpec((TOP_K, TM), lambda i: (0, i)),
            pl.BlockSpec((N_EXPERTS, 128), lambda i: (0, 0)),
        ],
        out_shape=[
            jax.ShapeDtypeStruct((t, d), F32),
            tok_row(jnp.int32), tok_row(F32), tok_row(jnp.int32),
            jax.ShapeDtypeStruct((N_EXPERTS, 128), F32),
        ],
        scratch_shapes=[pltpu.VMEM((N_EXPERTS, 128), F32)],
        compiler_params=_cparams(("arbitrary",)),
        name="router",
    )(x, g, mods, rw_t, rb)


def _dispatch_kernel(pos_ref, h_ref, hs_in, hs_out, sem):
    del hs_in

    def row_copy(tok, slot):
        return pltpu.make_async_copy(h_ref.at[pl.ds(tok, 1), :], hs_out.at[pl.ds(slot, 1), :], sem)

    def issue(tok, _):
        for kk in range(TOP_K):
            row_copy(tok, pos_ref[0, kk * TM + tok]).start()
        return 0

    lax.fori_loop(0, TM, issue, 0)

    def drain(tok, _):
        for kk in range(TOP_K):
            row_copy(tok, pos_ref[0, kk * TM + tok]).wait()
        return 0

    lax.fori_loop(0, TM, drain, 0)


def _dispatch(h, pos_tiles, s_pad):
    t, d = h.shape
    zeros = jnp.zeros((s_pad, d), F32)
    return pl.pallas_call(
        _dispatch_kernel,
        grid=(t // TM,),
        in_specs=[
            pl.BlockSpec((None, 1, TOP_K * TM), lambda i: (i, 0, 0), memory_space=pltpu.SMEM),
            pl.BlockSpec((TM, d), lambda i: (i, 0)),
            pl.BlockSpec(memory_space=pl.ANY),
        ],
        out_specs=pl.BlockSpec(memory_space=pl.ANY),
        out_shape=jax.ShapeDtypeStruct((s_pad, d), F32),
        scratch_shapes=[pltpu.SemaphoreType.DMA(())],
        input_output_aliases={2: 0},
        compiler_params=_cparams(("arbitrary",)),
        name="moe_dispatch",
    )(pos_tiles, h, zeros)


def _expert_kernel(te_ref, nu_ref, hs_ref, wu_ref, bu_ref, wd_ref, bd_ref, ys_ref):
    i = pl.program_id(0)
    dff = wd_ref.shape[0]

    @pl.when(i < nu_ref[0])
    def _():
        h = hs_ref[...].astype(BF16)
        u = jnp.dot(h, wu_ref[...], preferred_element_type=F32) + bu_ref[...]
        glu = jnp.minimum(u[:, :dff], SWIGLU_LIMIT)
        lin = jnp.clip(u[:, dff:], -SWIGLU_LIMIT, SWIGLU_LIMIT)
        act = glu * jax.nn.sigmoid(SWIGLU_ALPHA * glu) * (lin + 1.0)
        ys_ref[...] = jnp.dot(act.astype(BF16), wd_ref[...], preferred_element_type=F32) + bd_ref[...]

    @pl.when(i >= nu_ref[0])
    def _():
        ys_ref[...] = jnp.zeros_like(ys_ref)


def _experts(hs, tile_expert, n_used, w_up, b_up, w_down, b_down):
    s_pad, d = hs.shape
    dff = w_down.shape[1]
    grid_spec = pltpu.PrefetchScalarGridSpec(
        num_scalar_prefetch=2,
        grid=(s_pad // TM,),
        in_specs=[
            pl.BlockSpec((TM, d), lambda i, te, nu: (i, 0)),
            pl.BlockSpec((None, d, 2 * dff), lambda i, te, nu: (te[i], 0, 0)),
            pl.BlockSpec((None, 1, 2 * dff), lambda i, te, nu: (te[i], 0, 0)),
            pl.BlockSpec((None, dff, d), lambda i, te, nu: (te[i], 0, 0)),
            pl.BlockSpec((None, 1, d), lambda i, te, nu: (te[i], 0, 0)),
        ],
        out_specs=pl.BlockSpec((TM, d), lambda i, te, nu: (i, 0)),
    )
    return pl.pallas_call(
        _expert_kernel,
        grid_spec=grid_spec,
        out_shape=jax.ShapeDtypeStruct((s_pad, d), F32),
        compiler_params=_cparams(("arbitrary",)),
        name="moe_experts",
    )(tile_expert, n_used, hs, w_up, b_up, w_down, b_down)


def _combine_kernel(pos_ref, x_ref, mod_ref, gate_ref, ys_ref, o_ref, ybuf, sem):
    def row_copy(tok, kk):
        slot = pos_ref[0, kk * TM + tok]
        return pltpu.make_async_copy(ys_ref.at[pl.ds(slot, 1), :], ybuf.at[kk, pl.ds(tok, 1), :], sem)

    def issue(tok, _):
        for kk in range(TOP_K):
            row_copy(tok, kk).start()
        return 0

    lax.fori_loop(0, TM, issue, 0)

    def drain(tok, _):
        for kk in range(TOP_K):
            row_copy(tok, kk).wait()
        return 0

    lax.fori_loop(0, TM, drain, 0)

    acc = gate_ref[:, 0:1] * ybuf[0]
    for kk in range(1, TOP_K):
        acc = acc + gate_ref[:, kk:kk + 1] * ybuf[kk]
    o_ref[...] = x_ref[...] + mod_ref[5:6, :] * acc


def _combine(x, mods, gates_t, pos_tiles, ys, tpb, nb):
    t, d = x.shape
    row = _mod_row_map(tpb, nb)
    return pl.pallas_call(
        _combine_kernel,
        grid=(t // TM,),
        in_specs=[
            pl.BlockSpec((None, 1, TOP_K * TM), lambda i: (i, 0, 0), memory_space=pltpu.SMEM),
            pl.BlockSpec((TM, d), lambda i: (i, 0)),
            pl.BlockSpec((None, 6, d), lambda i: (row(i), 0, 0)),
            pl.BlockSpec((TM, TOP_K), lambda i: (i, 0)),
            pl.BlockSpec(memory_space=pl.ANY),
        ],
        out_specs=pl.BlockSpec((TM, d), lambda i: (i, 0)),
        out_shape=jax.ShapeDtypeStruct((t, d), F32),
        scratch_shapes=[pltpu.VMEM((TOP_K, TM, d), F32), pltpu.SemaphoreType.DMA(())],
        compiler_params=_cparams(("arbitrary",)),
        name="moe_combine",
    )(pos_tiles, x, mods, gates_t, ys)


def _moe_layer(x, g, mods, rw_t, rb, w_up, b_up, w_down, b_down, tpb, nb):
    t, d = x.shape
    n_tok_tiles = t // TM
    s_pad = t * TOP_K + N_EXPERTS * TM
    h, eid, gate, rank, cnt = _router(x, g, mods, rw_t, rb, tpb, nb)
    counts = cnt[:, 0].astype(jnp.int32)
    padded = ((counts + TM - 1) // TM) * TM
    ends = jnp.cumsum(padded)
    offs = ends - padded
    pos = offs[eid] + rank
    pos_tiles = pos.reshape(TOP_K, n_tok_tiles, TM).transpose(1, 0, 2).reshape(n_tok_tiles, 1, TOP_K * TM)
    tile_start = jnp.arange(s_pad // TM, dtype=jnp.int32) * TM
    tile_expert = jnp.minimum(jnp.searchsorted(ends, tile_start, side="right"), N_EXPERTS - 1).astype(jnp.int32)
    n_used = (ends[-1] // TM).astype(jnp.int32).reshape(1)
    hs = _dispatch(h, pos_tiles, s_pad)
    ys = _experts(hs, tile_expert, n_used, w_up, b_up, w_down, b_down)
    return _combine(x, mods, gate.T, pos_tiles, ys, tpb, nb)


def _final_kernel(x_ref, g_ref, o_ref):
    o_ref[...] = _rms(x_ref[...], g_ref[...])


def _final_norm(x, g, batch, tpb):
    t, d = x.shape
    lt = tpb - 1
    return pl.pallas_call(
        _final_kernel,
        grid=(batch, lt),
        in_specs=[
            pl.BlockSpec((TM, d), lambda b, j: (b * tpb + 1 + j, 0)),
            pl.BlockSpec((1, d), lambda b, j: (0, 0)),
        ],
        out_specs=pl.BlockSpec((TM, d), lambda b, j: (b * lt + j, 0)),
        out_shape=jax.ShapeDtypeStruct((batch * lt * TM, d), F32),
        compiler_params=_cparams(("parallel", "parallel")),
        name="final_norm",
    )(x, g)


def _position_tables(batch, seq, ctx_len):
    rows = seq // GRID_W
    row = jnp.broadcast_to(jnp.arange(rows, dtype=F32)[:, None], (rows, GRID_W)).reshape(-1)
    col = jnp.broadcast_to(jnp.arange(GRID_W, dtype=F32)[None, :], (rows, GRID_W)).reshape(-1)
    n_ax = MLA_ROPE // 4
    ax_freq = ROPE_BASE ** (-jnp.arange(n_ax, dtype=F32) / n_ax)
    ang_mla = jnp.concatenate([row[:, None] * ax_freq, col[:, None] * ax_freq], axis=-1)
    ret_theta = 1.0 / (RET_THETA_BASE ** jnp.linspace(0.0, 1.0, RET_DK // 2, dtype=F32))
    ang_ret = jnp.arange(seq, dtype=F32)[:, None] * ret_theta[None, :]

    def with_ctx(lat, ctx_val):
        ctx = jnp.full((ctx_len, lat.shape[1]), ctx_val, F32)
        return jnp.tile(jnp.concatenate([ctx, lat], axis=0), (batch, 1))

    ret_cos = with_ctx(jnp.cos(ang_ret), 1.0)
    ret_sin = with_ctx(jnp.sin(ang_ret), 0.0)
    cm, sm = jnp.cos(ang_mla), jnp.sin(ang_mla)
    z = jnp.zeros_like(cm)
    mla_c = with_ctx(jnp.concatenate([cm, cm, z, z], axis=-1), 1.0)
    mla_s1 = with_ctx(jnp.concatenate([-sm, z, z, z], axis=-1), 0.0)
    mla_s2 = with_ctx(jnp.concatenate([z, sm, z, z], axis=-1), 0.0)
    return ret_cos, ret_sin, (mla_c, mla_s1, mla_s2)


def _mla_weights(w_in, q_norm, kv_norm, w_uq, w_ukv):
    d = w_in.shape[0]
    wcq = w_in[:, :MLA_Q_LORA].astype(BF16)
    wckv = w_in[:, MLA_Q_LORA:MLA_Q_LORA + MLA_KV_LORA].astype(BF16)
    wkr = jnp.pad(w_in[:, MLA_Q_LORA + MLA_KV_LORA:], ((0, 0), (0, 128 - MLA_ROPE))).astype(BF16)
    wuq = w_uq.reshape(MLA_Q_LORA, MLA_HEADS, MLA_NOPE + MLA_ROPE)
    wuq = jnp.pad(wuq, ((0, 0), (0, 0), (0, MLA_HEAD_PAD - MLA_NOPE - MLA_ROPE)))
    wuq = wuq.reshape(MLA_Q_LORA, MLA_HEADS * MLA_HEAD_PAD).astype(BF16)
    wukv = w_ukv.reshape(MLA_KV_LORA, MLA_HEADS, MLA_NOPE + MLA_V)
    wuk = wukv[:, :, :MLA_NOPE].reshape(MLA_KV_LORA, MLA_HEADS * MLA_NOPE).astype(BF16)
    wuv = wukv[:, :, MLA_NOPE:].reshape(MLA_KV_LORA, MLA_HEADS * MLA_V).astype(BF16)
    del d
    return (wcq, wckv, wkr, q_norm.reshape(1, -1), kv_norm.reshape(1, -1), wuq, wuk, wuv)


def kernel(x, c, ctx, c_ctx, ret_w_in, ret_decay_logit, ret_gn, ret_w_out, mla_w_in, mla_q_norm,
           mla_kv_norm, mla_w_uq, mla_w_ukv, mla_w_out, ada_w, ada_b, norm_mix, norm_ffn, router_w,
           router_b, exp_w_up, exp_b_up, exp_w_down, exp_b_down, final_norm):
    batch, seq, d = x.shape
    ctx_len = ctx.shape[1]
    depth = ada_w.shape[0]
    assert ctx_len == TM and seq % ATT_TK == 0 and batch + 1 <= 8
    nb_tok = ctx_len + seq
    tpb = nb_tok // TM

    xs = jnp.concatenate([ctx, x], axis=1).reshape(batch * nb_tok, d)
    c_rows = jnp.concatenate([c, c_ctx[None, :], jnp.zeros((8 - batch - 1, d), F32)], axis=0)
    mods_all = _ada_mods(c_rows, ada_w, ada_b)
    ret_cos, ret_sin, mla_tabs = _position_tables(batch, seq, ctx_len)

    for layer in range(depth):
        j = layer // 2
        mods = mods_all[layer]
        g_mix = norm_mix[layer].reshape(1, d)
        if layer % 2 == 0:
            p = _ret_proj(xs, g_mix, mods, ret_w_in[j].astype(BF16), ret_cos, ret_sin, tpb, batch)
            o_f, o_b = _ret_core(p, ret_decay_logit[j], batch, nb_tok)
            xs = _ret_out(xs, mods, o_f, o_b, p, ret_gn[j], ret_w_out[j].astype(BF16), tpb, batch)
        else:
            wts = _mla_weights(mla_w_in[j], mla_q_norm[j], mla_kv_norm[j], mla_w_uq[j], mla_w_ukv[j])
            q, k, v = _mla_proj(xs, g_mix, mods, wts, mla_tabs, tpb, batch)
            o = _attention(q, k, v, batch, nb_tok)
            xs = _attn_out(xs, mods, o, mla_w_out[j].astype(BF16), tpb, batch)
        xs = _moe_layer(xs, norm_ffn[layer].reshape(1, d), mods, router_w[layer].T,
                        router_b[layer].reshape(-1, 1), exp_w_up[layer].astype(BF16),
                        exp_b_up[layer].reshape(N_EXPERTS, 1, -1), exp_w_down[layer].astype(BF16),
                        exp_b_down[layer].reshape(N_EXPERTS, 1, -1), tpb, batch)

    out = _final_norm(xs, final_norm.reshape(1, d), batch, tpb)
    return out.reshape(batch, seq, d)
```

```python
import functools

import jax
import jax.numpy as jnp
import numpy as np
from jax import lax
from jax.experimental import pallas as pl
from jax.experimental.pallas import tpu as pltpu

F32 = jnp.float32
BF16 = jnp.bfloat16
HIGHEST = lax.Precision.HIGHEST

TM = 256
RET_HEADS = 4
RET_DK = 256
RET_DV = 512
RET_CHUNK = 128
RET_THETA_BASE = 10000.0
MLA_HEADS = 8
MLA_NOPE = 128
MLA_ROPE = 64
MLA_V = 128
MLA_Q_LORA = 384
MLA_KV_LORA = 256
MLA_HEAD_PAD = 256
ROPE_BASE = 10000.0
GRID_W = 64
N_EXPERTS = 32
TOP_K = 4
SWIGLU_ALPHA = 1.702
SWIGLU_LIMIT = 7.0
EPS = 1e-6
GN_EPS = 1e-5
ATT_TK = 512
ATT_QT = 4
QK_SCALE = float((MLA_NOPE + MLA_ROPE) ** -0.5 * np.log2(np.e))
VMEM_LIMIT = 48 * 1024 * 1024


def _cparams(sem):
    return pltpu.CompilerParams(dimension_semantics=sem, vmem_limit_bytes=VMEM_LIMIT)


def _norm_mod(x, g, shift, scale):
    ms = jnp.mean(x * x, axis=-1, keepdims=True)
    xn = x * lax.rsqrt(ms + EPS)
    return (xn * g) * (1.0 + scale) + shift


def _rms(x, g):
    ms = jnp.mean(x * x, axis=-1, keepdims=True)
    return x * lax.rsqrt(ms + EPS) * g


def _mod_row_map(tpb, nb):
    def f(i):
        return jnp.where(i % tpb == 0, nb, i // tpb)
    return f


def _ada_kernel(c_ref, w_ref, b_ref, o_ref):
    s = c_ref[...]
    s = s * jax.nn.sigmoid(s)
    o_ref[...] = jnp.dot(s, w_ref[...], precision=HIGHEST, preferred_element_type=F32) + b_ref[...]


def _ada_mods(c_rows, ada_w, ada_b):
    depth, d, d6 = ada_w.shape
    nt = d6 // d
    out = pl.pallas_call(
        _ada_kernel,
        grid=(depth, nt),
        in_specs=[
            pl.BlockSpec((8, d), lambda l, n: (0, 0)),
            pl.BlockSpec((None, d, d), lambda l, n: (l, 0, n)),
            pl.BlockSpec((None, 1, d), lambda l, n: (l, 0, n)),
        ],
        out_specs=pl.BlockSpec((None, 8, d), lambda l, n: (l, 0, n)),
        out_shape=jax.ShapeDtypeStruct((depth, 8, d6), F32),
        compiler_params=_cparams(("parallel", "parallel")),
        name="ada_mods",
    )(c_rows, ada_w, ada_b.reshape(depth, 1, d6))
    return out.reshape(depth, 8, nt, d)


def _ret_proj_kernel(x_ref, g_ref, mod_ref, w_ref, cos_ref, sin_ref, o_ref):
    n = pl.program_id(0)
    h = _norm_mod(x_ref[...], g_ref[...], mod_ref[0:1, :], mod_ref[1:2, :]).astype(BF16)
    p = jnp.dot(h, w_ref[...], preferred_element_type=F32)

    @pl.when(n == 0)
    def _():
        cos = cos_ref[...]
        sin = sin_ref[...]
        half = RET_DK // 2
        for which in range(2):
            sc = 1.0 if which == 0 else RET_DK ** -0.5
            for hd in range(RET_HEADS):
                base = which * RET_HEADS * RET_DK + hd * RET_DK
                x1 = p[:, base:base + half]
                x2 = p[:, base + half:base + RET_DK]
                o_ref[:, base:base + half] = ((x1 * cos - x2 * sin) * sc).astype(BF16)
                o_ref[:, base + half:base + RET_DK] = ((x1 * sin + x2 * cos) * sc).astype(BF16)

    @pl.when(n != 0)
    def _():
        o_ref[...] = p.astype(BF16)


def _ret_proj(x, g, mods, w_bf, cos_t, sin_t, tpb, nb):
    t, d = x.shape
    ncols = w_bf.shape[1]
    tn = 2 * RET_HEADS * RET_DK
    row = _mod_row_map(tpb, nb)
    return pl.pallas_call(
        _ret_proj_kernel,
        grid=(ncols // tn, t // TM),
        in_specs=[
            pl.BlockSpec((TM, d), lambda n, i: (i, 0)),
            pl.BlockSpec((1, d), lambda n, i: (0, 0)),
            pl.BlockSpec((None, 6, d), lambda n, i: (row(i), 0, 0)),
            pl.BlockSpec((d, tn), lambda n, i: (0, n)),
            pl.BlockSpec((TM, RET_DK // 2), lambda n, i: (i, 0)),
            pl.BlockSpec((TM, RET_DK // 2), lambda n, i: (i, 0)),
        ],
        out_specs=pl.BlockSpec((TM, tn), lambda n, i: (i, n)),
        out_shape=jax.ShapeDtypeStruct((t, ncols), BF16),
        compiler_params=_cparams(("parallel", "parallel")),
        name="ret_proj",
    )(x, g, mods, w_bf, cos_t, sin_t)


def _log_sigmoid(x):
    return jnp.minimum(x, 0.0) - jnp.log1p(jnp.exp(-jnp.abs(x)))


def _ret_direction(q_ref, k_ref, v_ref, o_ref, s_ref, logit, forward):
    c = RET_CHUNK
    lg_row = _log_sigmoid(logit + jnp.zeros((1, c), F32))
    lg = lg_row[:, 0:1]
    i_col = lax.broadcasted_iota(jnp.int32, (c, 1), 0).astype(F32)
    i_mat = lax.broadcasted_iota(jnp.int32, (c, c), 0).astype(F32)
    j_mat = lax.broadcasted_iota(jnp.int32, (c, c), 1).astype(F32)
    if forward:
        rel = i_mat - j_mat
        q_pow = i_col + 1.0
        k_pow = (c - 1.0) - i_col
    else:
        rel = j_mat - i_mat
        q_pow = c - i_col
        k_pow = i_col
    mask = jnp.where(rel >= 0, jnp.exp(lg_row * jnp.maximum(rel, 0.0)), 0.0)
    q_decay = jnp.exp(lg * q_pow)
    k_decay = jnp.exp(lg * k_pow)
    chunk_decay = jnp.exp(lg * float(c))

    q = q_ref[...]
    k = k_ref[...]
    v = v_ref[...]
    s = s_ref[...]
    scores = lax.dot_general(q, k, (((1,), (1,)), ((), ())), preferred_element_type=F32) * mask
    qs = (q.astype(F32) * q_decay).astype(BF16)
    o = (jnp.dot(scores.astype(BF16), v, preferred_element_type=F32)
         + jnp.dot(qs, s.astype(BF16), preferred_element_type=F32))
    ks = (k.astype(F32) * k_decay).astype(BF16)
    s_ref[...] = s * chunk_decay + lax.dot_general(ks, v, (((0,), (0,)), ((), ())),
                                                   preferred_element_type=F32)
    o_ref[...] = o.astype(BF16)


def _ret_core_kernel(dl_ref, qf, kf, vf, qb, kb, vb, of_ref, ob_ref, sf, sb):
    hd = pl.program_id(1)

    @pl.when(pl.program_id(2) == 0)
    def _():
        sf[...] = jnp.zeros_like(sf)
        sb[...] = jnp.zeros_like(sb)

    _ret_direction(qf, kf, vf, of_ref, sf, dl_ref[0, hd], True)
    _ret_direction(qb, kb, vb, ob_ref, sb, dl_ref[1, hd], False)


def _ret_core(p, decay_logit, batch, nb_tok):
    t = p.shape[0]
    nc = nb_tok // RET_CHUNK
    nctx = TM // RET_CHUNK
    kq = RET_HEADS
    kv = 2 * RET_HEADS * RET_DK // RET_DV

    def cf(b, h, s):
        return b * nc + s

    def cb(b, h, s):
        return b * nc + jnp.where(s < nctx, nctx - 1 - s, nc - 1 + nctx - s)

    def spec(width, off, cmap):
        return pl.BlockSpec((RET_CHUNK, width), lambda b, h, s: (cmap(b, h, s), off + h))

    out_f = pl.BlockSpec((RET_CHUNK, RET_DV), lambda b, h, s: (cf(b, h, s), h))
    out_b = pl.BlockSpec((RET_CHUNK, RET_DV), lambda b, h, s: (cb(b, h, s), h))
    return pl.pallas_call(
        _ret_core_kernel,
        grid=(batch, RET_HEADS, nc),
        in_specs=[
            pl.BlockSpec(memory_space=pltpu.SMEM),
            spec(RET_DK, 0, cf), spec(RET_DK, kq, cf), spec(RET_DV, kv, cf),
            spec(RET_DK, 0, cb), spec(RET_DK, kq, cb), spec(RET_DV, kv, cb),
        ],
        out_specs=[out_f, out_b],
        out_shape=[jax.ShapeDtypeStruct((t, RET_HEADS * RET_DV), BF16)] * 2,
        scratch_shapes=[pltpu.VMEM((RET_DK, RET_DV), F32), pltpu.VMEM((RET_DK, RET_DV), F32)],
        compiler_params=_cparams(("parallel", "parallel", "arbitrary")),
        name="ret_core",
    )(decay_logit, p, p, p, p, p, p)


def _group_norm_head(o):
    mu = jnp.mean(o, axis=-1, keepdims=True)
    dlt = o - mu
    var = jnp.mean(dlt * dlt, axis=-1, keepdims=True)
    return dlt * lax.rsqrt(var + GN_EPS)


def _ret_out_kernel(x_ref, mod_ref, of_ref, ob_ref, gf_ref, gb_ref, gn_ref, w_ref, o_ref):
    acc = jnp.zeros(x_ref.shape, F32)
    for hd in range(RET_HEADS):
        sl = slice(hd * RET_DV, (hd + 1) * RET_DV)
        gf = gf_ref[:, sl].astype(F32)
        gb = gb_ref[:, sl].astype(F32)
        yf = _group_norm_head(of_ref[:, sl].astype(F32)) * gn_ref[0:1, sl]
        yb = _group_norm_head(ob_ref[:, sl].astype(F32)) * gn_ref[1:2, sl]
        y = gf * jax.nn.sigmoid(gf) * yf + gb * jax.nn.sigmoid(gb) * yb
        acc = acc + jnp.dot(y.astype(BF16), w_ref[sl, :], preferred_element_type=F32)
    o_ref[...] = x_ref[...] + mod_ref[2:3, :] * acc


def _ret_out(x, mods, o_f, o_b, p, gn, w_bf, tpb, nb):
    t, d = x.shape
    hv = RET_HEADS * RET_DV
    row = _mod_row_map(tpb, nb)
    gf_blk = (2 * RET_HEADS * RET_DK + hv) // hv
    return pl.pallas_call(
        _ret_out_kernel,
        grid=(t // TM,),
        in_specs=[
            pl.BlockSpec((TM, d), lambda i: (i, 0)),
            pl.BlockSpec((None, 6, d), lambda i: (row(i), 0, 0)),
            pl.BlockSpec((TM, hv), lambda i: (i, 0)),
            pl.BlockSpec((TM, hv), lambda i: (i, 0)),
            pl.BlockSpec((TM, hv), lambda i: (i, gf_blk)),
            pl.BlockSpec((TM, hv), lambda i: (i, gf_blk + 1)),
            pl.BlockSpec((2, hv), lambda i: (0, 0)),
            pl.BlockSpec((hv, d), lambda i: (0, 0)),
        ],
        out_specs=pl.BlockSpec((TM, d), lambda i: (i, 0)),
        out_shape=jax.ShapeDtypeStruct((t, d), F32),
        compiler_params=_cparams(("parallel",)),
        name="ret_out",
    )(x, mods, o_f, o_b, p, p, gn, w_bf)


def _mla_proj_kernel(x_ref, g_ref, mod_ref, wcq_ref, wckv_ref, wkr_ref, qn_ref, kvn_ref,
                     wuqt_ref, wuk_ref, wuvt_ref, c_ref, s1_ref, s2_ref, ct_ref, st_ref,
                     qt_out, k_out, vt_out):
    h = _norm_mod(x_ref[...], g_ref[...], mod_ref[0:1, :], mod_ref[1:2, :]).astype(BF16)
    cq = jnp.dot(h, wcq_ref[...], preferred_element_type=F32)
    ckv = jnp.dot(h, wckv_ref[...], preferred_element_type=F32)
    kr = jnp.dot(h, wkr_ref[...], preferred_element_type=F32)
    cqn = _rms(cq, qn_ref[...]).astype(BF16)
    ckvn = _rms(ckv, kvn_ref[...]).astype(BF16)
    nt = (((1,), (1,)), ((), ()))
    qt = lax.dot_general(wuqt_ref[...], cqn, nt, preferred_element_type=F32)
    vt = lax.dot_general(wuvt_ref[...], ckvn, nt, preferred_element_type=F32)
    kn = jnp.dot(ckvn, wuk_ref[...], preferred_element_type=F32)
    half = MLA_ROPE // 2
    krr = (kr * c_ref[...] + pltpu.roll(kr, 128 - half, 1) * s1_ref[...]
           + pltpu.roll(kr, half, 1) * s2_ref[...]).astype(BF16)
    ct = ct_ref[...]
    st = st_ref[...]
    for hd in range(MLA_HEADS):
        b0 = hd * MLA_HEAD_PAD
        r0 = b0 + MLA_NOPE
        qt_out[b0:r0, :] = (qt[b0:r0, :] * QK_SCALE).astype(BF16)
        x1 = qt[r0:r0 + half, :]
        x2 = qt[r0 + half:r0 + 2 * half, :]
        qt_out[r0:r0 + half, :] = ((x1 * ct - x2 * st) * QK_SCALE).astype(BF16)
        qt_out[r0 + half:r0 + 2 * half, :] = ((x1 * st + x2 * ct) * QK_SCALE).astype(BF16)
        qt_out[r0 + 2 * half:b0 + MLA_HEAD_PAD, :] = jnp.zeros(
            (MLA_HEAD_PAD - MLA_NOPE - MLA_ROPE, TM), BF16)
        k_out[:, b0:r0] = kn[:, hd * MLA_NOPE:(hd + 1) * MLA_NOPE].astype(BF16)
        k_out[:, r0:b0 + MLA_HEAD_PAD] = krr
    vt_out[...] = vt.astype(BF16)


def _mla_proj(x, g, mods, wts, tabs, tpb, nb):
    t, d = x.shape
    row = _mod_row_map(tpb, nb)
    wcq, wckv, wkr, qn, kvn, wuqt, wuk, wuvt = wts
    full = lambda a: pl.BlockSpec(a.shape, lambda i: (0,) * a.ndim)
    hq = MLA_HEADS * MLA_HEAD_PAD
    hv = MLA_HEADS * MLA_V
    tok_cols = lambda r: pl.BlockSpec((r, TM), lambda i: (0, i))
    return pl.pallas_call(
        _mla_proj_kernel,
        grid=(t // TM,),
        in_specs=[
            pl.BlockSpec((TM, d), lambda i: (i, 0)),
            pl.BlockSpec((1, d), lambda i: (0, 0)),
            pl.BlockSpec((None, 6, d), lambda i: (row(i), 0, 0)),
            full(wcq), full(wckv), full(wkr), full(qn), full(kvn), full(wuqt), full(wuk), full(wuvt),
            pl.BlockSpec((TM, 128), lambda i: (i, 0)),
            pl.BlockSpec((TM, 128), lambda i: (i, 0)),
            pl.BlockSpec((TM, 128), lambda i: (i, 0)),
            tok_cols(MLA_ROPE // 2), tok_cols(MLA_ROPE // 2),
        ],
        out_specs=[tok_cols(hq), pl.BlockSpec((TM, hq), lambda i: (i, 0)), tok_cols(hv)],
        out_shape=[
            jax.ShapeDtypeStruct((hq, t), BF16),
            jax.ShapeDtypeStruct((t, hq), BF16),
            jax.ShapeDtypeStruct((hv, t), BF16),
        ],
        compiler_params=_cparams(("parallel",)),
        name="mla_proj",
    )(x, g, mods, wcq, wckv, wkr, qn, kvn, wuqt, wuk, wuvt, *tabs)


def _scores(kt, qt):
    return jnp.dot(kt, qt, preferred_element_type=F32)


def _softmax_step(s, vtt, carry):
    m, l, acc = carry
    m_new = jnp.maximum(m, jnp.max(s, axis=0, keepdims=True))
    a = jnp.exp2(m - m_new)
    p = jnp.exp2(s - m_new)
    l = a * l + jnp.sum(p, axis=0, keepdims=True)
    acc = a * acc + jnp.dot(vtt, p.astype(BF16), preferred_element_type=F32)
    return m_new, l, acc


def _softmax_init():
    return (jnp.full((1, TM), -jnp.inf, F32), jnp.zeros((1, TM), F32), jnp.zeros((MLA_V, TM), F32))


def _attn_lat_kernel(*refs, n_lat_tiles):
    qt_refs = refs[:ATT_QT]
    k_ref, vt_ref, o_ref = refs[ATT_QT:ATT_QT + 3]
    scr = refs[ATT_QT + 3:]
    s_a, s_b, acc_s = scr[:ATT_QT], scr[ATT_QT:2 * ATT_QT], scr[2 * ATT_QT:]
    chains = range(ATT_QT)

    def k_tile(j):
        return k_ref[pl.ds(pl.multiple_of(TM + j * ATT_TK, ATT_TK // 2), ATT_TK), :]

    def vt_tile(j):
        return vt_ref[:, pl.ds(pl.multiple_of(TM + j * ATT_TK, ATT_TK // 2), ATT_TK)]

    def put_scores(j, dst):
        kt = k_tile(j)
        for c in chains:
            dst[c][...] = _scores(kt, qt_refs[c][...])

    def softmax(src, j, ml):
        vtt = vt_tile(j)
        out = []
        for c in chains:
            m, l, acc = _softmax_step(src[c][...], vtt, (ml[c][0], ml[c][1], acc_s[c][...]))
            acc_s[c][...] = acc
            out.append((m, l))
        return tuple(out)

    ctx_scores = tuple(_scores(k_ref[0:TM, :], qt_refs[c][...]) for c in chains)
    ml = []
    for c in chains:
        m, l, acc = _softmax_step(ctx_scores[c], vt_ref[:, 0:TM], _softmax_init())
        acc_s[c][...] = acc
        ml.append((m, l))
    ml = tuple(ml)

    put_scores(0, s_a)

    def body(i, ml):
        j = 2 * i
        put_scores(j + 1, s_b)
        ml = softmax(s_a, j, ml)
        put_scores(j + 2, s_a)
        return softmax(s_b, j + 1, ml)

    ml = lax.fori_loop(0, n_lat_tiles // 2 - 1, body, ml)
    put_scores(n_lat_tiles - 1, s_b)
    ml = softmax(s_a, n_lat_tiles - 2, ml)
    ml = softmax(s_b, n_lat_tiles - 1, ml)
    for c in chains:
        o_ref[c * TM:(c + 1) * TM, :] = (acc_s[c][...] / ml[c][1]).T.astype(BF16)


def _attn_ctx_kernel(qt_ref, k_ref, vt_ref, o_ref):
    m, l, acc = _softmax_step(_scores(k_ref[...], qt_ref[...]), vt_ref[...], _softmax_init())
    o_ref[...] = (acc / l).T.astype(BF16)


def _attention(qt, k, vt, batch, nb_tok):
    tpb = nb_tok // TM
    lat_tok = nb_tok - TM
    n_lat_tiles = lat_tok // ATT_TK
    qb = ATT_QT * TM
    nq = lat_tok // qb
    hv = MLA_HEADS * MLA_V

    def q_spec(c):
        return pl.BlockSpec((MLA_HEAD_PAD, TM), lambda b, h, i: (h, b * tpb + 1 + ATT_QT * i + c))

    o_lat = pl.pallas_call(
        functools.partial(_attn_lat_kernel, n_lat_tiles=n_lat_tiles),
        grid=(batch, MLA_HEADS, nq),
        in_specs=[q_spec(c) for c in range(ATT_QT)] + [
            pl.BlockSpec((nb_tok, MLA_HEAD_PAD), lambda b, h, i: (b, h)),
            pl.BlockSpec((MLA_V, nb_tok), lambda b, h, i: (h, b)),
        ],
        out_specs=pl.BlockSpec((qb, MLA_V), lambda b, h, i: (b * nq + i, h)),
        out_shape=jax.ShapeDtypeStruct((batch * lat_tok, hv), BF16),
        scratch_shapes=([pltpu.VMEM((ATT_TK, TM), F32)] * (2 * ATT_QT)
                        + [pltpu.VMEM((MLA_V, TM), F32)] * ATT_QT),
        compiler_params=_cparams(("parallel", "parallel", "arbitrary")),
        name="mla_attn",
    )(*([qt] * ATT_QT), k, vt)
    o_ctx = pl.pallas_call(
        _attn_ctx_kernel,
        grid=(batch, MLA_HEADS),
        in_specs=[
            pl.BlockSpec((MLA_HEAD_PAD, TM), lambda b, h: (h, b * tpb)),
            pl.BlockSpec((TM, MLA_HEAD_PAD), lambda b, h: (b * tpb, h)),
            pl.BlockSpec((MLA_V, TM), lambda b, h: (h, b * tpb)),
        ],
        out_specs=pl.BlockSpec((TM, MLA_V), lambda b, h: (b, h)),
        out_shape=jax.ShapeDtypeStruct((batch * TM, hv), BF16),
        compiler_params=_cparams(("parallel", "parallel")),
        name="mla_attn_ctx",
    )(qt, k, vt)
    return o_ctx, o_lat


def _attn_out_kernel(x_ref, mod_ref, oc_ref, ol_ref, w_ref, out_ref, *, tpb):
    is_ctx = pl.program_id(0) % tpb == 0

    @pl.when(is_ctx)
    def _():
        y = jnp.dot(oc_ref[...], w_ref[...], preferred_element_type=F32)
        out_ref[...] = x_ref[...] + mod_ref[2:3, :] * y

    @pl.when(jnp.logical_not(is_ctx))
    def _():
        y = jnp.dot(ol_ref[...], w_ref[...], preferred_element_type=F32)
        out_ref[...] = x_ref[...] + mod_ref[2:3, :] * y


def _attn_out(x, mods, o_ctx, o_lat, w_bf, tpb, nb):
    t, d = x.shape
    row = _mod_row_map(tpb, nb)
    hv = o_lat.shape[1]
    lt = tpb - 1
    return pl.pallas_call(
        functools.partial(_attn_out_kernel, tpb=tpb),
        grid=(t // TM,),
        in_specs=[
            pl.BlockSpec((TM, d), lambda i: (i, 0)),
            pl.BlockSpec((None, 6, d), lambda i: (row(i), 0, 0)),
            pl.BlockSpec((TM, hv), lambda i: (i // tpb, 0)),
            pl.BlockSpec((TM, hv), lambda i: ((i // tpb) * lt + jnp.maximum(i % tpb - 1, 0), 0)),
            pl.BlockSpec(w_bf.shape, lambda i: (0, 0)),
        ],
        out_specs=pl.BlockSpec((TM, d), lambda i: (i, 0)),
        out_shape=jax.ShapeDtypeStruct((t, d), F32),
        compiler_params=_cparams(("parallel",)),
        name="attn_out",
    )(x, mods, o_ctx, o_lat, w_bf)


def _router_kernel(x_ref, g_ref, mod_ref, rw_ref, rb_ref,
                   h_out, eid_out, gate_out, rank_out, cnt_out, carry):
    @pl.when(pl.program_id(0) == 0)
    def _():
        carry[...] = jnp.zeros_like(carry)

    h = _norm_mod(x_ref[...], g_ref[...], mod_ref[3:4, :], mod_ref[4:5, :])
    h_out[...] = h
    logits = lax.dot_general(rw_ref[...], h, (((1,), (1,)), ((), ())), precision=HIGHEST,
                             preferred_element_type=F32) + rb_ref[...]
    e_iota = lax.broadcasted_iota(jnp.int32, (N_EXPERTS, TM), 0).astype(F32)
    work = logits
    vals, idxs, hots = [], [], []
    for _ in range(TOP_K):
        m = jnp.max(work, axis=0, keepdims=True)
        idx = jnp.min(jnp.where(work == m, e_iota, float(N_EXPERTS)), axis=0, keepdims=True)
        hot = e_iota == idx
        vals.append(m)
        idxs.append(idx)
        hots.append(hot)
        work = jnp.where(hot, -jnp.inf, work)
    exps = [jnp.exp(vv - vals[0]) for vv in vals]
    den = exps[0]
    for e in exps[1:]:
        den = den + e
    mask = jnp.zeros((N_EXPERTS, TM), F32)
    for hot in hots:
        mask = mask + jnp.where(hot, 1.0, 0.0)
    r_i = lax.broadcasted_iota(jnp.int32, (TM, TM), 0)
    c_i = lax.broadcasted_iota(jnp.int32, (TM, TM), 1)
    upper = jnp.where(r_i < c_i, 1.0, 0.0).astype(BF16)
    prefix = jnp.dot(mask.astype(BF16), upper, preferred_element_type=F32) + carry[:, 0:1]
    for kk in range(TOP_K):
        eid_out[kk:kk + 1, :] = idxs[kk].astype(jnp.int32)
        gate_out[kk:kk + 1, :] = exps[kk] / den
        rank = jnp.sum(jnp.where(hots[kk], prefix, 0.0), axis=0, keepdims=True)
        rank_out[kk:kk + 1, :] = rank.astype(jnp.int32)
    new_carry = carry[...] + jnp.sum(mask, axis=1, keepdims=True)
    carry[...] = new_carry
    cnt_out[...] = new_carry


def _router(x, g, mods, rw_t, rb, tpb, nb):
    t, d = x.shape
    row = _mod_row_map(tpb, nb)
    tok_row = lambda dt: jax.ShapeDtypeStruct((TOP_K, t), dt)
    return pl.pallas_call(
        _router_kernel,
        grid=(t // TM,),
        in_specs=[
            pl.BlockSpec((TM, d), lambda i: (i, 0)),
            pl.BlockSpec((1, d), lambda i: (0, 0)),
            pl.BlockSpec((None, 6, d), lambda i: (row(i), 0, 0)),
            pl.BlockSpec((N_EXPERTS, d), lambda i: (0, 0)),
            pl.BlockSpec((N_EXPERTS, 1), lambda i: (0, 0)),
        ],
        out_specs=[
            pl.BlockSpec((TM, d), lambda i: (i, 0)),
            pl.BlockSpec((TOP_K, TM), lambda i: (0, i)),
            pl.BlockSpec((TOP_K, TM), lambda i: (0, i)),
            pl.BlockSpec((TOP_K, TM), lambda i: (0, i)),
            pl.BlockSpec((N_EXPERTS, 128), lambda i: (0, 0)),
        ],
        out_shape=[
            jax.ShapeDtypeStruct((t, d), F32),
            tok_row(jnp.int32), tok_row(F32), tok_row(jnp.int32),
            jax.ShapeDtypeStruct((N_EXPERTS, 128), F32),
        ],
        scratch_shapes=[pltpu.VMEM((N_EXPERTS, 128), F32)],
        compiler_params=_cparams(("arbitrary",)),
        name="router",
    )(x, g, mods, rw_t, rb)


def _dispatch_kernel(pos_ref, h_ref, hs_in, hs_out, sem):
    del hs_in

    def row_copy(tok, slot):
        return pltpu.make_async_copy(h_ref.at[pl.ds(tok, 1), :], hs_out.at[pl.ds(slot, 1), :], sem)

    def issue(tok, _):
        for kk in range(TOP_K):
            row_copy(tok, pos_ref[0, kk * TM + tok]).start()
        return 0

    lax.fori_loop(0, TM, issue, 0)

    def drain(tok, _):
        for kk in range(TOP_K):
            row_copy(tok, pos_ref[0, kk * TM + tok]).wait()
        return 0

    lax.fori_loop(0, TM, drain, 0)


def _dispatch(h, pos_tiles, s_pad):
    t, d = h.shape
    zeros = jnp.zeros((s_pad, d), F32)
    return pl.pallas_call(
        _dispatch_kernel,
        grid=(t // TM,),
        in_specs=[
            pl.BlockSpec((None, 1, TOP_K * TM), lambda i: (i, 0, 0), memory_space=pltpu.SMEM),
            pl.BlockSpec((TM, d), lambda i: (i, 0)),
            pl.BlockSpec(memory_space=pl.ANY),
        ],
        out_specs=pl.BlockSpec(memory_space=pl.ANY),
        out_shape=jax.ShapeDtypeStruct((s_pad, d), F32),
        scratch_shapes=[pltpu.SemaphoreType.DMA(())],
        input_output_aliases={2: 0},
        compiler_params=_cparams(("arbitrary",)),
        name="moe_dispatch",
    )(pos_tiles, h, zeros)


def _expert_kernel(te_ref, nu_ref, hs_ref, wu_ref, bu_ref, wd_ref, bd_ref, ys_ref):
    i = pl.program_id(0)
    dff = wd_ref.shape[0]

    @pl.when(i < nu_ref[0])
    def _():
        h = hs_ref[...].astype(BF16)
        u = jnp.dot(h, wu_ref[...], preferred_element_type=F32) + bu_ref[...]
        glu = jnp.minimum(u[:, :dff], SWIGLU_LIMIT)
        lin = jnp.clip(u[:, dff:], -SWIGLU_LIMIT, SWIGLU_LIMIT)
        act = glu * jax.nn.sigmoid(SWIGLU_ALPHA * glu) * (lin + 1.0)
        ys_ref[...] = jnp.dot(act.astype(BF16), wd_ref[...], preferred_element_type=F32) + bd_ref[...]

    @pl.when(i >= nu_ref[0])
    def _():
        ys_ref[...] = jnp.zeros_like(ys_ref)


def _experts(hs, tile_expert, n_used, w_up, b_up, w_down, b_down):
    s_pad, d = hs.shape
    dff = w_down.shape[1]
    grid_spec = pltpu.PrefetchScalarGridSpec(
        num_scalar_prefetch=2,
        grid=(s_pad // TM,),
        in_specs=[
            pl.BlockSpec((TM, d), lambda i, te, nu: (i, 0)),
            pl.BlockSpec((None, d, 2 * dff), lambda i, te, nu: (te[i], 0, 0)),
            pl.BlockSpec((None, 1, 2 * dff), lambda i, te, nu: (te[i], 0, 0)),
            pl.BlockSpec((None, dff, d), lambda i, te, nu: (te[i], 0, 0)),
            pl.BlockSpec((None, 1, d), lambda i, te, nu: (te[i], 0, 0)),
        ],
        out_specs=pl.BlockSpec((TM, d), lambda i, te, nu: (i, 0)),
    )
    return pl.pallas_call(
        _expert_kernel,
        grid_spec=grid_spec,
        out_shape=jax.ShapeDtypeStruct((s_pad, d), F32),
        compiler_params=_cparams(("arbitrary",)),
        name="moe_experts",
    )(tile_expert, n_used, hs, w_up, b_up, w_down, b_down)


def _combine_kernel(pos_ref, x_ref, mod_ref, gate_ref, ys_ref, o_ref, ybuf, sem):
    def row_copy(tok, kk):
        slot = pos_ref[0, kk * TM + tok]
        return pltpu.make_async_copy(ys_ref.at[pl.ds(slot, 1), :], ybuf.at[kk, pl.ds(tok, 1), :], sem)

    def issue(tok, _):
        for kk in range(TOP_K):
            row_copy(tok, kk).start()
        return 0

    lax.fori_loop(0, TM, issue, 0)

    def drain(tok, _):
        for kk in range(TOP_K):
            row_copy(tok, kk).wait()
        return 0

    lax.fori_loop(0, TM, drain, 0)

    acc = gate_ref[:, 0:1] * ybuf[0]
    for kk in range(1, TOP_K):
        acc = acc + gate_ref[:, kk:kk + 1] * ybuf[kk]
    o_ref[...] = x_ref[...] + mod_ref[5:6, :] * acc


def _combine(x, mods, gates_t, pos_tiles, ys, tpb, nb):
    t, d = x.shape
    row = _mod_row_map(tpb, nb)
    return pl.pallas_call(
        _combine_kernel,
        grid=(t // TM,),
        in_specs=[
            pl.BlockSpec((None, 1, TOP_K * TM), lambda i: (i, 0, 0), memory_space=pltpu.SMEM),
            pl.BlockSpec((TM, d), lambda i: (i, 0)),
            pl.BlockSpec((None, 6, d), lambda i: (row(i), 0, 0)),
            pl.BlockSpec((TM, TOP_K), lambda i: (i, 0)),
            pl.BlockSpec(memory_space=pl.ANY),
        ],
        out_specs=pl.BlockSpec((TM, d), lambda i: (i, 0)),
        out_shape=jax.ShapeDtypeStruct((t, d), F32),
        scratch_shapes=[pltpu.VMEM((TOP_K, TM, d), F32), pltpu.SemaphoreType.DMA(())],
        compiler_params=_cparams(("arbitrary",)),
        name="moe_combine",
    )(pos_tiles, x, mods, gates_t, ys)


def _moe_layer(x, g, mods, rw_t, rb, w_up, b_up, w_down, b_down, tpb, nb):
    t, d = x.shape
    n_tok_tiles = t // TM
    s_pad = t * TOP_K + N_EXPERTS * TM
    h, eid, gate, rank, cnt = _router(x, g, mods, rw_t, rb, tpb, nb)
    counts = cnt[:, 0].astype(jnp.int32)
    padded = ((counts + TM - 1) // TM) * TM
    ends = jnp.cumsum(padded)
    offs = ends - padded
    e_ids = jnp.arange(N_EXPERTS, dtype=jnp.int32)
    pos = jnp.sum(jnp.where(eid[..., None] == e_ids, offs, 0), axis=-1) + rank
    pos_tiles = pos.reshape(TOP_K, n_tok_tiles, TM).transpose(1, 0, 2).reshape(n_tok_tiles, 1, TOP_K * TM)
    tile_start = jnp.arange(s_pad // TM, dtype=jnp.int32) * TM
    tile_expert = jnp.sum((ends[None, :] <= tile_start[:, None]).astype(jnp.int32), axis=-1)
    tile_expert = jnp.minimum(tile_expert, N_EXPERTS - 1)
    n_used = (ends[-1] // TM).astype(jnp.int32).reshape(1)
    hs = _dispatch(h, pos_tiles, s_pad)
    ys = _experts(hs, tile_expert, n_used, w_up, b_up, w_down, b_down)
    return _combine(x, mods, gate.T, pos_tiles, ys, tpb, nb)


def _final_kernel(x_ref, g_ref, o_ref):
    o_ref[...] = _rms(x_ref[...], g_ref[...])


def _final_norm(x, g, batch, tpb):
    t, d = x.shape
    lt = tpb - 1
    return pl.pallas_call(
        _final_kernel,
        grid=(batch, lt),
        in_specs=[
            pl.BlockSpec((TM, d), lambda b, j: (b * tpb + 1 + j, 0)),
            pl.BlockSpec((1, d), lambda b, j: (0, 0)),
        ],
        out_specs=pl.BlockSpec((TM, d), lambda b, j: (b * lt + j, 0)),
        out_shape=jax.ShapeDtypeStruct((batch * lt * TM, d), F32),
        compiler_params=_cparams(("parallel", "parallel")),
        name="final_norm",
    )(x, g)


def _position_tables(batch, seq, ctx_len):
    rows = seq // GRID_W
    row = jnp.broadcast_to(jnp.arange(rows, dtype=F32)[:, None], (rows, GRID_W)).reshape(-1)
    col = jnp.broadcast_to(jnp.arange(GRID_W, dtype=F32)[None, :], (rows, GRID_W)).reshape(-1)
    n_ax = MLA_ROPE // 4
    ax_freq = ROPE_BASE ** (-jnp.arange(n_ax, dtype=F32) / n_ax)
    ang_mla = jnp.concatenate([row[:, None] * ax_freq, col[:, None] * ax_freq], axis=-1)
    ret_theta = 1.0 / (RET_THETA_BASE ** jnp.linspace(0.0, 1.0, RET_DK // 2, dtype=F32))
    ang_ret = jnp.arange(seq, dtype=F32)[:, None] * ret_theta[None, :]

    def with_ctx(lat, ctx_val):
        ctx = jnp.full((ctx_len, lat.shape[1]), ctx_val, F32)
        return jnp.tile(jnp.concatenate([ctx, lat], axis=0), (batch, 1))

    ret_cos = with_ctx(jnp.cos(ang_ret), 1.0)
    ret_sin = with_ctx(jnp.sin(ang_ret), 0.0)
    cm, sm = jnp.cos(ang_mla), jnp.sin(ang_mla)
    z = jnp.zeros_like(cm)
    mla_c = with_ctx(jnp.concatenate([cm, cm, z, z], axis=-1), 1.0)
    mla_s1 = with_ctx(jnp.concatenate([-sm, z, z, z], axis=-1), 0.0)
    mla_s2 = with_ctx(jnp.concatenate([z, sm, z, z], axis=-1), 0.0)
    mla_ct = with_ctx(cm, 1.0).T
    mla_st = with_ctx(sm, 0.0).T
    return ret_cos, ret_sin, (mla_c, mla_s1, mla_s2, mla_ct, mla_st)


def _mla_weights(w_in, q_norm, kv_norm, w_uq, w_ukv):
    d = w_in.shape[0]
    wcq = w_in[:, :MLA_Q_LORA].astype(BF16)
    wckv = w_in[:, MLA_Q_LORA:MLA_Q_LORA + MLA_KV_LORA].astype(BF16)
    wkr = jnp.pad(w_in[:, MLA_Q_LORA + MLA_KV_LORA:], ((0, 0), (0, 128 - MLA_ROPE))).astype(BF16)
    wuq = w_uq.reshape(MLA_Q_LORA, MLA_HEADS, MLA_NOPE + MLA_ROPE)
    wuq = jnp.pad(wuq, ((0, 0), (0, 0), (0, MLA_HEAD_PAD - MLA_NOPE - MLA_ROPE)))
    wuqt = wuq.reshape(MLA_Q_LORA, MLA_HEADS * MLA_HEAD_PAD).T.astype(BF16)
    wukv = w_ukv.reshape(MLA_KV_LORA, MLA_HEADS, MLA_NOPE + MLA_V)
    wuk = wukv[:, :, :MLA_NOPE].reshape(MLA_KV_LORA, MLA_HEADS * MLA_NOPE).astype(BF16)
    wuvt = wukv[:, :, MLA_NOPE:].reshape(MLA_KV_LORA, MLA_HEADS * MLA_V).T.astype(BF16)
    del d
    return (wcq, wckv, wkr, q_norm.reshape(1, -1), kv_norm.reshape(1, -1), wuqt, wuk, wuvt)


def kernel(x, c, ctx, c_ctx, ret_w_in, ret_decay_logit, ret_gn, ret_w_out, mla_w_in, mla_q_norm,
           mla_kv_norm, mla_w_uq, mla_w_ukv, mla_w_out, ada_w, ada_b, norm_mix, norm_ffn, router_w,
           router_b, exp_w_up, exp_b_up, exp_w_down, exp_b_down, final_norm):
    batch, seq, d = x.shape
    ctx_len = ctx.shape[1]
    depth = ada_w.shape[0]
    assert ctx_len == TM and seq % ATT_TK == 0 and seq % (ATT_QT * TM) == 0 and batch + 1 <= 8
    nb_tok = ctx_len + seq
    tpb = nb_tok // TM

    xs = jnp.concatenate([ctx, x], axis=1).reshape(batch * nb_tok, d)
    c_rows = jnp.concatenate([c, c_ctx[None, :], jnp.zeros((8 - batch - 1, d), F32)], axis=0)
    mods_all = _ada_mods(c_rows, ada_w, ada_b)
    ret_cos, ret_sin, mla_tabs = _position_tables(batch, seq, ctx_len)

    for layer in range(depth):
        j = layer // 2
        mods = mods_all[layer]
        g_mix = norm_mix[layer].reshape(1, d)
        if layer % 2 == 0:
            p = _ret_proj(xs, g_mix, mods, ret_w_in[j].astype(BF16), ret_cos, ret_sin, tpb, batch)
            o_f, o_b = _ret_core(p, ret_decay_logit[j], batch, nb_tok)
            xs = _ret_out(xs, mods, o_f, o_b, p, ret_gn[j], ret_w_out[j].astype(BF16), tpb, batch)
        else:
            wts = _mla_weights(mla_w_in[j], mla_q_norm[j], mla_kv_norm[j], mla_w_uq[j], mla_w_ukv[j])
            qt, k, vt = _mla_proj(xs, g_mix, mods, wts, mla_tabs, tpb, batch)
            o_ctx, o_lat = _attention(qt, k, vt, batch, nb_tok)
            xs = _attn_out(xs, mods, o_ctx, o_lat, mla_w_out[j].astype(BF16), tpb, batch)
        xs = _moe_layer(xs, norm_ffn[layer].reshape(1, d), mods, router_w[layer].T,
                        router_b[layer].reshape(-1, 1), exp_w_up[layer].astype(BF16),
                        exp_b_up[layer].reshape(N_EXPERTS, 1, -1), exp_w_down[layer].astype(BF16),
                        exp_b_down[layer].reshape(N_EXPERTS, 1, -1), tpb, batch)

    out = _final_norm(xs, final_norm.reshape(1, d), batch, tpb)
    return out.reshape(batch, seq, d)
```

```python
import functools

import jax
import jax.numpy as jnp
import numpy as np
from jax import lax
from jax.experimental import pallas as pl
from jax.experimental.pallas import tpu as pltpu

F32 = jnp.float32
BF16 = jnp.bfloat16
HIGHEST = lax.Precision.HIGHEST

TM = 256
LANES = 128
ROW_TILES = 8
RET_HEADS = 4
RET_DK = 256
RET_DV = 512
RET_CHUNK = 128
RET_THETA_BASE = 10000.0
MLA_HEADS = 8
MLA_NOPE = 128
MLA_ROPE = 64
MLA_V = 128
MLA_Q_LORA = 384
MLA_KV_LORA = 256
MLA_HEAD_PAD = 256
ROPE_BASE = 10000.0
GRID_W = 64
N_EXPERTS = 32
TOP_K = 4
SWIGLU_ALPHA = 1.702
SWIGLU_LIMIT = 7.0
EPS = 1e-6
GN_EPS = 1e-5
ATT_TK = 512
ATT_QT = 4
QK_SCALE = float((MLA_NOPE + MLA_ROPE) ** -0.5 * np.log2(np.e))
VMEM_LIMIT = 48 * 1024 * 1024


def _cparams(sem):
    return pltpu.CompilerParams(dimension_semantics=sem, vmem_limit_bytes=VMEM_LIMIT)


def _norm_mod(x, g, shift, scale):
    ms = jnp.mean(x * x, axis=-1, keepdims=True)
    xn = x * lax.rsqrt(ms + EPS)
    return (xn * g) * (1.0 + scale) + shift


def _rms(x, g):
    ms = jnp.mean(x * x, axis=-1, keepdims=True)
    return x * lax.rsqrt(ms + EPS) * g


def _mod_row_map(tpb, nb):
    def f(i):
        return jnp.where(i % tpb == 0, nb, i // tpb)
    return f


def _ada_kernel(c_ref, w_ref, b_ref, o_ref):
    s = c_ref[...]
    s = s * jax.nn.sigmoid(s)
    o_ref[...] = jnp.dot(s, w_ref[...], precision=HIGHEST, preferred_element_type=F32) + b_ref[...]


def _ada_mods(c_rows, ada_w, ada_b):
    depth, d, d6 = ada_w.shape
    nt = d6 // d
    out = pl.pallas_call(
        _ada_kernel,
        grid=(depth, nt),
        in_specs=[
            pl.BlockSpec((8, d), lambda l, n: (0, 0)),
            pl.BlockSpec((None, d, d), lambda l, n: (l, 0, n)),
            pl.BlockSpec((None, 1, d), lambda l, n: (l, 0, n)),
        ],
        out_specs=pl.BlockSpec((None, 8, d), lambda l, n: (l, 0, n)),
        out_shape=jax.ShapeDtypeStruct((depth, 8, d6), F32),
        compiler_params=_cparams(("parallel", "parallel")),
        name="ada_mods",
    )(c_rows, ada_w, ada_b.reshape(depth, 1, d6))
    return out.reshape(depth, 8, nt, d)


def _ret_proj_kernel(x_ref, g_ref, mod_ref, w_ref, cos_ref, sin_ref, o_ref):
    n = pl.program_id(0)
    h = _norm_mod(x_ref[...], g_ref[...], mod_ref[0:1, :], mod_ref[1:2, :]).astype(BF16)
    p = jnp.dot(h, w_ref[...], preferred_element_type=F32)

    @pl.when(n == 0)
    def _():
        cos = cos_ref[...]
        sin = sin_ref[...]
        half = RET_DK // 2
        for which in range(2):
            sc = 1.0 if which == 0 else RET_DK ** -0.5
            for hd in range(RET_HEADS):
                base = which * RET_HEADS * RET_DK + hd * RET_DK
                x1 = p[:, base:base + half]
                x2 = p[:, base + half:base + RET_DK]
                o_ref[:, base:base + half] = ((x1 * cos - x2 * sin) * sc).astype(BF16)
                o_ref[:, base + half:base + RET_DK] = ((x1 * sin + x2 * cos) * sc).astype(BF16)

    @pl.when(n != 0)
    def _():
        o_ref[...] = p.astype(BF16)


def _ret_proj(x, g, mods, w_bf, cos_t, sin_t, tpb, nb):
    t, d = x.shape
    ncols = w_bf.shape[1]
    tn = 2 * RET_HEADS * RET_DK
    row = _mod_row_map(tpb, nb)
    return pl.pallas_call(
        _ret_proj_kernel,
        grid=(ncols // tn, t // TM),
        in_specs=[
            pl.BlockSpec((TM, d), lambda n, i: (i, 0)),
            pl.BlockSpec((1, d), lambda n, i: (0, 0)),
            pl.BlockSpec((None, 6, d), lambda n, i: (row(i), 0, 0)),
            pl.BlockSpec((d, tn), lambda n, i: (0, n)),
            pl.BlockSpec((TM, RET_DK // 2), lambda n, i: (i, 0)),
            pl.BlockSpec((TM, RET_DK // 2), lambda n, i: (i, 0)),
        ],
        out_specs=pl.BlockSpec((TM, tn), lambda n, i: (i, n)),
        out_shape=jax.ShapeDtypeStruct((t, ncols), BF16),
        compiler_params=_cparams(("parallel", "parallel")),
        name="ret_proj",
    )(x, g, mods, w_bf, cos_t, sin_t)


def _log_sigmoid(x):
    return jnp.minimum(x, 0.0) - jnp.log1p(jnp.exp(-jnp.abs(x)))


def _ret_direction(q_ref, k_ref, v_ref, o_ref, s_ref, logit, forward):
    c = RET_CHUNK
    lg_row = _log_sigmoid(logit + jnp.zeros((1, c), F32))
    lg = lg_row[:, 0:1]
    i_col = lax.broadcasted_iota(jnp.int32, (c, 1), 0).astype(F32)
    i_mat = lax.broadcasted_iota(jnp.int32, (c, c), 0).astype(F32)
    j_mat = lax.broadcasted_iota(jnp.int32, (c, c), 1).astype(F32)
    if forward:
        rel = i_mat - j_mat
        q_pow = i_col + 1.0
        k_pow = (c - 1.0) - i_col
    else:
        rel = j_mat - i_mat
        q_pow = c - i_col
        k_pow = i_col
    mask = jnp.where(rel >= 0, jnp.exp(lg_row * jnp.maximum(rel, 0.0)), 0.0)
    q_decay = jnp.exp(lg * q_pow)
    k_decay = jnp.exp(lg * k_pow)
    chunk_decay = jnp.exp(lg * float(c))

    q = q_ref[...]
    k = k_ref[...]
    v = v_ref[...]
    s = s_ref[...]
    raw = lax.dot_general(q, k, (((1,), (1,)), ((), ())), preferred_element_type=F32)
    qs = (q.astype(F32) * q_decay).astype(BF16)
    inter = jnp.dot(qs, s.astype(BF16), preferred_element_type=F32)
    ks = (k.astype(F32) * k_decay).astype(BF16)
    upd = lax.dot_general(ks, v, (((0,), (0,)), ((), ())), preferred_element_type=F32)

    def finish():
        o = jnp.dot((raw * mask).astype(BF16), v, preferred_element_type=F32) + inter
        s_ref[...] = s * chunk_decay + upd
        o_ref[...] = o.astype(BF16)

    return finish


def _ret_core_kernel(dl_ref, qf, kf, vf, qb, kb, vb, of_ref, ob_ref, sf, sb):
    hd = pl.program_id(1)

    @pl.when(pl.program_id(2) == 0)
    def _():
        sf[...] = jnp.zeros_like(sf)
        sb[...] = jnp.zeros_like(sb)

    finish_f = _ret_direction(qf, kf, vf, of_ref, sf, dl_ref[0, hd], True)
    finish_b = _ret_direction(qb, kb, vb, ob_ref, sb, dl_ref[1, hd], False)
    finish_f()
    finish_b()


def _ret_core(p, decay_logit, batch, nb_tok):
    t = p.shape[0]
    nc = nb_tok // RET_CHUNK
    nctx = TM // RET_CHUNK
    kq = RET_HEADS
    kv = 2 * RET_HEADS * RET_DK // RET_DV

    def cf(b, h, s):
        return b * nc + s

    def cb(b, h, s):
        return b * nc + jnp.where(s < nctx, nctx - 1 - s, nc - 1 + nctx - s)

    def spec(width, off, cmap):
        return pl.BlockSpec((RET_CHUNK, width), lambda b, h, s: (cmap(b, h, s), off + h))

    out_f = pl.BlockSpec((RET_CHUNK, RET_DV), lambda b, h, s: (cf(b, h, s), h))
    out_b = pl.BlockSpec((RET_CHUNK, RET_DV), lambda b, h, s: (cb(b, h, s), h))
    return pl.pallas_call(
        _ret_core_kernel,
        grid=(batch, RET_HEADS, nc),
        in_specs=[
            pl.BlockSpec(memory_space=pltpu.SMEM),
            spec(RET_DK, 0, cf), spec(RET_DK, kq, cf), spec(RET_DV, kv, cf),
            spec(RET_DK, 0, cb), spec(RET_DK, kq, cb), spec(RET_DV, kv, cb),
        ],
        out_specs=[out_f, out_b],
        out_shape=[jax.ShapeDtypeStruct((t, RET_HEADS * RET_DV), BF16)] * 2,
        scratch_shapes=[pltpu.VMEM((RET_DK, RET_DV), F32), pltpu.VMEM((RET_DK, RET_DV), F32)],
        compiler_params=_cparams(("parallel", "parallel", "arbitrary")),
        name="ret_core",
    )(decay_logit, p, p, p, p, p, p)


def _group_norm_head(o):
    mu = jnp.mean(o, axis=-1, keepdims=True)
    dlt = o - mu
    var = jnp.mean(dlt * dlt, axis=-1, keepdims=True)
    return dlt * lax.rsqrt(var + GN_EPS)


def _ret_out_kernel(x_ref, mod_ref, of_ref, ob_ref, gf_ref, gb_ref, gn_ref, w_ref, o_ref):
    acc = jnp.zeros(x_ref.shape, F32)
    for hd in range(RET_HEADS):
        sl = slice(hd * RET_DV, (hd + 1) * RET_DV)
        gf = gf_ref[:, sl].astype(F32)
        gb = gb_ref[:, sl].astype(F32)
        yf = _group_norm_head(of_ref[:, sl].astype(F32)) * gn_ref[0:1, sl]
        yb = _group_norm_head(ob_ref[:, sl].astype(F32)) * gn_ref[1:2, sl]
        y = gf * jax.nn.sigmoid(gf) * yf + gb * jax.nn.sigmoid(gb) * yb
        acc = acc + jnp.dot(y.astype(BF16), w_ref[sl, :], preferred_element_type=F32)
    o_ref[...] = x_ref[...] + mod_ref[2:3, :] * acc


def _ret_out(x, mods, o_f, o_b, p, gn, w_bf, tpb, nb):
    t, d = x.shape
    hv = RET_HEADS * RET_DV
    row = _mod_row_map(tpb, nb)
    gf_blk = (2 * RET_HEADS * RET_DK + hv) // hv
    return pl.pallas_call(
        _ret_out_kernel,
        grid=(t // TM,),
        in_specs=[
            pl.BlockSpec((TM, d), lambda i: (i, 0)),
            pl.BlockSpec((None, 6, d), lambda i: (row(i), 0, 0)),
            pl.BlockSpec((TM, hv), lambda i: (i, 0)),
            pl.BlockSpec((TM, hv), lambda i: (i, 0)),
            pl.BlockSpec((TM, hv), lambda i: (i, gf_blk)),
            pl.BlockSpec((TM, hv), lambda i: (i, gf_blk + 1)),
            pl.BlockSpec((2, hv), lambda i: (0, 0)),
            pl.BlockSpec((hv, d), lambda i: (0, 0)),
        ],
        out_specs=pl.BlockSpec((TM, d), lambda i: (i, 0)),
        out_shape=jax.ShapeDtypeStruct((t, d), F32),
        compiler_params=_cparams(("parallel",)),
        name="ret_out",
    )(x, mods, o_f, o_b, p, p, gn, w_bf)


def _mla_proj_kernel(x_ref, g_ref, mod_ref, wcq_ref, wckv_ref, wkr_ref, qn_ref, kvn_ref,
                     wuqt_ref, wuk_ref, wuvt_ref, c_ref, s1_ref, s2_ref, ct_ref, st_ref,
                     qt_out, k_out, vt_out):
    h = _norm_mod(x_ref[...], g_ref[...], mod_ref[0:1, :], mod_ref[1:2, :]).astype(BF16)
    cq = jnp.dot(h, wcq_ref[...], preferred_element_type=F32)
    ckv = jnp.dot(h, wckv_ref[...], preferred_element_type=F32)
    kr = jnp.dot(h, wkr_ref[...], preferred_element_type=F32)
    cqn = _rms(cq, qn_ref[...]).astype(BF16)
    ckvn = _rms(ckv, kvn_ref[...]).astype(BF16)
    nt = (((1,), (1,)), ((), ()))
    qt = lax.dot_general(wuqt_ref[...], cqn, nt, preferred_element_type=F32)
    vt = lax.dot_general(wuvt_ref[...], ckvn, nt, preferred_element_type=F32)
    kn = jnp.dot(ckvn, wuk_ref[...], preferred_element_type=F32)
    half = MLA_ROPE // 2
    krr = (kr * c_ref[...] + pltpu.roll(kr, 128 - half, 1) * s1_ref[...]
           + pltpu.roll(kr, half, 1) * s2_ref[...]).astype(BF16)
    ct = ct_ref[...]
    st = st_ref[...]
    for hd in range(MLA_HEADS):
        b0 = hd * MLA_HEAD_PAD
        r0 = b0 + MLA_NOPE
        qt_out[b0:r0, :] = (qt[b0:r0, :] * QK_SCALE).astype(BF16)
        x1 = qt[r0:r0 + half, :]
        x2 = qt[r0 + half:r0 + 2 * half, :]
        qt_out[r0:r0 + half, :] = ((x1 * ct - x2 * st) * QK_SCALE).astype(BF16)
        qt_out[r0 + half:r0 + 2 * half, :] = ((x1 * st + x2 * ct) * QK_SCALE).astype(BF16)
        qt_out[r0 + 2 * half:b0 + MLA_HEAD_PAD, :] = jnp.zeros(
            (MLA_HEAD_PAD - MLA_NOPE - MLA_ROPE, TM), BF16)
        k_out[:, b0:r0] = kn[:, hd * MLA_NOPE:(hd + 1) * MLA_NOPE].astype(BF16)
        k_out[:, r0:b0 + MLA_HEAD_PAD] = krr
    vt_out[...] = vt.astype(BF16)


def _mla_proj(x, g, mods, wts, tabs, tpb, nb):
    t, d = x.shape
    row = _mod_row_map(tpb, nb)
    wcq, wckv, wkr, qn, kvn, wuqt, wuk, wuvt = wts
    full = lambda a: pl.BlockSpec(a.shape, lambda i: (0,) * a.ndim)
    hq = MLA_HEADS * MLA_HEAD_PAD
    hv = MLA_HEADS * MLA_V
    tok_cols = lambda r: pl.BlockSpec((r, TM), lambda i: (0, i))
    return pl.pallas_call(
        _mla_proj_kernel,
        grid=(t // TM,),
        in_specs=[
            pl.BlockSpec((TM, d), lambda i: (i, 0)),
            pl.BlockSpec((1, d), lambda i: (0, 0)),
            pl.BlockSpec((None, 6, d), lambda i: (row(i), 0, 0)),
            full(wcq), full(wckv), full(wkr), full(qn), full(kvn), full(wuqt), full(wuk), full(wuvt),
            pl.BlockSpec((TM, 128), lambda i: (i, 0)),
            pl.BlockSpec((TM, 128), lambda i: (i, 0)),
            pl.BlockSpec((TM, 128), lambda i: (i, 0)),
            tok_cols(MLA_ROPE // 2), tok_cols(MLA_ROPE // 2),
        ],
        out_specs=[tok_cols(hq), pl.BlockSpec((TM, hq), lambda i: (i, 0)), tok_cols(hv)],
        out_shape=[
            jax.ShapeDtypeStruct((hq, t), BF16),
            jax.ShapeDtypeStruct((t, hq), BF16),
            jax.ShapeDtypeStruct((hv, t), BF16),
        ],
        compiler_params=_cparams(("parallel",)),
        name="mla_proj",
    )(x, g, mods, wcq, wckv, wkr, qn, kvn, wuqt, wuk, wuvt, *tabs)


def _scores(kt, qt):
    return jnp.dot(kt, qt, preferred_element_type=F32)


def _softmax_step(s, vtt, carry):
    m, l, acc = carry
    m_new = jnp.maximum(m, jnp.max(s, axis=0, keepdims=True))
    a = jnp.exp2(m - m_new)
    p = jnp.exp2(s - m_new)
    l = a * l + jnp.sum(p, axis=0, keepdims=True)
    acc = a * acc + jnp.dot(vtt, p.astype(BF16), preferred_element_type=F32)
    return m_new, l, acc


def _softmax_init():
    return (jnp.full((1, TM), -jnp.inf, F32), jnp.zeros((1, TM), F32), jnp.zeros((MLA_V, TM), F32))


def _attn_lat_kernel(*refs, n_lat_tiles):
    qt_refs = refs[:ATT_QT]
    k_ref, vt_ref, o_ref = refs[ATT_QT:ATT_QT + 3]
    scr = refs[ATT_QT + 3:]
    s_a, s_b, acc_s = scr[:ATT_QT], scr[ATT_QT:2 * ATT_QT], scr[2 * ATT_QT:]
    chains = range(ATT_QT)

    def k_tile(j):
        return k_ref[pl.ds(pl.multiple_of(TM + j * ATT_TK, ATT_TK // 2), ATT_TK), :]

    def vt_tile(j):
        return vt_ref[:, pl.ds(pl.multiple_of(TM + j * ATT_TK, ATT_TK // 2), ATT_TK)]

    def put_scores(j, dst):
        kt = k_tile(j)
        for c in chains:
            dst[c][...] = _scores(kt, qt_refs[c][...])

    def softmax(src, j, ml):
        vtt = vt_tile(j)
        out = []
        for c in chains:
            m, l, acc = _softmax_step(src[c][...], vtt, (ml[c][0], ml[c][1], acc_s[c][...]))
            acc_s[c][...] = acc
            out.append((m, l))
        return tuple(out)

    ctx_scores = tuple(_scores(k_ref[0:TM, :], qt_refs[c][...]) for c in chains)
    ml = []
    for c in chains:
        m, l, acc = _softmax_step(ctx_scores[c], vt_ref[:, 0:TM], _softmax_init())
        acc_s[c][...] = acc
        ml.append((m, l))
    ml = tuple(ml)

    put_scores(0, s_a)

    def body(i, ml):
        j = 2 * i
        put_scores(j + 1, s_b)
        ml = softmax(s_a, j, ml)
        put_scores(j + 2, s_a)
        return softmax(s_b, j + 1, ml)

    ml = lax.fori_loop(0, n_lat_tiles // 2 - 1, body, ml)
    put_scores(n_lat_tiles - 1, s_b)
    ml = softmax(s_a, n_lat_tiles - 2, ml)
    ml = softmax(s_b, n_lat_tiles - 1, ml)
    for c in chains:
        o_ref[c * TM:(c + 1) * TM, :] = (acc_s[c][...] / ml[c][1]).T.astype(BF16)


def _attn_ctx_kernel(qt_ref, k_ref, vt_ref, o_ref):
    m, l, acc = _softmax_step(_scores(k_ref[...], qt_ref[...]), vt_ref[...], _softmax_init())
    o_ref[...] = (acc / l).T.astype(BF16)


def _attention(qt, k, vt, batch, nb_tok):
    tpb = nb_tok // TM
    lat_tok = nb_tok - TM
    n_lat_tiles = lat_tok // ATT_TK
    qb = ATT_QT * TM
    nq = lat_tok // qb
    hv = MLA_HEADS * MLA_V

    def q_spec(c):
        return pl.BlockSpec((MLA_HEAD_PAD, TM), lambda b, h, i: (h, b * tpb + 1 + ATT_QT * i + c))

    o_lat = pl.pallas_call(
        functools.partial(_attn_lat_kernel, n_lat_tiles=n_lat_tiles),
        grid=(batch, MLA_HEADS, nq),
        in_specs=[q_spec(c) for c in range(ATT_QT)] + [
            pl.BlockSpec((nb_tok, MLA_HEAD_PAD), lambda b, h, i: (b, h)),
            pl.BlockSpec((MLA_V, nb_tok), lambda b, h, i: (h, b)),
        ],
        out_specs=pl.BlockSpec((qb, MLA_V), lambda b, h, i: (b * nq + i, h)),
        out_shape=jax.ShapeDtypeStruct((batch * lat_tok, hv), BF16),
        scratch_shapes=([pltpu.VMEM((ATT_TK, TM), F32)] * (2 * ATT_QT)
                        + [pltpu.VMEM((MLA_V, TM), F32)] * ATT_QT),
        compiler_params=_cparams(("parallel", "parallel", "arbitrary")),
        name="mla_attn",
    )(*([qt] * ATT_QT), k, vt)
    o_ctx = pl.pallas_call(
        _attn_ctx_kernel,
        grid=(batch, MLA_HEADS),
        in_specs=[
            pl.BlockSpec((MLA_HEAD_PAD, TM), lambda b, h: (h, b * tpb)),
            pl.BlockSpec((TM, MLA_HEAD_PAD), lambda b, h: (b * tpb, h)),
            pl.BlockSpec((MLA_V, TM), lambda b, h: (h, b * tpb)),
        ],
        out_specs=pl.BlockSpec((TM, MLA_V), lambda b, h: (b, h)),
        out_shape=jax.ShapeDtypeStruct((batch * TM, hv), BF16),
        compiler_params=_cparams(("parallel", "parallel")),
        name="mla_attn_ctx",
    )(qt, k, vt)
    return o_ctx, o_lat


def _attn_out_kernel(x_ref, mod_ref, oc_ref, ol_ref, w_ref, out_ref, *, tpb):
    is_ctx = pl.program_id(0) % tpb == 0

    @pl.when(is_ctx)
    def _():
        y = jnp.dot(oc_ref[...], w_ref[...], preferred_element_type=F32)
        out_ref[...] = x_ref[...] + mod_ref[2:3, :] * y

    @pl.when(jnp.logical_not(is_ctx))
    def _():
        y = jnp.dot(ol_ref[...], w_ref[...], preferred_element_type=F32)
        out_ref[...] = x_ref[...] + mod_ref[2:3, :] * y


def _attn_out(x, mods, o_ctx, o_lat, w_bf, tpb, nb):
    t, d = x.shape
    row = _mod_row_map(tpb, nb)
    hv = o_lat.shape[1]
    lt = tpb - 1
    return pl.pallas_call(
        functools.partial(_attn_out_kernel, tpb=tpb),
        grid=(t // TM,),
        in_specs=[
            pl.BlockSpec((TM, d), lambda i: (i, 0)),
            pl.BlockSpec((None, 6, d), lambda i: (row(i), 0, 0)),
            pl.BlockSpec((TM, hv), lambda i: (i // tpb, 0)),
            pl.BlockSpec((TM, hv), lambda i: ((i // tpb) * lt + jnp.maximum(i % tpb - 1, 0), 0)),
            pl.BlockSpec(w_bf.shape, lambda i: (0, 0)),
        ],
        out_specs=pl.BlockSpec((TM, d), lambda i: (i, 0)),
        out_shape=jax.ShapeDtypeStruct((t, d), F32),
        compiler_params=_cparams(("parallel",)),
        name="attn_out",
    )(x, mods, o_ctx, o_lat, w_bf)


def _router_kernel(x_ref, g_ref, mod_ref, rw_ref, rb_ref,
                   h_out, eid_out, gate_out, rank_out, cnt_out, carry):
    @pl.when(pl.program_id(0) == 0)
    def _():
        carry[...] = jnp.zeros_like(carry)

    h = _norm_mod(x_ref[...], g_ref[...], mod_ref[3:4, :], mod_ref[4:5, :])
    for j in range(ROW_TILES):
        h_out[pl.ds(j, TM, stride=ROW_TILES), :] = h[:, j * LANES:(j + 1) * LANES]
    logits = lax.dot_general(rw_ref[...], h, (((1,), (1,)), ((), ())), precision=HIGHEST,
                             preferred_element_type=F32) + rb_ref[...]
    e_iota = lax.broadcasted_iota(jnp.int32, (N_EXPERTS, TM), 0).astype(F32)
    work = logits
    vals, idxs, hots = [], [], []
    for _ in range(TOP_K):
        m = jnp.max(work, axis=0, keepdims=True)
        idx = jnp.min(jnp.where(work == m, e_iota, float(N_EXPERTS)), axis=0, keepdims=True)
        hot = e_iota == idx
        vals.append(m)
        idxs.append(idx)
        hots.append(hot)
        work = jnp.where(hot, -jnp.inf, work)
    exps = [jnp.exp(vv - vals[0]) for vv in vals]
    den = exps[0]
    for e in exps[1:]:
        den = den + e
    mask = jnp.zeros((N_EXPERTS, TM), F32)
    for hot in hots:
        mask = mask + jnp.where(hot, 1.0, 0.0)
    r_i = lax.broadcasted_iota(jnp.int32, (TM, TM), 0)
    c_i = lax.broadcasted_iota(jnp.int32, (TM, TM), 1)
    upper = jnp.where(r_i < c_i, 1.0, 0.0).astype(BF16)
    prefix = jnp.dot(mask.astype(BF16), upper, preferred_element_type=F32) + carry[:, 0:1]
    for kk in range(TOP_K):
        eid_out[kk:kk + 1, :] = idxs[kk].astype(jnp.int32)
        gate_out[kk:kk + 1, :] = exps[kk] / den
        rank = jnp.sum(jnp.where(hots[kk], prefix, 0.0), axis=0, keepdims=True)
        rank_out[kk:kk + 1, :] = rank.astype(jnp.int32)
    new_carry = carry[...] + jnp.sum(mask, axis=1, keepdims=True)
    carry[...] = new_carry
    cnt_out[...] = new_carry


def _router(x, g, mods, rw_t, rb, tpb, nb):
    t, d = x.shape
    row = _mod_row_map(tpb, nb)
    tok_row = lambda dt: jax.ShapeDtypeStruct((TOP_K, t), dt)
    return pl.pallas_call(
        _router_kernel,
        grid=(t // TM,),
        in_specs=[
            pl.BlockSpec((TM, d), lambda i: (i, 0)),
            pl.BlockSpec((1, d), lambda i: (0, 0)),
            pl.BlockSpec((None, 6, d), lambda i: (row(i), 0, 0)),
            pl.BlockSpec((N_EXPERTS, d), lambda i: (0, 0)),
            pl.BlockSpec((N_EXPERTS, 1), lambda i: (0, 0)),
        ],
        out_specs=[
            pl.BlockSpec((TM * ROW_TILES, LANES), lambda i: (i, 0)),
            pl.BlockSpec((TOP_K, TM), lambda i: (0, i)),
            pl.BlockSpec((TOP_K, TM), lambda i: (0, i)),
            pl.BlockSpec((TOP_K, TM), lambda i: (0, i)),
            pl.BlockSpec((N_EXPERTS, 128), lambda i: (0, 0)),
        ],
        out_shape=[
            jax.ShapeDtypeStruct((t * ROW_TILES, LANES), F32),
            tok_row(jnp.int32), tok_row(F32), tok_row(jnp.int32),
            jax.ShapeDtypeStruct((N_EXPERTS, 128), F32),
        ],
        scratch_shapes=[pltpu.VMEM((N_EXPERTS, 128), F32)],
        compiler_params=_cparams(("arbitrary",)),
        name="router",
    )(x, g, mods, rw_t, rb)


def _dispatch_kernel(pos_ref, h_ref, hs_in, hs_out, sem):
    del hs_in

    def row_copy(tok, slot):
        src = h_ref.at[pl.ds(pl.multiple_of(tok * ROW_TILES, ROW_TILES), ROW_TILES), :]
        return pltpu.make_async_copy(src, hs_out.at[slot], sem)

    def issue(tok, _):
        for kk in range(TOP_K):
            row_copy(tok, pos_ref[0, kk * TM + tok]).start()
        return 0

    lax.fori_loop(0, TM, issue, 0)

    def drain(tok, _):
        for kk in range(TOP_K):
            row_copy(tok, pos_ref[0, kk * TM + tok]).wait()
        return 0

    lax.fori_loop(0, TM, drain, 0)


def _dispatch(h, pos_tiles, s_pad):
    t = h.shape[0] // ROW_TILES
    zeros = jnp.zeros((s_pad, ROW_TILES, LANES), F32)
    return pl.pallas_call(
        _dispatch_kernel,
        grid=(t // TM,),
        in_specs=[
            pl.BlockSpec((None, 1, TOP_K * TM), lambda i: (i, 0, 0), memory_space=pltpu.SMEM),
            pl.BlockSpec((TM * ROW_TILES, LANES), lambda i: (i, 0)),
            pl.BlockSpec(memory_space=pl.ANY),
        ],
        out_specs=pl.BlockSpec(memory_space=pl.ANY),
        out_shape=jax.ShapeDtypeStruct((s_pad, ROW_TILES, LANES), F32),
        scratch_shapes=[pltpu.SemaphoreType.DMA(())],
        input_output_aliases={2: 0},
        compiler_params=_cparams(("arbitrary",)),
        name="moe_dispatch",
    )(pos_tiles, h, zeros)


def _expert_kernel(te_ref, nu_ref, hs_ref, wu_ref, bu_ref, wd_ref, bd_ref, ys_ref, wu_bf, wd_bf):
    i = pl.program_id(0)
    dff = wd_ref.shape[0]
    cast_rows = 128

    @pl.when(jnp.logical_or(i == 0, te_ref[i] != te_ref[jnp.maximum(i - 1, 0)]))
    def _():
        def cast(r, _):
            rows = pl.ds(pl.multiple_of(r * cast_rows, cast_rows), cast_rows)
            wu_bf[rows, :] = wu_ref[rows, :].astype(BF16)
            wd_bf[rows, :] = wd_ref[rows, :].astype(BF16)
            return 0

        lax.fori_loop(0, dff // cast_rows, cast, 0)

    @pl.when(i < nu_ref[0])
    def _():
        h = jnp.concatenate([hs_ref[pl.ds(j, TM, stride=ROW_TILES), :] for j in range(ROW_TILES)],
                            axis=1).astype(BF16)
        u = jnp.dot(h, wu_bf[...], preferred_element_type=F32) + bu_ref[...]
        glu = jnp.minimum(u[:, :dff], SWIGLU_LIMIT)
        lin = jnp.clip(u[:, dff:], -SWIGLU_LIMIT, SWIGLU_LIMIT)
        act = glu * jax.nn.sigmoid(SWIGLU_ALPHA * glu) * (lin + 1.0)
        y = jnp.dot(act.astype(BF16), wd_bf[...], preferred_element_type=F32) + bd_ref[...]
        for j in range(ROW_TILES):
            ys_ref[pl.ds(j, TM, stride=ROW_TILES), :] = y[:, j * LANES:(j + 1) * LANES]

    @pl.when(i >= nu_ref[0])
    def _():
        ys_ref[...] = jnp.zeros_like(ys_ref)


def _experts(hs, tile_expert, n_used, layer, w_up, b_up, w_down, b_down):
    s_pad = hs.shape[0] // ROW_TILES
    d, dff2 = w_up.shape[2], w_up.shape[3]
    dff = dff2 // 2
    assert d == dff
    blk = lambda a, b: pl.BlockSpec((None, None, a, b), lambda i, te, nu: (layer, te[i], 0, 0))
    grid_spec = pltpu.PrefetchScalarGridSpec(
        num_scalar_prefetch=2,
        grid=(s_pad // TM,),
        in_specs=[
            pl.BlockSpec((TM * ROW_TILES, LANES), lambda i, te, nu: (i, 0)),
            blk(d, dff2), blk(1, dff2), blk(dff, d), blk(1, d),
        ],
        out_specs=pl.BlockSpec((TM * ROW_TILES, LANES), lambda i, te, nu: (i, 0)),
        scratch_shapes=[pltpu.VMEM((d, dff2), BF16), pltpu.VMEM((dff, d), BF16)],
    )
    return pl.pallas_call(
        _expert_kernel,
        grid_spec=grid_spec,
        out_shape=jax.ShapeDtypeStruct((s_pad * ROW_TILES, LANES), F32),
        compiler_params=_cparams(("arbitrary",)),
        name="moe_experts",
    )(tile_expert, n_used, hs, w_up, b_up, w_down, b_down)


def _combine_kernel(pos_ref, x_ref, mod_ref, gate_ref, ys_ref, o_ref, ybuf, sem):
    def row_copy(tok, kk):
        slot = pos_ref[0, kk * TM + tok]
        dst = ybuf.at[pl.ds(pl.multiple_of((kk * TM + tok) * ROW_TILES, ROW_TILES), ROW_TILES), :]
        return pltpu.make_async_copy(ys_ref.at[slot], dst, sem)

    def issue(tok, _):
        for kk in range(TOP_K):
            row_copy(tok, kk).start()
        return 0

    lax.fori_loop(0, TM, issue, 0)

    def drain(tok, _):
        for kk in range(TOP_K):
            row_copy(tok, kk).wait()
        return 0

    lax.fori_loop(0, TM, drain, 0)

    gates = [jnp.broadcast_to(gate_ref[:, kk:kk + 1], (TM, LANES)) for kk in range(TOP_K)]
    for j in range(ROW_TILES):
        cols = slice(j * LANES, (j + 1) * LANES)
        acc = gates[0] * ybuf[pl.ds(j, TM, stride=ROW_TILES), :]
        for kk in range(1, TOP_K):
            acc = acc + gates[kk] * ybuf[pl.ds(kk * TM * ROW_TILES + j, TM, stride=ROW_TILES), :]
        o_ref[:, cols] = x_ref[:, cols] + mod_ref[5:6, cols] * acc


def _combine(x, mods, gates_t, pos_tiles, ys, tpb, nb):
    t, d = x.shape
    row = _mod_row_map(tpb, nb)
    return pl.pallas_call(
        _combine_kernel,
        grid=(t // TM,),
        in_specs=[
            pl.BlockSpec((None, 1, TOP_K * TM), lambda i: (i, 0, 0), memory_space=pltpu.SMEM),
            pl.BlockSpec((TM, d), lambda i: (i, 0)),
            pl.BlockSpec((None, 6, d), lambda i: (row(i), 0, 0)),
            pl.BlockSpec((TM, TOP_K), lambda i: (i, 0)),
            pl.BlockSpec(memory_space=pl.ANY),
        ],
        out_specs=pl.BlockSpec((TM, d), lambda i: (i, 0)),
        out_shape=jax.ShapeDtypeStruct((t, d), F32),
        scratch_shapes=[pltpu.VMEM((TOP_K * TM * ROW_TILES, LANES), F32), pltpu.SemaphoreType.DMA(())],
        compiler_params=_cparams(("arbitrary",)),
        name="moe_combine",
    )(pos_tiles, x, mods, gates_t, ys)


def _moe_layer(x, g, mods, rw_t, rb, layer, w_up, b_up, w_down, b_down, tpb, nb):
    t, d = x.shape
    assert d == ROW_TILES * LANES
    n_tok_tiles = t // TM
    s_pad = t * TOP_K + N_EXPERTS * TM
    h, eid, gate, rank, cnt = _router(x, g, mods, rw_t, rb, tpb, nb)
    counts = cnt[:, 0].astype(jnp.int32)
    padded = ((counts + TM - 1) // TM) * TM
    ends = jnp.cumsum(padded)
    offs = ends - padded
    e_ids = jnp.arange(N_EXPERTS, dtype=jnp.int32)
    pos = jnp.sum(jnp.where(eid[..., None] == e_ids, offs, 0), axis=-1) + rank
    pos_tiles = pos.reshape(TOP_K, n_tok_tiles, TM).transpose(1, 0, 2).reshape(n_tok_tiles, 1, TOP_K * TM)
    tile_start = jnp.arange(s_pad // TM, dtype=jnp.int32) * TM
    tile_expert = jnp.sum((ends[None, :] <= tile_start[:, None]).astype(jnp.int32), axis=-1)
    tile_expert = jnp.minimum(tile_expert, N_EXPERTS - 1)
    n_used = (ends[-1] // TM).astype(jnp.int32).reshape(1)
    hs = _dispatch(h, pos_tiles, s_pad)
    ys = _experts(hs.reshape(s_pad * ROW_TILES, LANES), tile_expert, n_used, layer, w_up, b_up, w_down, b_down)
    return _combine(x, mods, gate.T, pos_tiles, ys.reshape(s_pad, ROW_TILES, LANES), tpb, nb)


def _final_kernel(x_ref, g_ref, o_ref):
    o_ref[...] = _rms(x_ref[...], g_ref[...])


def _final_norm(x, g, batch, tpb):
    t, d = x.shape
    lt = tpb - 1
    return pl.pallas_call(
        _final_kernel,
        grid=(batch, lt),
        in_specs=[
            pl.BlockSpec((TM, d), lambda b, j: (b * tpb + 1 + j, 0)),
            pl.BlockSpec((1, d), lambda b, j: (0, 0)),
        ],
        out_specs=pl.BlockSpec((TM, d), lambda b, j: (b * lt + j, 0)),
        out_shape=jax.ShapeDtypeStruct((batch * lt * TM, d), F32),
        compiler_params=_cparams(("parallel", "parallel")),
        name="final_norm",
    )(x, g)


def _position_tables(batch, seq, ctx_len):
    rows = seq // GRID_W
    row = jnp.broadcast_to(jnp.arange(rows, dtype=F32)[:, None], (rows, GRID_W)).reshape(-1)
    col = jnp.broadcast_to(jnp.arange(GRID_W, dtype=F32)[None, :], (rows, GRID_W)).reshape(-1)
    n_ax = MLA_ROPE // 4
    ax_freq = ROPE_BASE ** (-jnp.arange(n_ax, dtype=F32) / n_ax)
    ang_mla = jnp.concatenate([row[:, None] * ax_freq, col[:, None] * ax_freq], axis=-1)
    ret_theta = 1.0 / (RET_THETA_BASE ** jnp.linspace(0.0, 1.0, RET_DK // 2, dtype=F32))
    ang_ret = jnp.arange(seq, dtype=F32)[:, None] * ret_theta[None, :]

    def with_ctx(lat, ctx_val):
        ctx = jnp.full((ctx_len, lat.shape[1]), ctx_val, F32)
        return jnp.tile(jnp.concatenate([ctx, lat], axis=0), (batch, 1))

    ret_cos = with_ctx(jnp.cos(ang_ret), 1.0)
    ret_sin = with_ctx(jnp.sin(ang_ret), 0.0)
    cm, sm = jnp.cos(ang_mla), jnp.sin(ang_mla)
    z = jnp.zeros_like(cm)
    mla_c = with_ctx(jnp.concatenate([cm, cm, z, z], axis=-1), 1.0)
    mla_s1 = with_ctx(jnp.concatenate([-sm, z, z, z], axis=-1), 0.0)
    mla_s2 = with_ctx(jnp.concatenate([z, sm, z, z], axis=-1), 0.0)
    mla_ct = with_ctx(cm, 1.0).T
    mla_st = with_ctx(sm, 0.0).T
    return ret_cos, ret_sin, (mla_c, mla_s1, mla_s2, mla_ct, mla_st)


def _mla_weights(w_in, q_norm, kv_norm, w_uq, w_ukv):
    d = w_in.shape[0]
    wcq = w_in[:, :MLA_Q_LORA].astype(BF16)
    wckv = w_in[:, MLA_Q_LORA:MLA_Q_LORA + MLA_KV_LORA].astype(BF16)
    wkr = jnp.pad(w_in[:, MLA_Q_LORA + MLA_KV_LORA:], ((0, 0), (0, 128 - MLA_ROPE))).astype(BF16)
    wuq = w_uq.reshape(MLA_Q_LORA, MLA_HEADS, MLA_NOPE + MLA_ROPE)
    wuq = jnp.pad(wuq, ((0, 0), (0, 0), (0, MLA_HEAD_PAD - MLA_NOPE - MLA_ROPE)))
    wuqt = wuq.reshape(MLA_Q_LORA, MLA_HEADS * MLA_HEAD_PAD).T.astype(BF16)
    wukv = w_ukv.reshape(MLA_KV_LORA, MLA_HEADS, MLA_NOPE + MLA_V)
    wuk = wukv[:, :, :MLA_NOPE].reshape(MLA_KV_LORA, MLA_HEADS * MLA_NOPE).astype(BF16)
    wuvt = wukv[:, :, MLA_NOPE:].reshape(MLA_KV_LORA, MLA_HEADS * MLA_V).T.astype(BF16)
    del d
    return (wcq, wckv, wkr, q_norm.reshape(1, -1), kv_norm.reshape(1, -1), wuqt, wuk, wuvt)


def kernel(x, c, ctx, c_ctx, ret_w_in, ret_decay_logit, ret_gn, ret_w_out, mla_w_in, mla_q_norm,
           mla_kv_norm, mla_w_uq, mla_w_ukv, mla_w_out, ada_w, ada_b, norm_mix, norm_ffn, router_w,
           router_b, exp_w_up, exp_b_up, exp_w_down, exp_b_down, final_norm):
    batch, seq, d = x.shape
    ctx_len = ctx.shape[1]
    depth = ada_w.shape[0]
    assert ctx_len == TM and seq % ATT_TK == 0 and seq % (ATT_QT * TM) == 0 and batch + 1 <= 8
    nb_tok = ctx_len + seq
    tpb = nb_tok // TM

    xs = jnp.concatenate([ctx, x], axis=1).reshape(batch * nb_tok, d)
    c_rows = jnp.concatenate([c, c_ctx[None, :], jnp.zeros((8 - batch - 1, d), F32)], axis=0)
    mods_all = _ada_mods(c_rows, ada_w, ada_b)
    ret_cos, ret_sin, mla_tabs = _position_tables(batch, seq, ctx_len)

    for layer in range(depth):
        j = layer // 2
        mods = mods_all[layer]
        g_mix = norm_mix[layer].reshape(1, d)
        if layer % 2 == 0:
            p = _ret_proj(xs, g_mix, mods, ret_w_in[j].astype(BF16), ret_cos, ret_sin, tpb, batch)
            o_f, o_b = _ret_core(p, ret_decay_logit[j], batch, nb_tok)
            xs = _ret_out(xs, mods, o_f, o_b, p, ret_gn[j], ret_w_out[j].astype(BF16), tpb, batch)
        else:
            wts = _mla_weights(mla_w_in[j], mla_q_norm[j], mla_kv_norm[j], mla_w_uq[j], mla_w_ukv[j])
            qt, k, vt = _mla_proj(xs, g_mix, mods, wts, mla_tabs, tpb, batch)
            o_ctx, o_lat = _attention(qt, k, vt, batch, nb_tok)
            xs = _attn_out(xs, mods, o_ctx, o_lat, mla_w_out[j].astype(BF16), tpb, batch)
        xs = _moe_layer(xs, norm_ffn[layer].reshape(1, d), mods, router_w[layer].T,
                        router_b[layer].reshape(-1, 1), layer, exp_w_up,
                        exp_b_up.reshape(depth, N_EXPERTS, 1, -1), exp_w_down,
                        exp_b_down.reshape(depth, N_EXPERTS, 1, -1), tpb, batch)

    out = _final_norm(xs, final_norm.reshape(1, d), batch, tpb)
    return out.reshape(batch, seq, d)
```

```python
import functools

import jax
import jax.numpy as jnp
import numpy as np
from jax import lax
from jax.experimental import pallas as pl
from jax.experimental.pallas import tpu as pltpu

F32 = jnp.float32
BF16 = jnp.bfloat16
HIGHEST = lax.Precision.HIGHEST

TM = 256
LANES = 128
ROW_TILES = 8
RET_HEADS = 4
RET_DK = 256
RET_DV = 512
RET_CHUNK = 128
RET_HP = 2
RET_THETA_BASE = 10000.0
MLA_HEADS = 8
MLA_NOPE = 128
MLA_ROPE = 64
MLA_V = 128
MLA_Q_LORA = 384
MLA_KV_LORA = 256
MLA_HEAD_PAD = 256
ROPE_BASE = 10000.0
GRID_W = 64
N_EXPERTS = 32
TOP_K = 4
SWIGLU_ALPHA = 1.702
SWIGLU_LIMIT = 7.0
EPS = 1e-6
GN_EPS = 1e-5
ATT_TK = 512
ATT_QT = 4
QK_SCALE = float((MLA_NOPE + MLA_ROPE) ** -0.5 * np.log2(np.e))
VMEM_LIMIT = 48 * 1024 * 1024


def _cparams(sem):
    return pltpu.CompilerParams(dimension_semantics=sem, vmem_limit_bytes=VMEM_LIMIT)


def _norm_mod(x, g, shift, scale):
    ms = jnp.mean(x * x, axis=-1, keepdims=True)
    xn = x * lax.rsqrt(ms + EPS)
    return (xn * g) * (1.0 + scale) + shift


def _rms(x, g):
    ms = jnp.mean(x * x, axis=-1, keepdims=True)
    return x * lax.rsqrt(ms + EPS) * g


def _mod_row_map(tpb, nb):
    def f(i):
        return jnp.where(i % tpb == 0, nb, i // tpb)
    return f


def _ada_kernel(c_ref, w_ref, b_ref, o_ref):
    s = c_ref[...]
    s = s * jax.nn.sigmoid(s)
    o_ref[...] = jnp.dot(s, w_ref[...], precision=HIGHEST, preferred_element_type=F32) + b_ref[...]


def _ada_mods(c_rows, ada_w, ada_b):
    depth, d, d6 = ada_w.shape
    nt = d6 // d
    out = pl.pallas_call(
        _ada_kernel,
        grid=(depth, nt),
        in_specs=[
            pl.BlockSpec((8, d), lambda l, n: (0, 0)),
            pl.BlockSpec((None, d, d), lambda l, n: (l, 0, n)),
            pl.BlockSpec((None, 1, d), lambda l, n: (l, 0, n)),
        ],
        out_specs=pl.BlockSpec((None, 8, d), lambda l, n: (l, 0, n)),
        out_shape=jax.ShapeDtypeStruct((depth, 8, d6), F32),
        compiler_params=_cparams(("parallel", "parallel")),
        name="ada_mods",
    )(c_rows, ada_w, ada_b.reshape(depth, 1, d6))
    return out.reshape(depth, 8, nt, d)


def _ret_proj_kernel(x_ref, g_ref, mod_ref, w_ref, cos_ref, sin_ref, o_ref):
    h = _norm_mod(x_ref[...], g_ref[...], mod_ref[0:1, :], mod_ref[1:2, :]).astype(BF16)
    tn = 2 * RET_HEADS * RET_DK
    p = jnp.dot(h, w_ref[:, 0:tn], preferred_element_type=F32)
    cos = cos_ref[...]
    sin = sin_ref[...]
    half = RET_DK // 2
    for which in range(2):
        sc = 1.0 if which == 0 else RET_DK ** -0.5
        for hd in range(RET_HEADS):
            base = which * RET_HEADS * RET_DK + hd * RET_DK
            x1 = p[:, base:base + half]
            x2 = p[:, base + half:base + RET_DK]
            o_ref[:, base:base + half] = ((x1 * cos - x2 * sin) * sc).astype(BF16)
            o_ref[:, base + half:base + RET_DK] = ((x1 * sin + x2 * cos) * sc).astype(BF16)
    for n in range(1, w_ref.shape[1] // tn):
        cols = slice(n * tn, (n + 1) * tn)
        o_ref[:, cols] = jnp.dot(h, w_ref[:, cols], preferred_element_type=F32).astype(BF16)


def _ret_proj(x, g, mods, w_bf, cos_t, sin_t, tpb, nb):
    t, d = x.shape
    ncols = w_bf.shape[1]
    row = _mod_row_map(tpb, nb)
    return pl.pallas_call(
        _ret_proj_kernel,
        grid=(t // TM,),
        in_specs=[
            pl.BlockSpec((TM, d), lambda i: (i, 0)),
            pl.BlockSpec((1, d), lambda i: (0, 0)),
            pl.BlockSpec((None, 6, d), lambda i: (row(i), 0, 0)),
            pl.BlockSpec((d, ncols), lambda i: (0, 0), pipeline_mode=pl.Buffered(1)),
            pl.BlockSpec((TM, RET_DK // 2), lambda i: (i, 0)),
            pl.BlockSpec((TM, RET_DK // 2), lambda i: (i, 0)),
        ],
        out_specs=pl.BlockSpec((TM, ncols), lambda i: (i, 0)),
        out_shape=jax.ShapeDtypeStruct((t, ncols), BF16),
        compiler_params=_cparams(("parallel",)),
        name="ret_proj",
    )(x, g, mods, w_bf, cos_t, sin_t)


def _log_sigmoid(x):
    return jnp.minimum(x, 0.0) - jnp.log1p(jnp.exp(-jnp.abs(x)))


def _ret_direction(q_ref, k_ref, v_ref, o_ref, s_ref, logit, forward, hh):
    c = RET_CHUNK
    lg_row = _log_sigmoid(logit + jnp.zeros((1, c), F32))
    lg = lg_row[:, 0:1]
    i_col = lax.broadcasted_iota(jnp.int32, (c, 1), 0).astype(F32)
    i_mat = lax.broadcasted_iota(jnp.int32, (c, c), 0).astype(F32)
    j_mat = lax.broadcasted_iota(jnp.int32, (c, c), 1).astype(F32)
    if forward:
        rel = i_mat - j_mat
        q_pow = i_col + 1.0
        k_pow = (c - 1.0) - i_col
    else:
        rel = j_mat - i_mat
        q_pow = c - i_col
        k_pow = i_col
    mask = jnp.where(rel >= 0, jnp.exp(lg_row * jnp.maximum(rel, 0.0)), 0.0)
    q_decay = jnp.exp(lg * q_pow)
    k_decay = jnp.exp(lg * k_pow)
    chunk_decay = jnp.exp(lg * float(c))

    q = q_ref[:, hh * RET_DK:(hh + 1) * RET_DK]
    k = k_ref[:, hh * RET_DK:(hh + 1) * RET_DK]
    v = v_ref[:, hh * RET_DV:(hh + 1) * RET_DV]
    s = s_ref[...]
    raw = lax.dot_general(q, k, (((1,), (1,)), ((), ())), preferred_element_type=F32)
    qs = (q.astype(F32) * q_decay).astype(BF16)
    inter = jnp.dot(qs, s.astype(BF16), preferred_element_type=F32)
    ks = (k.astype(F32) * k_decay).astype(BF16)
    upd = lax.dot_general(ks, v, (((0,), (0,)), ((), ())), preferred_element_type=F32)

    def finish():
        o = jnp.dot((raw * mask).astype(BF16), v, preferred_element_type=F32) + inter
        s_ref[...] = s * chunk_decay + upd
        o_ref[:, hh * RET_DV:(hh + 1) * RET_DV] = o.astype(BF16)

    return finish


def _ret_core_kernel(dl_ref, qf, kf, vf, qb, kb, vb, of_ref, ob_ref, *states):
    hblk = pl.program_id(1)

    @pl.when(pl.program_id(2) == 0)
    def _():
        for s_ref in states:
            s_ref[...] = jnp.zeros_like(s_ref)

    finishes = []
    for hh in range(RET_HP):
        hd = hblk * RET_HP + hh
        finishes.append(_ret_direction(qf, kf, vf, of_ref, states[2 * hh], dl_ref[0, hd], True, hh))
        finishes.append(_ret_direction(qb, kb, vb, ob_ref, states[2 * hh + 1], dl_ref[1, hd], False, hh))
    for fin in finishes:
        fin()


def _ret_core(p, decay_logit, batch, nb_tok):
    t = p.shape[0]
    nc = nb_tok // RET_CHUNK
    nctx = TM // RET_CHUNK
    qw, vw = RET_HP * RET_DK, RET_HP * RET_DV
    kq = RET_HEADS * RET_DK // qw
    kv = 2 * RET_HEADS * RET_DK // vw

    def cf(b, h, s):
        return b * nc + s

    def cb(b, h, s):
        return b * nc + jnp.where(s < nctx, nctx - 1 - s, nc - 1 + nctx - s)

    def spec(width, off, cmap):
        return pl.BlockSpec((RET_CHUNK, width), lambda b, h, s: (cmap(b, h, s), off + h))

    out_f = pl.BlockSpec((RET_CHUNK, vw), lambda b, h, s: (cf(b, h, s), h))
    out_b = pl.BlockSpec((RET_CHUNK, vw), lambda b, h, s: (cb(b, h, s), h))
    return pl.pallas_call(
        _ret_core_kernel,
        grid=(batch, RET_HEADS // RET_HP, nc),
        in_specs=[
            pl.BlockSpec(memory_space=pltpu.SMEM),
            spec(qw, 0, cf), spec(qw, kq, cf), spec(vw, kv, cf),
            spec(qw, 0, cb), spec(qw, kq, cb), spec(vw, kv, cb),
        ],
        out_specs=[out_f, out_b],
        out_shape=[jax.ShapeDtypeStruct((t, RET_HEADS * RET_DV), BF16)] * 2,
        scratch_shapes=[pltpu.VMEM((RET_DK, RET_DV), F32)] * (2 * RET_HP),
        compiler_params=_cparams(("parallel", "parallel", "arbitrary")),
        name="ret_core",
    )(decay_logit, p, p, p, p, p, p)


def _group_norm_head(o):
    mu = jnp.mean(o, axis=-1, keepdims=True)
    dlt = o - mu
    var = jnp.mean(dlt * dlt, axis=-1, keepdims=True)
    return dlt * lax.rsqrt(var + GN_EPS)


def _ret_out_kernel(x_ref, mod_ref, of_ref, ob_ref, gf_ref, gb_ref, gn_ref, w_ref, o_ref):
    acc = jnp.zeros(x_ref.shape, F32)
    for hd in range(RET_HEADS):
        sl = slice(hd * RET_DV, (hd + 1) * RET_DV)
        gf = gf_ref[:, sl].astype(F32)
        gb = gb_ref[:, sl].astype(F32)
        yf = _group_norm_head(of_ref[:, sl].astype(F32)) * gn_ref[0:1, sl]
        yb = _group_norm_head(ob_ref[:, sl].astype(F32)) * gn_ref[1:2, sl]
        y = gf * jax.nn.sigmoid(gf) * yf + gb * jax.nn.sigmoid(gb) * yb
        acc = acc + jnp.dot(y.astype(BF16), w_ref[sl, :], preferred_element_type=F32)
    o_ref[...] = x_ref[...] + mod_ref[2:3, :] * acc


def _ret_out(x, mods, o_f, o_b, p, gn, w_bf, tpb, nb):
    t, d = x.shape
    hv = RET_HEADS * RET_DV
    row = _mod_row_map(tpb, nb)
    gf_blk = (2 * RET_HEADS * RET_DK + hv) // hv
    return pl.pallas_call(
        _ret_out_kernel,
        grid=(t // TM,),
        in_specs=[
            pl.BlockSpec((TM, d), lambda i: (i, 0)),
            pl.BlockSpec((None, 6, d), lambda i: (row(i), 0, 0)),
            pl.BlockSpec((TM, hv), lambda i: (i, 0)),
            pl.BlockSpec((TM, hv), lambda i: (i, 0)),
            pl.BlockSpec((TM, hv), lambda i: (i, gf_blk)),
            pl.BlockSpec((TM, hv), lambda i: (i, gf_blk + 1)),
            pl.BlockSpec((2, hv), lambda i: (0, 0)),
            pl.BlockSpec((hv, d), lambda i: (0, 0)),
        ],
        out_specs=pl.BlockSpec((TM, d), lambda i: (i, 0)),
        out_shape=jax.ShapeDtypeStruct((t, d), F32),
        compiler_params=_cparams(("parallel",)),
        name="ret_out",
    )(x, mods, o_f, o_b, p, p, gn, w_bf)


def _mla_proj_kernel(x_ref, g_ref, mod_ref, wcq_ref, wckv_ref, wkr_ref, qn_ref, kvn_ref,
                     wuqt_ref, wuk_ref, wuvt_ref, c_ref, s1_ref, s2_ref, ct_ref, st_ref,
                     qt_out, k_out, vt_out):
    h = _norm_mod(x_ref[...], g_ref[...], mod_ref[0:1, :], mod_ref[1:2, :]).astype(BF16)
    cq = jnp.dot(h, wcq_ref[...], preferred_element_type=F32)
    ckv = jnp.dot(h, wckv_ref[...], preferred_element_type=F32)
    kr = jnp.dot(h, wkr_ref[...], preferred_element_type=F32)
    cqn = _rms(cq, qn_ref[...]).astype(BF16)
    ckvn = _rms(ckv, kvn_ref[...]).astype(BF16)
    nt = (((1,), (1,)), ((), ()))
    qt = lax.dot_general(wuqt_ref[...], cqn, nt, preferred_element_type=F32)
    vt = lax.dot_general(wuvt_ref[...], ckvn, nt, preferred_element_type=F32)
    kn = jnp.dot(ckvn, wuk_ref[...], preferred_element_type=F32)
    half = MLA_ROPE // 2
    krr = (kr * c_ref[...] + pltpu.roll(kr, 128 - half, 1) * s1_ref[...]
           + pltpu.roll(kr, half, 1) * s2_ref[...]).astype(BF16)
    ct = ct_ref[...]
    st = st_ref[...]
    for hd in range(MLA_HEADS):
        b0 = hd * MLA_HEAD_PAD
        r0 = b0 + MLA_NOPE
        qt_out[b0:r0, :] = (qt[b0:r0, :] * QK_SCALE).astype(BF16)
        x1 = qt[r0:r0 + half, :]
        x2 = qt[r0 + half:r0 + 2 * half, :]
        qt_out[r0:r0 + half, :] = ((x1 * ct - x2 * st) * QK_SCALE).astype(BF16)
        qt_out[r0 + half:r0 + 2 * half, :] = ((x1 * st + x2 * ct) * QK_SCALE).astype(BF16)
        qt_out[r0 + 2 * half:b0 + MLA_HEAD_PAD, :] = jnp.zeros(
            (MLA_HEAD_PAD - MLA_NOPE - MLA_ROPE, TM), BF16)
        k_out[:, b0:r0] = kn[:, hd * MLA_NOPE:(hd + 1) * MLA_NOPE].astype(BF16)
        k_out[:, r0:b0 + MLA_HEAD_PAD] = krr
    vt_out[...] = vt.astype(BF16)


def _mla_proj(x, g, mods, wts, tabs, tpb, nb):
    t, d = x.shape
    row = _mod_row_map(tpb, nb)
    wcq, wckv, wkr, qn, kvn, wuqt, wuk, wuvt = wts
    full = lambda a: pl.BlockSpec(a.shape, lambda i: (0,) * a.ndim)
    hq = MLA_HEADS * MLA_HEAD_PAD
    hv = MLA_HEADS * MLA_V
    tok_cols = lambda r: pl.BlockSpec((r, TM), lambda i: (0, i))
    return pl.pallas_call(
        _mla_proj_kernel,
        grid=(t // TM,),
        in_specs=[
            pl.BlockSpec((TM, d), lambda i: (i, 0)),
            pl.BlockSpec((1, d), lambda i: (0, 0)),
            pl.BlockSpec((None, 6, d), lambda i: (row(i), 0, 0)),
            full(wcq), full(wckv), full(wkr), full(qn), full(kvn), full(wuqt), full(wuk), full(wuvt),
            pl.BlockSpec((TM, 128), lambda i: (i, 0)),
            pl.BlockSpec((TM, 128), lambda i: (i, 0)),
            pl.BlockSpec((TM, 128), lambda i: (i, 0)),
            tok_cols(MLA_ROPE // 2), tok_cols(MLA_ROPE // 2),
        ],
        out_specs=[tok_cols(hq), pl.BlockSpec((TM, hq), lambda i: (i, 0)), tok_cols(hv)],
        out_shape=[
            jax.ShapeDtypeStruct((hq, t), BF16),
            jax.ShapeDtypeStruct((t, hq), BF16),
            jax.ShapeDtypeStruct((hv, t), BF16),
        ],
        compiler_params=_cparams(("parallel",)),
        name="mla_proj",
    )(x, g, mods, wcq, wckv, wkr, qn, kvn, wuqt, wuk, wuvt, *tabs)


def _scores(kt, qt):
    return jnp.dot(kt, qt, preferred_element_type=F32)


def _softmax_step(s, vtt, carry):
    m, l, acc = carry
    m_new = jnp.maximum(m, jnp.max(s, axis=0, keepdims=True))
    a = jnp.exp2(m - m_new)
    p = jnp.exp2(s - m_new)
    l = a * l + jnp.sum(p, axis=0, keepdims=True)
    acc = a * acc + jnp.dot(vtt, p.astype(BF16), preferred_element_type=F32)
    return m_new, l, acc


def _softmax_init():
    return (jnp.full((1, TM), -jnp.inf, F32), jnp.zeros((1, TM), F32), jnp.zeros((MLA_V, TM), F32))


def _attn_lat_kernel(*refs, n_lat_tiles):
    qt_refs = refs[:ATT_QT]
    k_ref, vt_ref, o_ref = refs[ATT_QT:ATT_QT + 3]
    scr = refs[ATT_QT + 3:]
    s_a, s_b, acc_s = scr[:ATT_QT], scr[ATT_QT:2 * ATT_QT], scr[2 * ATT_QT:]
    chains = range(ATT_QT)

    def k_tile(j):
        return k_ref[pl.ds(pl.multiple_of(TM + j * ATT_TK, ATT_TK // 2), ATT_TK), :]

    def vt_tile(j):
        return vt_ref[:, pl.ds(pl.multiple_of(TM + j * ATT_TK, ATT_TK // 2), ATT_TK)]

    def put_scores(j, dst):
        kt = k_tile(j)
        for c in chains:
            dst[c][...] = _scores(kt, qt_refs[c][...])

    def softmax(src, j, ml):
        vtt = vt_tile(j)
        out = []
        for c in chains:
            m, l, acc = _softmax_step(src[c][...], vtt, (ml[c][0], ml[c][1], acc_s[c][...]))
            acc_s[c][...] = acc
            out.append((m, l))
        return tuple(out)

    ctx_scores = tuple(_scores(k_ref[0:TM, :], qt_refs[c][...]) for c in chains)
    ml = []
    for c in chains:
        m, l, acc = _softmax_step(ctx_scores[c], vt_ref[:, 0:TM], _softmax_init())
        acc_s[c][...] = acc
        ml.append((m, l))
    ml = tuple(ml)

    put_scores(0, s_a)

    def body(i, ml):
        j = 2 * i
        put_scores(j + 1, s_b)
        ml = softmax(s_a, j, ml)
        put_scores(j + 2, s_a)
        return softmax(s_b, j + 1, ml)

    ml = lax.fori_loop(0, n_lat_tiles // 2 - 1, body, ml)
    put_scores(n_lat_tiles - 1, s_b)
    ml = softmax(s_a, n_lat_tiles - 2, ml)
    ml = softmax(s_b, n_lat_tiles - 1, ml)
    for c in chains:
        o_ref[c * TM:(c + 1) * TM, :] = (acc_s[c][...] / ml[c][1]).T.astype(BF16)


def _attn_ctx_kernel(qt_ref, k_ref, vt_ref, o_ref):
    m, l, acc = _softmax_step(_scores(k_ref[...], qt_ref[...]), vt_ref[...], _softmax_init())
    o_ref[...] = (acc / l).T.astype(BF16)


def _attention(qt, k, vt, batch, nb_tok):
    tpb = nb_tok // TM
    lat_tok = nb_tok - TM
    n_lat_tiles = lat_tok // ATT_TK
    qb = ATT_QT * TM
    nq = lat_tok // qb
    hv = MLA_HEADS * MLA_V

    def q_spec(c):
        return pl.BlockSpec((MLA_HEAD_PAD, TM), lambda b, h, i: (h, b * tpb + 1 + ATT_QT * i + c))

    o_lat = pl.pallas_call(
        functools.partial(_attn_lat_kernel, n_lat_tiles=n_lat_tiles),
        grid=(batch, MLA_HEADS, nq),
        in_specs=[q_spec(c) for c in range(ATT_QT)] + [
            pl.BlockSpec((nb_tok, MLA_HEAD_PAD), lambda b, h, i: (b, h)),
            pl.BlockSpec((MLA_V, nb_tok), lambda b, h, i: (h, b)),
        ],
        out_specs=pl.BlockSpec((qb, MLA_V), lambda b, h, i: (b * nq + i, h)),
        out_shape=jax.ShapeDtypeStruct((batch * lat_tok, hv), BF16),
        scratch_shapes=([pltpu.VMEM((ATT_TK, TM), F32)] * (2 * ATT_QT)
                        + [pltpu.VMEM((MLA_V, TM), F32)] * ATT_QT),
        compiler_params=_cparams(("parallel", "parallel", "arbitrary")),
        name="mla_attn",
    )(*([qt] * ATT_QT), k, vt)
    o_ctx = pl.pallas_call(
        _attn_ctx_kernel,
        grid=(batch, MLA_HEADS),
        in_specs=[
            pl.BlockSpec((MLA_HEAD_PAD, TM), lambda b, h: (h, b * tpb)),
            pl.BlockSpec((TM, MLA_HEAD_PAD), lambda b, h: (b * tpb, h)),
            pl.BlockSpec((MLA_V, TM), lambda b, h: (h, b * tpb)),
        ],
        out_specs=pl.BlockSpec((TM, MLA_V), lambda b, h: (b, h)),
        out_shape=jax.ShapeDtypeStruct((batch * TM, hv), BF16),
        compiler_params=_cparams(("parallel", "parallel")),
        name="mla_attn_ctx",
    )(qt, k, vt)
    return o_ctx, o_lat


def _attn_out_kernel(x_ref, mod_ref, oc_ref, ol_ref, w_ref, out_ref, *, tpb):
    is_ctx = pl.program_id(0) % tpb == 0

    @pl.when(is_ctx)
    def _():
        y = jnp.dot(oc_ref[...], w_ref[...], preferred_element_type=F32)
        out_ref[...] = x_ref[...] + mod_ref[2:3, :] * y

    @pl.when(jnp.logical_not(is_ctx))
    def _():
        y = jnp.dot(ol_ref[...], w_ref[...], preferred_element_type=F32)
        out_ref[...] = x_ref[...] + mod_ref[2:3, :] * y


def _attn_out(x, mods, o_ctx, o_lat, w_bf, tpb, nb):
    t, d = x.shape
    row = _mod_row_map(tpb, nb)
    hv = o_lat.shape[1]
    lt = tpb - 1
    return pl.pallas_call(
        functools.partial(_attn_out_kernel, tpb=tpb),
        grid=(t // TM,),
        in_specs=[
            pl.BlockSpec((TM, d), lambda i: (i, 0)),
            pl.BlockSpec((None, 6, d), lambda i: (row(i), 0, 0)),
            pl.BlockSpec((TM, hv), lambda i: (i // tpb, 0)),
            pl.BlockSpec((TM, hv), lambda i: ((i // tpb) * lt + jnp.maximum(i % tpb - 1, 0), 0)),
            pl.BlockSpec(w_bf.shape, lambda i: (0, 0)),
        ],
        out_specs=pl.BlockSpec((TM, d), lambda i: (i, 0)),
        out_shape=jax.ShapeDtypeStruct((t, d), F32),
        compiler_params=_cparams(("parallel",)),
        name="attn_out",
    )(x, mods, o_ctx, o_lat, w_bf)


def _router_kernel(x_ref, g_ref, mod_ref, rw_ref, rb_ref,
                   h_out, eid_out, gate_out, rank_out, cnt_out, carry):
    @pl.when(pl.program_id(0) == 0)
    def _():
        carry[...] = jnp.zeros_like(carry)

    h = _norm_mod(x_ref[...], g_ref[...], mod_ref[3:4, :], mod_ref[4:5, :])
    for j in range(ROW_TILES):
        h_out[pl.ds(j, TM, stride=ROW_TILES), :] = h[:, j * LANES:(j + 1) * LANES]
    logits = lax.dot_general(rw_ref[...], h, (((1,), (1,)), ((), ())), precision=HIGHEST,
                             preferred_element_type=F32) + rb_ref[...]
    e_iota = lax.broadcasted_iota(jnp.int32, (N_EXPERTS, TM), 0).astype(F32)
    work = logits
    vals, idxs, hots = [], [], []
    for _ in range(TOP_K):
        m = jnp.max(work, axis=0, keepdims=True)
        idx = jnp.min(jnp.where(work == m, e_iota, float(N_EXPERTS)), axis=0, keepdims=True)
        hot = e_iota == idx
        vals.append(m)
        idxs.append(idx)
        hots.append(hot)
        work = jnp.where(hot, -jnp.inf, work)
    exps = [jnp.exp(vv - vals[0]) for vv in vals]
    den = exps[0]
    for e in exps[1:]:
        den = den + e
    mask = jnp.zeros((N_EXPERTS, TM), F32)
    for hot in hots:
        mask = mask + jnp.where(hot, 1.0, 0.0)
    r_i = lax.broadcasted_iota(jnp.int32, (TM, TM), 0)
    c_i = lax.broadcasted_iota(jnp.int32, (TM, TM), 1)
    upper = jnp.where(r_i < c_i, 1.0, 0.0).astype(BF16)
    prefix = jnp.dot(mask.astype(BF16), upper, preferred_element_type=F32) + carry[:, 0:1]
    for kk in range(TOP_K):
        eid_out[kk:kk + 1, :] = idxs[kk].astype(jnp.int32)
        gate_out[kk:kk + 1, :] = exps[kk] / den
        rank = jnp.sum(jnp.where(hots[kk], prefix, 0.0), axis=0, keepdims=True)
        rank_out[kk:kk + 1, :] = rank.astype(jnp.int32)
    new_carry = carry[...] + jnp.sum(mask, axis=1, keepdims=True)
    carry[...] = new_carry
    cnt_out[...] = new_carry


def _router(x, g, mods, rw_t, rb, tpb, nb):
    t, d = x.shape
    row = _mod_row_map(tpb, nb)
    tok_row = lambda dt: jax.ShapeDtypeStruct((TOP_K, t), dt)
    return pl.pallas_call(
        _router_kernel,
        grid=(t // TM,),
        in_specs=[
            pl.BlockSpec((TM, d), lambda i: (i, 0)),
            pl.BlockSpec((1, d), lambda i: (0, 0)),
            pl.BlockSpec((None, 6, d), lambda i: (row(i), 0, 0)),
            pl.BlockSpec((N_EXPERTS, d), lambda i: (0, 0)),
            pl.BlockSpec((N_EXPERTS, 1), lambda i: (0, 0)),
        ],
        out_specs=[
            pl.BlockSpec((TM * ROW_TILES, LANES), lambda i: (i, 0)),
            pl.BlockSpec((TOP_K, TM), lambda i: (0, i)),
            pl.BlockSpec((TOP_K, TM), lambda i: (0, i)),
            pl.BlockSpec((TOP_K, TM), lambda i: (0, i)),
            pl.BlockSpec((N_EXPERTS, 128), lambda i: (0, 0)),
        ],
        out_shape=[
            jax.ShapeDtypeStruct((t * ROW_TILES, LANES), F32),
            tok_row(jnp.int32), tok_row(F32), tok_row(jnp.int32),
            jax.ShapeDtypeStruct((N_EXPERTS, 128), F32),
        ],
        scratch_shapes=[pltpu.VMEM((N_EXPERTS, 128), F32)],
        compiler_params=_cparams(("arbitrary",)),
        name="router",
    )(x, g, mods, rw_t, rb)


def _dispatch_kernel(pos_ref, h_ref, hs_in, hs_out, sem):
    del hs_in

    def row_copy(tok, slot):
        src = h_ref.at[pl.ds(pl.multiple_of(tok * ROW_TILES, ROW_TILES), ROW_TILES), :]
        return pltpu.make_async_copy(src, hs_out.at[slot], sem)

    def issue(tok, _):
        for kk in range(TOP_K):
            row_copy(tok, pos_ref[0, kk * TM + tok]).start(priority=kk % 2)
        return 0

    lax.fori_loop(0, TM, issue, 0)

    def drain(tok, _):
        for kk in range(TOP_K):
            row_copy(tok, pos_ref[0, kk * TM + tok]).wait()
        return 0

    lax.fori_loop(0, TM, drain, 0)


def _dispatch(h, pos_tiles, s_pad):
    t = h.shape[0] // ROW_TILES
    zeros = jnp.zeros((s_pad, ROW_TILES, LANES), F32)
    return pl.pallas_call(
        _dispatch_kernel,
        grid=(t // TM,),
        in_specs=[
            pl.BlockSpec((None, 1, TOP_K * TM), lambda i: (i, 0, 0), memory_space=pltpu.SMEM),
            pl.BlockSpec((TM * ROW_TILES, LANES), lambda i: (i, 0)),
            pl.BlockSpec(memory_space=pl.ANY),
        ],
        out_specs=pl.BlockSpec(memory_space=pl.ANY),
        out_shape=jax.ShapeDtypeStruct((s_pad, ROW_TILES, LANES), F32),
        scratch_shapes=[pltpu.SemaphoreType.DMA(())],
        input_output_aliases={2: 0},
        compiler_params=_cparams(("arbitrary",)),
        name="moe_dispatch",
    )(pos_tiles, h, zeros)


def _expert_kernel(te_ref, nu_ref, hs_ref, wu_ref, bu_ref, wd_ref, bd_ref, ys_ref, wu_bf, wd_bf):
    i = pl.program_id(0)
    dff = wd_ref.shape[0]
    cast_rows = 128

    @pl.when(jnp.logical_or(i == 0, te_ref[i] != te_ref[jnp.maximum(i - 1, 0)]))
    def _():
        def cast(r, _):
            rows = pl.ds(pl.multiple_of(r * cast_rows, cast_rows), cast_rows)
            wu_bf[rows, :] = wu_ref[rows, :].astype(BF16)
            wd_bf[rows, :] = wd_ref[rows, :].astype(BF16)
            return 0

        lax.fori_loop(0, dff // cast_rows, cast, 0)

    @pl.when(i < nu_ref[0])
    def _():
        h = jnp.concatenate([hs_ref[pl.ds(j, TM, stride=ROW_TILES), :] for j in range(ROW_TILES)],
                            axis=1).astype(BF16)
        u = jnp.dot(h, wu_bf[...], preferred_element_type=F32) + bu_ref[...]
        glu = jnp.minimum(u[:, :dff], SWIGLU_LIMIT)
        lin = jnp.clip(u[:, dff:], -SWIGLU_LIMIT, SWIGLU_LIMIT)
        act = glu * jax.nn.sigmoid(SWIGLU_ALPHA * glu) * (lin + 1.0)
        y = jnp.dot(act.astype(BF16), wd_bf[...], preferred_element_type=F32) + bd_ref[...]
        for j in range(ROW_TILES):
            ys_ref[pl.ds(j, TM, stride=ROW_TILES), :] = y[:, j * LANES:(j + 1) * LANES]

    @pl.when(i >= nu_ref[0])
    def _():
        ys_ref[...] = jnp.zeros_like(ys_ref)


def _experts(hs, tile_expert, n_used, layer, w_up, b_up, w_down, b_down):
    s_pad = hs.shape[0] // ROW_TILES
    d, dff2 = w_up.shape[2], w_up.shape[3]
    dff = dff2 // 2
    assert d == dff
    blk = lambda a, b: pl.BlockSpec((None, None, a, b), lambda i, te, nu: (layer, te[i], 0, 0))
    grid_spec = pltpu.PrefetchScalarGridSpec(
        num_scalar_prefetch=2,
        grid=(s_pad // TM,),
        in_specs=[
            pl.BlockSpec((TM * ROW_TILES, LANES), lambda i, te, nu: (i, 0)),
            blk(d, dff2), blk(1, dff2), blk(dff, d), blk(1, d),
        ],
        out_specs=pl.BlockSpec((TM * ROW_TILES, LANES), lambda i, te, nu: (i, 0)),
        scratch_shapes=[pltpu.VMEM((d, dff2), BF16), pltpu.VMEM((dff, d), BF16)],
    )
    return pl.pallas_call(
        _expert_kernel,
        grid_spec=grid_spec,
        out_shape=jax.ShapeDtypeStruct((s_pad * ROW_TILES, LANES), F32),
        compiler_params=_cparams(("arbitrary",)),
        name="moe_experts",
    )(tile_expert, n_used, hs, w_up, b_up, w_down, b_down)


def _combine_kernel(pos_ref, x_ref, mod_ref, gate_ref, ys_ref, o_ref, ybuf, sem):
    def row_copy(tok, kk):
        slot = pos_ref[0, kk * TM + tok]
        dst = ybuf.at[pl.ds(pl.multiple_of((kk * TM + tok) * ROW_TILES, ROW_TILES), ROW_TILES), :]
        return pltpu.make_async_copy(ys_ref.at[slot], dst, sem)

    def issue(tok, _):
        for kk in range(TOP_K):
            row_copy(tok, kk).start(priority=kk % 2)
        return 0

    lax.fori_loop(0, TM, issue, 0)

    def drain(tok, _):
        for kk in range(TOP_K):
            row_copy(tok, kk).wait()
        return 0

    lax.fori_loop(0, TM, drain, 0)

    gates = [jnp.broadcast_to(gate_ref[:, kk:kk + 1], (TM, LANES)) for kk in range(TOP_K)]
    for j in range(ROW_TILES):
        cols = slice(j * LANES, (j + 1) * LANES)
        acc = gates[0] * ybuf[pl.ds(j, TM, stride=ROW_TILES), :]
        for kk in range(1, TOP_K):
            acc = acc + gates[kk] * ybuf[pl.ds(kk * TM * ROW_TILES + j, TM, stride=ROW_TILES), :]
        o_ref[:, cols] = x_ref[:, cols] + mod_ref[5:6, cols] * acc


def _combine(x, mods, gates_t, pos_tiles, ys, tpb, nb):
    t, d = x.shape
    row = _mod_row_map(tpb, nb)
    return pl.pallas_call(
        _combine_kernel,
        grid=(t // TM,),
        in_specs=[
            pl.BlockSpec((None, 1, TOP_K * TM), lambda i: (i, 0, 0), memory_space=pltpu.SMEM),
            pl.BlockSpec((TM, d), lambda i: (i, 0)),
            pl.BlockSpec((None, 6, d), lambda i: (row(i), 0, 0)),
            pl.BlockSpec((TM, TOP_K), lambda i: (i, 0)),
            pl.BlockSpec(memory_space=pl.ANY),
        ],
        out_specs=pl.BlockSpec((TM, d), lambda i: (i, 0)),
        out_shape=jax.ShapeDtypeStruct((t, d), F32),
        scratch_shapes=[pltpu.VMEM((TOP_K * TM * ROW_TILES, LANES), F32), pltpu.SemaphoreType.DMA(())],
        compiler_params=_cparams(("arbitrary",)),
        name="moe_combine",
    )(pos_tiles, x, mods, gates_t, ys)


def _moe_layer(x, g, mods, rw_t, rb, layer, w_up, b_up, w_down, b_down, tpb, nb):
    t, d = x.shape
    assert d == ROW_TILES * LANES
    n_tok_tiles = t // TM
    s_pad = t * TOP_K + N_EXPERTS * TM
    h, eid, gate, rank, cnt = _router(x, g, mods, rw_t, rb, tpb, nb)
    counts = cnt[:, 0].astype(jnp.int32)
    padded = ((counts + TM - 1) // TM) * TM
    ends = jnp.cumsum(padded)
    offs = ends - padded
    e_ids = jnp.arange(N_EXPERTS, dtype=jnp.int32)
    pos = jnp.sum(jnp.where(eid[..., None] == e_ids, offs, 0), axis=-1) + rank
    pos_tiles = pos.reshape(TOP_K, n_tok_tiles, TM).transpose(1, 0, 2).reshape(n_tok_tiles, 1, TOP_K * TM)
    tile_start = jnp.arange(s_pad // TM, dtype=jnp.int32) * TM
    tile_expert = jnp.sum((ends[None, :] <= tile_start[:, None]).astype(jnp.int32), axis=-1)
    tile_expert = jnp.minimum(tile_expert, N_EXPERTS - 1)
    n_used = (ends[-1] // TM).astype(jnp.int32).reshape(1)
    hs = _dispatch(h, pos_tiles, s_pad)
    ys = _experts(hs.reshape(s_pad * ROW_TILES, LANES), tile_expert, n_used, layer, w_up, b_up, w_down, b_down)
    return _combine(x, mods, gate.T, pos_tiles, ys.reshape(s_pad, ROW_TILES, LANES), tpb, nb)


def _final_kernel(x_ref, g_ref, o_ref):
    o_ref[...] = _rms(x_ref[...], g_ref[...])


def _final_norm(x, g, batch, tpb):
    t, d = x.shape
    lt = tpb - 1
    return pl.pallas_call(
        _final_kernel,
        grid=(batch, lt),
        in_specs=[
            pl.BlockSpec((TM, d), lambda b, j: (b * tpb + 1 + j, 0)),
            pl.BlockSpec((1, d), lambda b, j: (0, 0)),
        ],
        out_specs=pl.BlockSpec((TM, d), lambda b, j: (b * lt + j, 0)),
        out_shape=jax.ShapeDtypeStruct((batch * lt * TM, d), F32),
        compiler_params=_cparams(("parallel", "parallel")),
        name="final_norm",
    )(x, g)


def _position_tables(batch, seq, ctx_len):
    rows = seq // GRID_W
    row = jnp.broadcast_to(jnp.arange(rows, dtype=F32)[:, None], (rows, GRID_W)).reshape(-1)
    col = jnp.broadcast_to(jnp.arange(GRID_W, dtype=F32)[None, :], (rows, GRID_W)).reshape(-1)
    n_ax = MLA_ROPE // 4
    ax_freq = ROPE_BASE ** (-jnp.arange(n_ax, dtype=F32) / n_ax)
    ang_mla = jnp.concatenate([row[:, None] * ax_freq, col[:, None] * ax_freq], axis=-1)
    ret_theta = 1.0 / (RET_THETA_BASE ** jnp.linspace(0.0, 1.0, RET_DK // 2, dtype=F32))
    ang_ret = jnp.arange(seq, dtype=F32)[:, None] * ret_theta[None, :]

    def with_ctx(lat, ctx_val):
        ctx = jnp.full((ctx_len, lat.shape[1]), ctx_val, F32)
        return jnp.tile(jnp.concatenate([ctx, lat], axis=0), (batch, 1))

    ret_cos = with_ctx(jnp.cos(ang_ret), 1.0)
    ret_sin = with_ctx(jnp.sin(ang_ret), 0.0)
    cm, sm = jnp.cos(ang_mla), jnp.sin(ang_mla)
    z = jnp.zeros_like(cm)
    mla_c = with_ctx(jnp.concatenate([cm, cm, z, z], axis=-1), 1.0)
    mla_s1 = with_ctx(jnp.concatenate([-sm, z, z, z], axis=-1), 0.0)
    mla_s2 = with_ctx(jnp.concatenate([z, sm, z, z], axis=-1), 0.0)
    mla_ct = with_ctx(cm, 1.0).T
    mla_st = with_ctx(sm, 0.0).T
    return ret_cos, ret_sin, (mla_c, mla_s1, mla_s2, mla_ct, mla_st)


def _mla_weights(w_in, q_norm, kv_norm, w_uq, w_ukv):
    d = w_in.shape[0]
    wcq = w_in[:, :MLA_Q_LORA].astype(BF16)
    wckv = w_in[:, MLA_Q_LORA:MLA_Q_LORA + MLA_KV_LORA].astype(BF16)
    wkr = jnp.pad(w_in[:, MLA_Q_LORA + MLA_KV_LORA:], ((0, 0), (0, 128 - MLA_ROPE))).astype(BF16)
    wuq = w_uq.reshape(MLA_Q_LORA, MLA_HEADS, MLA_NOPE + MLA_ROPE)
    wuq = jnp.pad(wuq, ((0, 0), (0, 0), (0, MLA_HEAD_PAD - MLA_NOPE - MLA_ROPE)))
    wuqt = wuq.reshape(MLA_Q_LORA, MLA_HEADS * MLA_HEAD_PAD).T.astype(BF16)
    wukv = w_ukv.reshape(MLA_KV_LORA, MLA_HEADS, MLA_NOPE + MLA_V)
    wuk = wukv[:, :, :MLA_NOPE].reshape(MLA_KV_LORA, MLA_HEADS * MLA_NOPE).astype(BF16)
    wuvt = wukv[:, :, MLA_NOPE:].reshape(MLA_KV_LORA, MLA_HEADS * MLA_V).T.astype(BF16)
    del d
    return (wcq, wckv, wkr, q_norm.reshape(1, -1), kv_norm.reshape(1, -1), wuqt, wuk, wuvt)


def kernel(x, c, ctx, c_ctx, ret_w_in, ret_decay_logit, ret_gn, ret_w_out, mla_w_in, mla_q_norm,
           mla_kv_norm, mla_w_uq, mla_w_ukv, mla_w_out, ada_w, ada_b, norm_mix, norm_ffn, router_w,
           router_b, exp_w_up, exp_b_up, exp_w_down, exp_b_down, final_norm):
    batch, seq, d = x.shape
    ctx_len = ctx.shape[1]
    depth = ada_w.shape[0]
    assert ctx_len == TM and seq % ATT_TK == 0 and seq % (ATT_QT * TM) == 0 and batch + 1 <= 8
    nb_tok = ctx_len + seq
    tpb = nb_tok // TM

    xs = jnp.concatenate([ctx, x], axis=1).reshape(batch * nb_tok, d)
    c_rows = jnp.concatenate([c, c_ctx[None, :], jnp.zeros((8 - batch - 1, d), F32)], axis=0)
    mods_all = _ada_mods(c_rows, ada_w, ada_b)
    ret_cos, ret_sin, mla_tabs = _position_tables(batch, seq, ctx_len)

    for layer in range(depth):
        j = layer // 2
        mods = mods_all[layer]
        g_mix = norm_mix[layer].reshape(1, d)
        if layer % 2 == 0:
            p = _ret_proj(xs, g_mix, mods, ret_w_in[j].astype(BF16), ret_cos, ret_sin, tpb, batch)
            o_f, o_b = _ret_core(p, ret_decay_logit[j], batch, nb_tok)
            xs = _ret_out(xs, mods, o_f, o_b, p, ret_gn[j], ret_w_out[j].astype(BF16), tpb, batch)
        else:
            wts = _mla_weights(mla_w_in[j], mla_q_norm[j], mla_kv_norm[j], mla_w_uq[j], mla_w_ukv[j])
            qt, k, vt = _mla_proj(xs, g_mix, mods, wts, mla_tabs, tpb, batch)
            o_ctx, o_lat = _attention(qt, k, vt, batch, nb_tok)
            xs = _attn_out(xs, mods, o_ctx, o_lat, mla_w_out[j].astype(BF16), tpb, batch)
        xs = _moe_layer(xs, norm_ffn[layer].reshape(1, d), mods, router_w[layer].T,
                        router_b[layer].reshape(-1, 1), layer, exp_w_up,
                        exp_b_up.reshape(depth, N_EXPERTS, 1, -1), exp_w_down,
                        exp_b_down.reshape(depth, N_EXPERTS, 1, -1), tpb, batch)

    out = _final_norm(xs, final_norm.reshape(1, d), batch, tpb)
    return out.reshape(batch, seq, d)
```

```python
import functools

import jax
import jax.numpy as jnp
import numpy as np
from jax import lax
from jax.experimental import pallas as pl
from jax.experimental.pallas import tpu as pltpu

F32 = jnp.float32
BF16 = jnp.bfloat16
HIGHEST = lax.Precision.HIGHEST

TM = 256
LANES = 128
ROW_TILES = 8
RET_HEADS = 4
RET_DK = 256
RET_DV = 512
RET_CHUNK = 128
RET_HP = 2
RET_THETA_BASE = 10000.0
MLA_HEADS = 8
MLA_NOPE = 128
MLA_ROPE = 64
MLA_V = 128
MLA_Q_LORA = 384
MLA_KV_LORA = 256
MLA_HEAD_PAD = 256
ROPE_BASE = 10000.0
GRID_W = 64
N_EXPERTS = 32
TOP_K = 4
SWIGLU_ALPHA = 1.702
SWIGLU_LIMIT = 7.0
EPS = 1e-6
GN_EPS = 1e-5
ATT_TK = 512
ATT_QT = 4
QK_SCALE = float((MLA_NOPE + MLA_ROPE) ** -0.5 * np.log2(np.e))
VMEM_LIMIT = 48 * 1024 * 1024


def _cparams(sem):
    return pltpu.CompilerParams(dimension_semantics=sem, vmem_limit_bytes=VMEM_LIMIT)


def _norm_mod(x, g, shift, scale):
    ms = jnp.mean(x * x, axis=-1, keepdims=True)
    xn = x * lax.rsqrt(ms + EPS)
    return (xn * g) * (1.0 + scale) + shift


def _rms(x, g):
    ms = jnp.mean(x * x, axis=-1, keepdims=True)
    return x * lax.rsqrt(ms + EPS) * g


def _mod_row_map(tpb, nb):
    def f(i):
        return jnp.where(i % tpb == 0, nb, i // tpb)
    return f


def _ada_kernel(c_ref, w_ref, b_ref, o_ref):
    s = c_ref[...]
    s = s * jax.nn.sigmoid(s)
    o_ref[...] = jnp.dot(s, w_ref[...], precision=HIGHEST, preferred_element_type=F32) + b_ref[...]


def _ada_mods(c_rows, ada_w, ada_b):
    depth, d, d6 = ada_w.shape
    nt = d6 // d
    out = pl.pallas_call(
        _ada_kernel,
        grid=(depth, nt),
        in_specs=[
            pl.BlockSpec((8, d), lambda l, n: (0, 0)),
            pl.BlockSpec((None, d, d), lambda l, n: (l, 0, n)),
            pl.BlockSpec((None, 1, d), lambda l, n: (l, 0, n)),
        ],
        out_specs=pl.BlockSpec((None, 8, d), lambda l, n: (l, 0, n)),
        out_shape=jax.ShapeDtypeStruct((depth, 8, d6), F32),
        compiler_params=_cparams(("parallel", "parallel")),
        name="ada_mods",
    )(c_rows, ada_w, ada_b.reshape(depth, 1, d6))
    return out.reshape(depth, 8, nt, d)


def _ret_proj_kernel(x_ref, g_ref, mod_ref, w_ref, cos_ref, sin_ref, o_ref):
    h = _norm_mod(x_ref[...], g_ref[...], mod_ref[0:1, :], mod_ref[1:2, :]).astype(BF16)
    tn = 2 * RET_HEADS * RET_DK
    p = jnp.dot(h, w_ref[:, 0:tn], preferred_element_type=F32)
    cos = cos_ref[...]
    sin = sin_ref[...]
    half = RET_DK // 2
    for which in range(2):
        sc = 1.0 if which == 0 else RET_DK ** -0.5
        for hd in range(RET_HEADS):
            base = which * RET_HEADS * RET_DK + hd * RET_DK
            x1 = p[:, base:base + half]
            x2 = p[:, base + half:base + RET_DK]
            o_ref[:, base:base + half] = ((x1 * cos - x2 * sin) * sc).astype(BF16)
            o_ref[:, base + half:base + RET_DK] = ((x1 * sin + x2 * cos) * sc).astype(BF16)
    for n in range(1, w_ref.shape[1] // tn):
        cols = slice(n * tn, (n + 1) * tn)
        o_ref[:, cols] = jnp.dot(h, w_ref[:, cols], preferred_element_type=F32).astype(BF16)


def _ret_proj(x, g, mods, w_bf, cos_t, sin_t, tpb, nb):
    t, d = x.shape
    ncols = w_bf.shape[1]
    row = _mod_row_map(tpb, nb)
    return pl.pallas_call(
        _ret_proj_kernel,
        grid=(t // TM,),
        in_specs=[
            pl.BlockSpec((TM, d), lambda i: (i, 0)),
            pl.BlockSpec((1, d), lambda i: (0, 0)),
            pl.BlockSpec((None, 6, d), lambda i: (row(i), 0, 0)),
            pl.BlockSpec((d, ncols), lambda i: (0, 0), pipeline_mode=pl.Buffered(1)),
            pl.BlockSpec((TM, RET_DK // 2), lambda i: (i, 0)),
            pl.BlockSpec((TM, RET_DK // 2), lambda i: (i, 0)),
        ],
        out_specs=pl.BlockSpec((TM, ncols), lambda i: (i, 0)),
        out_shape=jax.ShapeDtypeStruct((t, ncols), BF16),
        compiler_params=_cparams(("parallel",)),
        name="ret_proj",
    )(x, g, mods, w_bf, cos_t, sin_t)


def _log_sigmoid(x):
    return jnp.minimum(x, 0.0) - jnp.log1p(jnp.exp(-jnp.abs(x)))


def _ret_direction(q_ref, k_ref, v_ref, o_ref, s_ref, logit, forward, hh):
    c = RET_CHUNK
    lg_row = _log_sigmoid(logit + jnp.zeros((1, c), F32))
    lg = lg_row[:, 0:1]
    i_col = lax.broadcasted_iota(jnp.int32, (c, 1), 0).astype(F32)
    i_mat = lax.broadcasted_iota(jnp.int32, (c, c), 0).astype(F32)
    j_mat = lax.broadcasted_iota(jnp.int32, (c, c), 1).astype(F32)
    if forward:
        rel = i_mat - j_mat
        q_pow = i_col + 1.0
        k_pow = (c - 1.0) - i_col
    else:
        rel = j_mat - i_mat
        q_pow = c - i_col
        k_pow = i_col
    mask = jnp.where(rel >= 0, jnp.exp(lg_row * jnp.maximum(rel, 0.0)), 0.0)
    q_decay = jnp.exp(lg * q_pow)
    k_decay = jnp.exp(lg * k_pow)
    chunk_decay = jnp.exp(lg * float(c))

    q = q_ref[:, hh * RET_DK:(hh + 1) * RET_DK]
    k = k_ref[:, hh * RET_DK:(hh + 1) * RET_DK]
    v = v_ref[:, hh * RET_DV:(hh + 1) * RET_DV]
    s = s_ref[...]
    raw = lax.dot_general(q, k, (((1,), (1,)), ((), ())), preferred_element_type=F32)
    qs = (q.astype(F32) * q_decay).astype(BF16)
    inter = jnp.dot(qs, s.astype(BF16), preferred_element_type=F32)
    ks = (k.astype(F32) * k_decay).astype(BF16)
    upd = lax.dot_general(ks, v, (((0,), (0,)), ((), ())), preferred_element_type=F32)

    def finish():
        o = jnp.dot((raw * mask).astype(BF16), v, preferred_element_type=F32) + inter
        s_ref[...] = s * chunk_decay + upd
        o_ref[:, hh * RET_DV:(hh + 1) * RET_DV] = o.astype(BF16)

    return finish


def _ret_core_kernel(dl_ref, qf, kf, vf, qb, kb, vb, of_ref, ob_ref, *states):
    hblk = pl.program_id(1)

    @pl.when(pl.program_id(2) == 0)
    def _():
        for s_ref in states:
            s_ref[...] = jnp.zeros_like(s_ref)

    finishes = []
    for hh in range(RET_HP):
        hd = hblk * RET_HP + hh
        finishes.append(_ret_direction(qf, kf, vf, of_ref, states[2 * hh], dl_ref[0, hd], True, hh))
        finishes.append(_ret_direction(qb, kb, vb, ob_ref, states[2 * hh + 1], dl_ref[1, hd], False, hh))
    for fin in finishes:
        fin()


def _ret_core(p, decay_logit, batch, nb_tok):
    t = p.shape[0]
    nc = nb_tok // RET_CHUNK
    nctx = TM // RET_CHUNK
    qw, vw = RET_HP * RET_DK, RET_HP * RET_DV
    kq = RET_HEADS * RET_DK // qw
    kv = 2 * RET_HEADS * RET_DK // vw

    def cf(b, h, s):
        return b * nc + s

    def cb(b, h, s):
        return b * nc + jnp.where(s < nctx, nctx - 1 - s, nc - 1 + nctx - s)

    def spec(width, off, cmap):
        return pl.BlockSpec((RET_CHUNK, width), lambda b, h, s: (cmap(b, h, s), off + h))

    out_f = pl.BlockSpec((RET_CHUNK, vw), lambda b, h, s: (cf(b, h, s), h))
    out_b = pl.BlockSpec((RET_CHUNK, vw), lambda b, h, s: (cb(b, h, s), h))
    return pl.pallas_call(
        _ret_core_kernel,
        grid=(batch, RET_HEADS // RET_HP, nc),
        in_specs=[
            pl.BlockSpec(memory_space=pltpu.SMEM),
            spec(qw, 0, cf), spec(qw, kq, cf), spec(vw, kv, cf),
            spec(qw, 0, cb), spec(qw, kq, cb), spec(vw, kv, cb),
        ],
        out_specs=[out_f, out_b],
        out_shape=[jax.ShapeDtypeStruct((t, RET_HEADS * RET_DV), BF16)] * 2,
        scratch_shapes=[pltpu.VMEM((RET_DK, RET_DV), F32)] * (2 * RET_HP),
        compiler_params=_cparams(("parallel", "parallel", "arbitrary")),
        name="ret_core",
    )(decay_logit, p, p, p, p, p, p)


def _group_norm_head(o):
    mu = jnp.mean(o, axis=-1, keepdims=True)
    dlt = o - mu
    var = jnp.mean(dlt * dlt, axis=-1, keepdims=True)
    return dlt * lax.rsqrt(var + GN_EPS)


def _ret_out_kernel(x_ref, mod_ref, of_ref, ob_ref, gf_ref, gb_ref, gn_ref, w_ref, o_ref):
    acc = jnp.zeros(x_ref.shape, F32)
    for hd in range(RET_HEADS):
        sl = slice(hd * RET_DV, (hd + 1) * RET_DV)
        gf = gf_ref[:, sl].astype(F32)
        gb = gb_ref[:, sl].astype(F32)
        yf = _group_norm_head(of_ref[:, sl].astype(F32)) * gn_ref[0:1, sl]
        yb = _group_norm_head(ob_ref[:, sl].astype(F32)) * gn_ref[1:2, sl]
        y = gf * jax.nn.sigmoid(gf) * yf + gb * jax.nn.sigmoid(gb) * yb
        acc = acc + jnp.dot(y.astype(BF16), w_ref[sl, :], preferred_element_type=F32)
    o_ref[...] = x_ref[...] + mod_ref[2:3, :] * acc


def _ret_out(x, mods, o_f, o_b, p, gn, w_bf, tpb, nb):
    t, d = x.shape
    hv = RET_HEADS * RET_DV
    row = _mod_row_map(tpb, nb)
    gf_blk = (2 * RET_HEADS * RET_DK + hv) // hv
    return pl.pallas_call(
        _ret_out_kernel,
        grid=(t // TM,),
        in_specs=[
            pl.BlockSpec((TM, d), lambda i: (i, 0)),
            pl.BlockSpec((None, 6, d), lambda i: (row(i), 0, 0)),
            pl.BlockSpec((TM, hv), lambda i: (i, 0)),
            pl.BlockSpec((TM, hv), lambda i: (i, 0)),
            pl.BlockSpec((TM, hv), lambda i: (i, gf_blk)),
            pl.BlockSpec((TM, hv), lambda i: (i, gf_blk + 1)),
            pl.BlockSpec((2, hv), lambda i: (0, 0)),
            pl.BlockSpec((hv, d), lambda i: (0, 0)),
        ],
        out_specs=pl.BlockSpec((TM, d), lambda i: (i, 0)),
        out_shape=jax.ShapeDtypeStruct((t, d), F32),
        compiler_params=_cparams(("parallel",)),
        name="ret_out",
    )(x, mods, o_f, o_b, p, p, gn, w_bf)


def _mla_proj_kernel(x_ref, g_ref, mod_ref, wcq_ref, wckv_ref, wkr_ref, qn_ref, kvn_ref,
                     wuqt_ref, wuk_ref, wuvt_ref, c_ref, s1_ref, s2_ref, ct_ref, st_ref,
                     qt_out, k_out, vt_out):
    h = _norm_mod(x_ref[...], g_ref[...], mod_ref[0:1, :], mod_ref[1:2, :]).astype(BF16)
    cq = jnp.dot(h, wcq_ref[...], preferred_element_type=F32)
    ckv = jnp.dot(h, wckv_ref[...], preferred_element_type=F32)
    kr = jnp.dot(h, wkr_ref[...], preferred_element_type=F32)
    cqn = _rms(cq, qn_ref[...]).astype(BF16)
    ckvn = _rms(ckv, kvn_ref[...]).astype(BF16)
    nt = (((1,), (1,)), ((), ()))
    qt = lax.dot_general(wuqt_ref[...], cqn, nt, preferred_element_type=F32)
    vt = lax.dot_general(wuvt_ref[...], ckvn, nt, preferred_element_type=F32)
    kn = jnp.dot(ckvn, wuk_ref[...], preferred_element_type=F32)
    half = MLA_ROPE // 2
    krr = (kr * c_ref[...] + pltpu.roll(kr, 128 - half, 1) * s1_ref[...]
           + pltpu.roll(kr, half, 1) * s2_ref[...]).astype(BF16)
    ct = ct_ref[...]
    st = st_ref[...]
    for hd in range(MLA_HEADS):
        b0 = hd * MLA_HEAD_PAD
        r0 = b0 + MLA_NOPE
        qt_out[b0:r0, :] = (qt[b0:r0, :] * QK_SCALE).astype(BF16)
        x1 = qt[r0:r0 + half, :]
        x2 = qt[r0 + half:r0 + 2 * half, :]
        qt_out[r0:r0 + half, :] = ((x1 * ct - x2 * st) * QK_SCALE).astype(BF16)
        qt_out[r0 + half:r0 + 2 * half, :] = ((x1 * st + x2 * ct) * QK_SCALE).astype(BF16)
        qt_out[r0 + 2 * half:b0 + MLA_HEAD_PAD, :] = jnp.zeros(
            (MLA_HEAD_PAD - MLA_NOPE - MLA_ROPE, TM), BF16)
        k_out[:, b0:r0] = kn[:, hd * MLA_NOPE:(hd + 1) * MLA_NOPE].astype(BF16)
        k_out[:, r0:b0 + MLA_HEAD_PAD] = krr
    vt_out[...] = vt.astype(BF16)


def _mla_proj(x, g, mods, wts, tabs, tpb, nb):
    t, d = x.shape
    row = _mod_row_map(tpb, nb)
    wcq, wckv, wkr, qn, kvn, wuqt, wuk, wuvt = wts
    full = lambda a: pl.BlockSpec(a.shape, lambda i: (0,) * a.ndim)
    hq = MLA_HEADS * MLA_HEAD_PAD
    hv = MLA_HEADS * MLA_V
    tok_cols = lambda r: pl.BlockSpec((r, TM), lambda i: (0, i))
    return pl.pallas_call(
        _mla_proj_kernel,
        grid=(t // TM,),
        in_specs=[
            pl.BlockSpec((TM, d), lambda i: (i, 0)),
            pl.BlockSpec((1, d), lambda i: (0, 0)),
            pl.BlockSpec((None, 6, d), lambda i: (row(i), 0, 0)),
            full(wcq), full(wckv), full(wkr), full(qn), full(kvn), full(wuqt), full(wuk), full(wuvt),
            pl.BlockSpec((TM, 128), lambda i: (i, 0)),
            pl.BlockSpec((TM, 128), lambda i: (i, 0)),
            pl.BlockSpec((TM, 128), lambda i: (i, 0)),
            tok_cols(MLA_ROPE // 2), tok_cols(MLA_ROPE // 2),
        ],
        out_specs=[tok_cols(hq), pl.BlockSpec((TM, hq), lambda i: (i, 0)), tok_cols(hv)],
        out_shape=[
            jax.ShapeDtypeStruct((hq, t), BF16),
            jax.ShapeDtypeStruct((t, hq), BF16),
            jax.ShapeDtypeStruct((hv, t), BF16),
        ],
        compiler_params=_cparams(("parallel",)),
        name="mla_proj",
    )(x, g, mods, wcq, wckv, wkr, qn, kvn, wuqt, wuk, wuvt, *tabs)


def _scores(kt, qt):
    return jnp.dot(kt, qt, preferred_element_type=F32)


def _softmax_step(s, vtt, carry):
    m, l, acc = carry
    m_new = jnp.maximum(m, jnp.max(s, axis=0, keepdims=True))
    a = jnp.exp2(m - m_new)
    p = jnp.exp2(s - m_new)
    l = a * l + jnp.sum(p, axis=0, keepdims=True)
    acc = a * acc + jnp.dot(vtt, p.astype(BF16), preferred_element_type=F32)
    return m_new, l, acc


def _softmax_init():
    return (jnp.full((1, TM), -jnp.inf, F32), jnp.zeros((1, TM), F32), jnp.zeros((MLA_V, TM), F32))


def _attn_lat_kernel(*refs, n_lat_tiles):
    qt_refs = refs[:ATT_QT]
    k_ref, vt_ref, o_ref = refs[ATT_QT:ATT_QT + 3]
    scr = refs[ATT_QT + 3:]
    s_a, s_b, acc_s = scr[:ATT_QT], scr[ATT_QT:2 * ATT_QT], scr[2 * ATT_QT:]
    chains = range(ATT_QT)

    def k_tile(j):
        return k_ref[pl.ds(pl.multiple_of(TM + j * ATT_TK, ATT_TK // 2), ATT_TK), :]

    def vt_tile(j):
        return vt_ref[:, pl.ds(pl.multiple_of(TM + j * ATT_TK, ATT_TK // 2), ATT_TK)]

    def put_scores(j, dst):
        kt = k_tile(j)
        for c in chains:
            dst[c][...] = _scores(kt, qt_refs[c][...])

    def softmax(src, j, ml):
        vtt = vt_tile(j)
        out = []
        for c in chains:
            m, l, acc = _softmax_step(src[c][...], vtt, (ml[c][0], ml[c][1], acc_s[c][...]))
            acc_s[c][...] = acc
            out.append((m, l))
        return tuple(out)

    ctx_scores = tuple(_scores(k_ref[0:TM, :], qt_refs[c][...]) for c in chains)
    ml = []
    for c in chains:
        m, l, acc = _softmax_step(ctx_scores[c], vt_ref[:, 0:TM], _softmax_init())
        acc_s[c][...] = acc
        ml.append((m, l))
    ml = tuple(ml)

    put_scores(0, s_a)

    def body(i, ml):
        j = 2 * i
        put_scores(j + 1, s_b)
        ml = softmax(s_a, j, ml)
        put_scores(j + 2, s_a)
        return softmax(s_b, j + 1, ml)

    ml = lax.fori_loop(0, n_lat_tiles // 2 - 1, body, ml)
    put_scores(n_lat_tiles - 1, s_b)
    ml = softmax(s_a, n_lat_tiles - 2, ml)
    ml = softmax(s_b, n_lat_tiles - 1, ml)
    for c in chains:
        o_ref[c * TM:(c + 1) * TM, :] = (acc_s[c][...] / ml[c][1]).T.astype(BF16)


def _attn_ctx_kernel(qt_ref, k_ref, vt_ref, o_ref):
    m, l, acc = _softmax_step(_scores(k_ref[...], qt_ref[...]), vt_ref[...], _softmax_init())
    o_ref[...] = (acc / l).T.astype(BF16)


def _attention(qt, k, vt, batch, nb_tok):
    tpb = nb_tok // TM
    lat_tok = nb_tok - TM
    n_lat_tiles = lat_tok // ATT_TK
    qb = ATT_QT * TM
    nq = lat_tok // qb
    hv = MLA_HEADS * MLA_V

    def q_spec(c):
        return pl.BlockSpec((MLA_HEAD_PAD, TM), lambda b, h, i: (h, b * tpb + 1 + ATT_QT * i + c))

    o_lat = pl.pallas_call(
        functools.partial(_attn_lat_kernel, n_lat_tiles=n_lat_tiles),
        grid=(batch, MLA_HEADS, nq),
        in_specs=[q_spec(c) for c in range(ATT_QT)] + [
            pl.BlockSpec((nb_tok, MLA_HEAD_PAD), lambda b, h, i: (b, h)),
            pl.BlockSpec((MLA_V, nb_tok), lambda b, h, i: (h, b)),
        ],
        out_specs=pl.BlockSpec((qb, MLA_V), lambda b, h, i: (b * nq + i, h)),
        out_shape=jax.ShapeDtypeStruct((batch * lat_tok, hv), BF16),
        scratch_shapes=([pltpu.VMEM((ATT_TK, TM), F32)] * (2 * ATT_QT)
                        + [pltpu.VMEM((MLA_V, TM), F32)] * ATT_QT),
        compiler_params=_cparams(("parallel", "parallel", "arbitrary")),
        name="mla_attn",
    )(*([qt] * ATT_QT), k, vt)
    o_ctx = pl.pallas_call(
        _attn_ctx_kernel,
        grid=(batch, MLA_HEADS),
        in_specs=[
            pl.BlockSpec((MLA_HEAD_PAD, TM), lambda b, h: (h, b * tpb)),
            pl.BlockSpec((TM, MLA_HEAD_PAD), lambda b, h: (b * tpb, h)),
            pl.BlockSpec((MLA_V, TM), lambda b, h: (h, b * tpb)),
        ],
        out_specs=pl.BlockSpec((TM, MLA_V), lambda b, h: (b, h)),
        out_shape=jax.ShapeDtypeStruct((batch * TM, hv), BF16),
        compiler_params=_cparams(("parallel", "parallel")),
        name="mla_attn_ctx",
    )(qt, k, vt)
    return o_ctx, o_lat


def _attn_out_kernel(x_ref, mod_ref, oc_ref, ol_ref, w_ref, out_ref, *, tpb):
    is_ctx = pl.program_id(0) % tpb == 0

    @pl.when(is_ctx)
    def _():
        y = jnp.dot(oc_ref[...], w_ref[...], preferred_element_type=F32)
        out_ref[...] = x_ref[...] + mod_ref[2:3, :] * y

    @pl.when(jnp.logical_not(is_ctx))
    def _():
        y = jnp.dot(ol_ref[...], w_ref[...], preferred_element_type=F32)
        out_ref[...] = x_ref[...] + mod_ref[2:3, :] * y


def _attn_out(x, mods, o_ctx, o_lat, w_bf, tpb, nb):
    t, d = x.shape
    row = _mod_row_map(tpb, nb)
    hv = o_lat.shape[1]
    lt = tpb - 1
    return pl.pallas_call(
        functools.partial(_attn_out_kernel, tpb=tpb),
        grid=(t // TM,),
        in_specs=[
            pl.BlockSpec((TM, d), lambda i: (i, 0)),
            pl.BlockSpec((None, 6, d), lambda i: (row(i), 0, 0)),
            pl.BlockSpec((TM, hv), lambda i: (i // tpb, 0)),
            pl.BlockSpec((TM, hv), lambda i: ((i // tpb) * lt + jnp.maximum(i % tpb - 1, 0), 0)),
            pl.BlockSpec(w_bf.shape, lambda i: (0, 0)),
        ],
        out_specs=pl.BlockSpec((TM, d), lambda i: (i, 0)),
        out_shape=jax.ShapeDtypeStruct((t, d), F32),
        compiler_params=_cparams(("parallel",)),
        name="attn_out",
    )(x, mods, o_ctx, o_lat, w_bf)


def _router_kernel(x_ref, g_ref, mod_ref, rw_ref, rb_ref,
                   h_out, eid_out, gate_out, rank_out, cnt_out, carry):
    @pl.when(pl.program_id(0) == 0)
    def _():
        carry[...] = jnp.zeros_like(carry)

    h = _norm_mod(x_ref[...], g_ref[...], mod_ref[3:4, :], mod_ref[4:5, :])
    for j in range(ROW_TILES):
        h_out[pl.ds(j, TM, stride=ROW_TILES), :] = h[:, j * LANES:(j + 1) * LANES]
    logits = lax.dot_general(rw_ref[...], h, (((1,), (1,)), ((), ())), precision=HIGHEST,
                             preferred_element_type=F32) + rb_ref[...]
    e_iota = lax.broadcasted_iota(jnp.int32, (N_EXPERTS, TM), 0).astype(F32)
    work = logits
    vals, idxs, hots = [], [], []
    for _ in range(TOP_K):
        m = jnp.max(work, axis=0, keepdims=True)
        idx = jnp.min(jnp.where(work == m, e_iota, float(N_EXPERTS)), axis=0, keepdims=True)
        hot = e_iota == idx
        vals.append(m)
        idxs.append(idx)
        hots.append(hot)
        work = jnp.where(hot, -jnp.inf, work)
    exps = [jnp.exp(vv - vals[0]) for vv in vals]
    den = exps[0]
    for e in exps[1:]:
        den = den + e
    mask = jnp.zeros((N_EXPERTS, TM), F32)
    for hot in hots:
        mask = mask + jnp.where(hot, 1.0, 0.0)
    r_i = lax.broadcasted_iota(jnp.int32, (TM, TM), 0)
    c_i = lax.broadcasted_iota(jnp.int32, (TM, TM), 1)
    upper = jnp.where(r_i < c_i, 1.0, 0.0).astype(BF16)
    prefix = jnp.dot(mask.astype(BF16), upper, preferred_element_type=F32) + carry[:, 0:1]
    for kk in range(TOP_K):
        eid_out[kk:kk + 1, :] = idxs[kk].astype(jnp.int32)
        gate_out[kk:kk + 1, :] = exps[kk] / den
        rank = jnp.sum(jnp.where(hots[kk], prefix, 0.0), axis=0, keepdims=True)
        rank_out[kk:kk + 1, :] = rank.astype(jnp.int32)
    new_carry = carry[...] + jnp.sum(mask, axis=1, keepdims=True)
    carry[...] = new_carry
    cnt_out[...] = new_carry


def _router(x, g, mods, rw_t, rb, tpb, nb):
    t, d = x.shape
    row = _mod_row_map(tpb, nb)
    tok_row = lambda dt: jax.ShapeDtypeStruct((TOP_K, t), dt)
    return pl.pallas_call(
        _router_kernel,
        grid=(t // TM,),
        in_specs=[
            pl.BlockSpec((TM, d), lambda i: (i, 0)),
            pl.BlockSpec((1, d), lambda i: (0, 0)),
            pl.BlockSpec((None, 6, d), lambda i: (row(i), 0, 0)),
            pl.BlockSpec((N_EXPERTS, d), lambda i: (0, 0)),
            pl.BlockSpec((N_EXPERTS, 1), lambda i: (0, 0)),
        ],
        out_specs=[
            pl.BlockSpec((TM * ROW_TILES, LANES), lambda i: (i, 0)),
            pl.BlockSpec((TOP_K, TM), lambda i: (0, i)),
            pl.BlockSpec((TOP_K, TM), lambda i: (0, i)),
            pl.BlockSpec((TOP_K, TM), lambda i: (0, i)),
            pl.BlockSpec((N_EXPERTS, 128), lambda i: (0, 0)),
        ],
        out_shape=[
            jax.ShapeDtypeStruct((t * ROW_TILES, LANES), F32),
            tok_row(jnp.int32), tok_row(F32), tok_row(jnp.int32),
            jax.ShapeDtypeStruct((N_EXPERTS, 128), F32),
        ],
        scratch_shapes=[pltpu.VMEM((N_EXPERTS, 128), F32)],
        compiler_params=_cparams(("arbitrary",)),
        name="router",
    )(x, g, mods, rw_t, rb)


def _dispatch_kernel(pad_start, pad_len, pos_cur, pos_prev, h_hbm, hs_out,
                     hbuf, zero_tile, load_sem, row_sem, zsem):
    i = pl.program_id(0)
    last = pl.num_programs(0) - 1
    tile_rows = TM * ROW_TILES

    def load(step):
        rows = pl.ds(pl.multiple_of(step * tile_rows, tile_rows), tile_rows)
        return pltpu.make_async_copy(h_hbm.at[rows, :], hbuf.at[step % 3], load_sem.at[step % 3])

    slot_cur = i % 3
    slot_prev = (i + 2) % 3

    def row_copy(slot, pos_ref, tok, kk):
        src = hbuf.at[slot, pl.ds(pl.multiple_of(tok * ROW_TILES, ROW_TILES), ROW_TILES), :]
        return pltpu.make_async_copy(src, hs_out.at[pos_ref[0, kk * TM + tok]], row_sem.at[slot])

    def pad_copy(e, r):
        return pltpu.make_async_copy(zero_tile, hs_out.at[pad_start[e] + r], zsem)

    @pl.when(i == 0)
    def _():
        load(0).start()
        zero_tile[...] = jnp.zeros_like(zero_tile)

        def start_e(e, _):
            lax.fori_loop(0, pad_len[e], lambda r, c: (pad_copy(e, r).start(), c)[1], 0)
            return 0

        lax.fori_loop(0, N_EXPERTS + 1, start_e, 0)

    @pl.when(i < last)
    def _():
        load(i + 1).start()

    load(i).wait()

    def issue(tok, _):
        for kk in range(TOP_K):
            row_copy(slot_cur, pos_cur, tok, kk).start(priority=kk % 2)
        return 0

    lax.fori_loop(0, TM, issue, 0)

    def drain(slot, pos_ref):
        def body(tok, _):
            for kk in range(TOP_K):
                row_copy(slot, pos_ref, tok, kk).wait()
            return 0

        lax.fori_loop(0, TM, body, 0)

    @pl.when(i > 0)
    def _():
        drain(slot_prev, pos_prev)

    @pl.when(i == last)
    def _():
        drain(slot_cur, pos_cur)

        def wait_e(e, _):
            lax.fori_loop(0, pad_len[e], lambda r, c: (pad_copy(e, r).wait(), c)[1], 0)
            return 0

        lax.fori_loop(0, N_EXPERTS + 1, wait_e, 0)


def _dispatch(h, pos_tiles, pad_start, pad_len, s_pad):
    t = h.shape[0] // ROW_TILES
    pos_spec = lambda f: pl.BlockSpec((None, 1, TOP_K * TM), f, memory_space=pltpu.SMEM)
    grid_spec = pltpu.PrefetchScalarGridSpec(
        num_scalar_prefetch=2,
        grid=(t // TM,),
        in_specs=[
            pos_spec(lambda i, ps, pn: (i, 0, 0)),
            pos_spec(lambda i, ps, pn: (jnp.maximum(i - 1, 0), 0, 0)),
            pl.BlockSpec(memory_space=pl.ANY),
        ],
        out_specs=pl.BlockSpec(memory_space=pl.ANY),
        scratch_shapes=[pltpu.VMEM((3, TM * ROW_TILES, LANES), F32), pltpu.VMEM((ROW_TILES, LANES), F32),
                        pltpu.SemaphoreType.DMA((3,)), pltpu.SemaphoreType.DMA((3,)),
                        pltpu.SemaphoreType.DMA(())],
    )
    return pl.pallas_call(
        _dispatch_kernel,
        grid_spec=grid_spec,
        out_shape=jax.ShapeDtypeStruct((s_pad, ROW_TILES, LANES), F32),
        compiler_params=_cparams(("arbitrary",)),
        name="moe_dispatch",
    )(pad_start, pad_len, pos_tiles, pos_tiles, h)


def _expert_kernel(te_ref, nu_ref, hs_ref, wu_ref, bu_ref, wd_ref, bd_ref, ys_ref, wu_bf, wd_bf):
    i = pl.program_id(0)
    dff = wd_ref.shape[0]
    cast_rows = 128

    @pl.when(jnp.logical_or(i == 0, te_ref[i] != te_ref[jnp.maximum(i - 1, 0)]))
    def _():
        def cast(r, _):
            rows = pl.ds(pl.multiple_of(r * cast_rows, cast_rows), cast_rows)
            wu_bf[rows, :] = wu_ref[rows, :].astype(BF16)
            wd_bf[rows, :] = wd_ref[rows, :].astype(BF16)
            return 0

        lax.fori_loop(0, dff // cast_rows, cast, 0)

    @pl.when(i < nu_ref[0])
    def _():
        h = jnp.concatenate([hs_ref[pl.ds(j, TM, stride=ROW_TILES), :] for j in range(ROW_TILES)],
                            axis=1).astype(BF16)
        u = jnp.dot(h, wu_bf[...], preferred_element_type=F32) + bu_ref[...]
        glu = jnp.minimum(u[:, :dff], SWIGLU_LIMIT)
        lin = jnp.clip(u[:, dff:], -SWIGLU_LIMIT, SWIGLU_LIMIT)
        act = glu * jax.nn.sigmoid(SWIGLU_ALPHA * glu) * (lin + 1.0)
        y = jnp.dot(act.astype(BF16), wd_bf[...], preferred_element_type=F32) + bd_ref[...]
        for j in range(ROW_TILES):
            ys_ref[pl.ds(j, TM, stride=ROW_TILES), :] = y[:, j * LANES:(j + 1) * LANES]

    @pl.when(i >= nu_ref[0])
    def _():
        ys_ref[...] = jnp.zeros_like(ys_ref)


def _experts(hs, tile_expert, n_used, layer, w_up, b_up, w_down, b_down):
    s_pad = hs.shape[0] // ROW_TILES
    d, dff2 = w_up.shape[2], w_up.shape[3]
    dff = dff2 // 2
    assert d == dff
    blk = lambda a, b: pl.BlockSpec((None, None, a, b), lambda i, te, nu: (layer, te[i], 0, 0))
    grid_spec = pltpu.PrefetchScalarGridSpec(
        num_scalar_prefetch=2,
        grid=(s_pad // TM,),
        in_specs=[
            pl.BlockSpec((TM * ROW_TILES, LANES), lambda i, te, nu: (jnp.minimum(i, nu[0] - 1), 0)),
            blk(d, dff2), blk(1, dff2), blk(dff, d), blk(1, d),
        ],
        out_specs=pl.BlockSpec((TM * ROW_TILES, LANES), lambda i, te, nu: (i, 0)),
        scratch_shapes=[pltpu.VMEM((d, dff2), BF16), pltpu.VMEM((dff, d), BF16)],
    )
    return pl.pallas_call(
        _expert_kernel,
        grid_spec=grid_spec,
        out_shape=jax.ShapeDtypeStruct((s_pad * ROW_TILES, LANES), F32),
        compiler_params=_cparams(("arbitrary",)),
        name="moe_experts",
    )(tile_expert, n_used, hs, w_up, b_up, w_down, b_down)


def _combine_kernel(pos_cur, pos_nxt, x_ref, mod_ref, gate_ref, ys_ref, o_ref, ybuf0, ybuf1, sem0, sem1):
    i = pl.program_id(0)
    last = pl.num_programs(0) - 1

    def row_copy(pos_ref, buf, sem, tok, kk):
        slot = pos_ref[0, kk * TM + tok]
        dst = buf.at[pl.ds(pl.multiple_of((kk * TM + tok) * ROW_TILES, ROW_TILES), ROW_TILES), :]
        return pltpu.make_async_copy(ys_ref.at[slot], dst, sem)

    def issue(pos_ref, buf, sem):
        def body(tok, _):
            for kk in range(TOP_K):
                row_copy(pos_ref, buf, sem, tok, kk).start(priority=kk % 2)
            return 0

        lax.fori_loop(0, TM, body, 0)

    def drain(pos_ref, buf, sem):
        def body(tok, _):
            for kk in range(TOP_K):
                row_copy(pos_ref, buf, sem, tok, kk).wait()
            return 0

        lax.fori_loop(0, TM, body, 0)

    def combine(buf):
        gates = [jnp.broadcast_to(gate_ref[:, kk:kk + 1], (TM, LANES)) for kk in range(TOP_K)]
        for j in range(ROW_TILES):
            cols = slice(j * LANES, (j + 1) * LANES)
            acc = gates[0] * buf[pl.ds(j, TM, stride=ROW_TILES), :]
            for kk in range(1, TOP_K):
                acc = acc + gates[kk] * buf[pl.ds(kk * TM * ROW_TILES + j, TM, stride=ROW_TILES), :]
            o_ref[:, cols] = x_ref[:, cols] + mod_ref[5:6, cols] * acc

    @pl.when(i == 0)
    def _():
        issue(pos_cur, ybuf0, sem0)

    def phase(cur_buf, cur_sem, nxt_buf, nxt_sem):
        @pl.when(i < last)
        def _():
            issue(pos_nxt, nxt_buf, nxt_sem)

        drain(pos_cur, cur_buf, cur_sem)
        combine(cur_buf)

    @pl.when(i % 2 == 0)
    def _():
        phase(ybuf0, sem0, ybuf1, sem1)

    @pl.when(i % 2 == 1)
    def _():
        phase(ybuf1, sem1, ybuf0, sem0)


def _combine(x, mods, gates_t, pos_tiles, ys, tpb, nb):
    t, d = x.shape
    row = _mod_row_map(tpb, nb)
    n_tiles = t // TM
    ybuf = pltpu.VMEM((TOP_K * TM * ROW_TILES, LANES), F32)
    return pl.pallas_call(
        _combine_kernel,
        grid=(n_tiles,),
        in_specs=[
            pl.BlockSpec((None, 1, TOP_K * TM), lambda i: (i, 0, 0), memory_space=pltpu.SMEM),
            pl.BlockSpec((None, 1, TOP_K * TM), lambda i: (jnp.minimum(i + 1, n_tiles - 1), 0, 0),
                         memory_space=pltpu.SMEM),
            pl.BlockSpec((TM, d), lambda i: (i, 0)),
            pl.BlockSpec((None, 6, d), lambda i: (row(i), 0, 0)),
            pl.BlockSpec((TM, TOP_K), lambda i: (i, 0)),
            pl.BlockSpec(memory_space=pl.ANY),
        ],
        out_specs=pl.BlockSpec((TM, d), lambda i: (i, 0)),
        out_shape=jax.ShapeDtypeStruct((t, d), F32),
        scratch_shapes=[ybuf, ybuf, pltpu.SemaphoreType.DMA(()), pltpu.SemaphoreType.DMA(())],
        compiler_params=_cparams(("arbitrary",)),
        name="moe_combine",
    )(pos_tiles, pos_tiles, x, mods, gates_t, ys)


def _moe_layer(x, g, mods, rw_t, rb, layer, w_up, b_up, w_down, b_down, tpb, nb):
    t, d = x.shape
    assert d == ROW_TILES * LANES
    n_tok_tiles = t // TM
    s_pad = t * TOP_K + N_EXPERTS * TM
    h, eid, gate, rank, cnt = _router(x, g, mods, rw_t, rb, tpb, nb)
    counts = cnt[:, 0].astype(jnp.int32)
    padded = ((counts + TM - 1) // TM) * TM
    ends = jnp.cumsum(padded)
    offs = ends - padded
    e_ids = jnp.arange(N_EXPERTS, dtype=jnp.int32)
    pos = jnp.sum(jnp.where(eid[..., None] == e_ids, offs, 0), axis=-1) + rank
    pos_tiles = pos.reshape(TOP_K, n_tok_tiles, TM).transpose(1, 0, 2).reshape(n_tok_tiles, 1, TOP_K * TM)
    tile_start = jnp.arange(s_pad // TM, dtype=jnp.int32) * TM
    tile_expert = jnp.sum((ends[None, :] <= tile_start[:, None]).astype(jnp.int32), axis=-1)
    tile_expert = jnp.minimum(tile_expert, N_EXPERTS - 1)
    n_used = (ends[-1] // TM).astype(jnp.int32).reshape(1)
    pad_start = jnp.concatenate([offs + counts, ends[-1:]])
    pad_len = jnp.concatenate([padded - counts, s_pad - ends[-1:]])
    hs = _dispatch(h, pos_tiles, pad_start, pad_len, s_pad)
    ys = _experts(hs.reshape(s_pad * ROW_TILES, LANES), tile_expert, n_used, layer, w_up, b_up, w_down, b_down)
    return _combine(x, mods, gate.T, pos_tiles, ys.reshape(s_pad, ROW_TILES, LANES), tpb, nb)


def _final_kernel(x_ref, g_ref, o_ref):
    o_ref[...] = _rms(x_ref[...], g_ref[...])


def _final_norm(x, g, batch, tpb):
    t, d = x.shape
    lt = tpb - 1
    return pl.pallas_call(
        _final_kernel,
        grid=(batch, lt),
        in_specs=[
            pl.BlockSpec((TM, d), lambda b, j: (b * tpb + 1 + j, 0)),
            pl.BlockSpec((1, d), lambda b, j: (0, 0)),
        ],
        out_specs=pl.BlockSpec((TM, d), lambda b, j: (b * lt + j, 0)),
        out_shape=jax.ShapeDtypeStruct((batch * lt * TM, d), F32),
        compiler_params=_cparams(("parallel", "parallel")),
        name="final_norm",
    )(x, g)


def _position_tables(batch, seq, ctx_len):
    rows = seq // GRID_W
    row = jnp.broadcast_to(jnp.arange(rows, dtype=F32)[:, None], (rows, GRID_W)).reshape(-1)
    col = jnp.broadcast_to(jnp.arange(GRID_W, dtype=F32)[None, :], (rows, GRID_W)).reshape(-1)
    n_ax = MLA_ROPE // 4
    ax_freq = ROPE_BASE ** (-jnp.arange(n_ax, dtype=F32) / n_ax)
    ang_mla = jnp.concatenate([row[:, None] * ax_freq, col[:, None] * ax_freq], axis=-1)
    ret_theta = 1.0 / (RET_THETA_BASE ** jnp.linspace(0.0, 1.0, RET_DK // 2, dtype=F32))
    ang_ret = jnp.arange(seq, dtype=F32)[:, None] * ret_theta[None, :]

    def with_ctx(lat, ctx_val):
        ctx = jnp.full((ctx_len, lat.shape[1]), ctx_val, F32)
        return jnp.tile(jnp.concatenate([ctx, lat], axis=0), (batch, 1))

    ret_cos = with_ctx(jnp.cos(ang_ret), 1.0)
    ret_sin = with_ctx(jnp.sin(ang_ret), 0.0)
    cm, sm = jnp.cos(ang_mla), jnp.sin(ang_mla)
    z = jnp.zeros_like(cm)
    mla_c = with_ctx(jnp.concatenate([cm, cm, z, z], axis=-1), 1.0)
    mla_s1 = with_ctx(jnp.concatenate([-sm, z, z, z], axis=-1), 0.0)
    mla_s2 = with_ctx(jnp.concatenate([z, sm, z, z], axis=-1), 0.0)
    mla_ct = with_ctx(cm, 1.0).T
    mla_st = with_ctx(sm, 0.0).T
    return ret_cos, ret_sin, (mla_c, mla_s1, mla_s2, mla_ct, mla_st)


def _mla_weights(w_in, q_norm, kv_norm, w_uq, w_ukv):
    d = w_in.shape[0]
    wcq = w_in[:, :MLA_Q_LORA].astype(BF16)
    wckv = w_in[:, MLA_Q_LORA:MLA_Q_LORA + MLA_KV_LORA].astype(BF16)
    wkr = jnp.pad(w_in[:, MLA_Q_LORA + MLA_KV_LORA:], ((0, 0), (0, 128 - MLA_ROPE))).astype(BF16)
    wuq = w_uq.reshape(MLA_Q_LORA, MLA_HEADS, MLA_NOPE + MLA_ROPE)
    wuq = jnp.pad(wuq, ((0, 0), (0, 0), (0, MLA_HEAD_PAD - MLA_NOPE - MLA_ROPE)))
    wuqt = wuq.reshape(MLA_Q_LORA, MLA_HEADS * MLA_HEAD_PAD).T.astype(BF16)
    wukv = w_ukv.reshape(MLA_KV_LORA, MLA_HEADS, MLA_NOPE + MLA_V)
    wuk = wukv[:, :, :MLA_NOPE].reshape(MLA_KV_LORA, MLA_HEADS * MLA_NOPE).astype(BF16)
    wuvt = wukv[:, :, MLA_NOPE:].reshape(MLA_KV_LORA, MLA_HEADS * MLA_V).T.astype(BF16)
    del d
    return (wcq, wckv, wkr, q_norm.reshape(1, -1), kv_norm.reshape(1, -1), wuqt, wuk, wuvt)


def kernel(x, c, ctx, c_ctx, ret_w_in, ret_decay_logit, ret_gn, ret_w_out, mla_w_in, mla_q_norm,
           mla_kv_norm, mla_w_uq, mla_w_ukv, mla_w_out, ada_w, ada_b, norm_mix, norm_ffn, router_w,
           router_b, exp_w_up, exp_b_up, exp_w_down, exp_b_down, final_norm):
    batch, seq, d = x.shape
    ctx_len = ctx.shape[1]
    depth = ada_w.shape[0]
    assert ctx_len == TM and seq % ATT_TK == 0 and seq % (ATT_QT * TM) == 0 and batch + 1 <= 8
    nb_tok = ctx_len + seq
    tpb = nb_tok // TM

    xs = jnp.concatenate([ctx, x], axis=1).reshape(batch * nb_tok, d)
    c_rows = jnp.concatenate([c, c_ctx[None, :], jnp.zeros((8 - batch - 1, d), F32)], axis=0)
    mods_all = _ada_mods(c_rows, ada_w, ada_b)
    ret_cos, ret_sin, mla_tabs = _position_tables(batch, seq, ctx_len)

    for layer in range(depth):
        j = layer // 2
        mods = mods_all[layer]
        g_mix = norm_mix[layer].reshape(1, d)
        if layer % 2 == 0:
            p = _ret_proj(xs, g_mix, mods, ret_w_in[j].astype(BF16), ret_cos, ret_sin, tpb, batch)
            o_f, o_b = _ret_core(p, ret_decay_logit[j], batch, nb_tok)
            xs = _ret_out(xs, mods, o_f, o_b, p, ret_gn[j], ret_w_out[j].astype(BF16), tpb, batch)
        else:
            wts = _mla_weights(mla_w_in[j], mla_q_norm[j], mla_kv_norm[j], mla_w_uq[j], mla_w_ukv[j])
            qt, k, vt = _mla_proj(xs, g_mix, mods, wts, mla_tabs, tpb, batch)
            o_ctx, o_lat = _attention(qt, k, vt, batch, nb_tok)
            xs = _attn_out(xs, mods, o_ctx, o_lat, mla_w_out[j].astype(BF16), tpb, batch)
        xs = _moe_layer(xs, norm_ffn[layer].reshape(1, d), mods, router_w[layer].T,
                        router_b[layer].reshape(-1, 1), layer, exp_w_up,
                        exp_b_up.reshape(depth, N_EXPERTS, 1, -1), exp_w_down,
                        exp_b_down.reshape(depth, N_EXPERTS, 1, -1), tpb, batch)

    out = _final_norm(xs, final_norm.reshape(1, d), batch, tpb)
    return out.reshape(batch, seq, d)
```

```python
import functools

import jax
import jax.numpy as jnp
import numpy as np
from jax import lax
from jax.experimental import pallas as pl
from jax.experimental.pallas import tpu as pltpu

F32 = jnp.float32
BF16 = jnp.bfloat16
HIGHEST = lax.Precision.HIGHEST

TM = 256
LANES = 128
ROW_TILES = 8
RET_HEADS = 4
RET_DK = 256
RET_DV = 512
RET_CHUNK = 128
RET_HP = 2
RET_THETA_BASE = 10000.0
MLA_HEADS = 8
MLA_NOPE = 128
MLA_ROPE = 64
MLA_V = 128
MLA_Q_LORA = 384
MLA_KV_LORA = 256
MLA_HEAD_PAD = 256
ROPE_BASE = 10000.0
GRID_W = 64
N_EXPERTS = 32
TOP_K = 4
SWIGLU_ALPHA = 1.702
SWIGLU_LIMIT = 7.0
EPS = 1e-6
GN_EPS = 1e-5
ATT_TK = 512
ATT_QT = 4
QK_SCALE = float((MLA_NOPE + MLA_ROPE) ** -0.5 * np.log2(np.e))
VMEM_LIMIT = 48 * 1024 * 1024
TE = 512
EXPERT_VMEM_LIMIT = 58 * 1024 * 1024


def _cparams(sem):
    return pltpu.CompilerParams(dimension_semantics=sem, vmem_limit_bytes=VMEM_LIMIT)


def _norm_mod(x, g, shift, scale):
    ms = jnp.mean(x * x, axis=-1, keepdims=True)
    xn = x * lax.rsqrt(ms + EPS)
    return (xn * g) * (1.0 + scale) + shift


def _rms(x, g):
    ms = jnp.mean(x * x, axis=-1, keepdims=True)
    return x * lax.rsqrt(ms + EPS) * g


def _mod_row_map(tpb, nb):
    def f(i):
        return jnp.where(i % tpb == 0, nb, i // tpb)
    return f


def _ada_kernel(c_ref, w_ref, b_ref, o_ref):
    s = c_ref[...]
    s = s * jax.nn.sigmoid(s)
    o_ref[...] = jnp.dot(s, w_ref[...], precision=HIGHEST, preferred_element_type=F32) + b_ref[...]


def _ada_mods(c_rows, ada_w, ada_b):
    depth, d, d6 = ada_w.shape
    nt = d6 // d
    out = pl.pallas_call(
        _ada_kernel,
        grid=(depth, nt),
        in_specs=[
            pl.BlockSpec((8, d), lambda l, n: (0, 0)),
            pl.BlockSpec((None, d, d), lambda l, n: (l, 0, n)),
            pl.BlockSpec((None, 1, d), lambda l, n: (l, 0, n)),
        ],
        out_specs=pl.BlockSpec((None, 8, d), lambda l, n: (l, 0, n)),
        out_shape=jax.ShapeDtypeStruct((depth, 8, d6), F32),
        compiler_params=_cparams(("parallel", "parallel")),
        name="ada_mods",
    )(c_rows, ada_w, ada_b.reshape(depth, 1, d6))
    return out.reshape(depth, 8, nt, d)


def _ret_proj_kernel(x_ref, g_ref, mod_ref, w_ref, cos_ref, sin_ref, o_ref):
    h = _norm_mod(x_ref[...], g_ref[...], mod_ref[0:1, :], mod_ref[1:2, :]).astype(BF16)
    tn = 2 * RET_HEADS * RET_DK
    p = jnp.dot(h, w_ref[:, 0:tn], preferred_element_type=F32)
    cos = cos_ref[...]
    sin = sin_ref[...]
    half = RET_DK // 2
    for which in range(2):
        sc = 1.0 if which == 0 else RET_DK ** -0.5
        for hd in range(RET_HEADS):
            base = which * RET_HEADS * RET_DK + hd * RET_DK
            x1 = p[:, base:base + half]
            x2 = p[:, base + half:base + RET_DK]
            o_ref[:, base:base + half] = ((x1 * cos - x2 * sin) * sc).astype(BF16)
            o_ref[:, base + half:base + RET_DK] = ((x1 * sin + x2 * cos) * sc).astype(BF16)
    for n in range(1, w_ref.shape[1] // tn):
        cols = slice(n * tn, (n + 1) * tn)
        o_ref[:, cols] = jnp.dot(h, w_ref[:, cols], preferred_element_type=F32).astype(BF16)


def _ret_proj(x, g, mods, w_bf, cos_t, sin_t, tpb, nb):
    t, d = x.shape
    ncols = w_bf.shape[1]
    row = _mod_row_map(tpb, nb)
    return pl.pallas_call(
        _ret_proj_kernel,
        grid=(t // TM,),
        in_specs=[
            pl.BlockSpec((TM, d), lambda i: (i, 0)),
            pl.BlockSpec((1, d), lambda i: (0, 0)),
            pl.BlockSpec((None, 6, d), lambda i: (row(i), 0, 0)),
            pl.BlockSpec((d, ncols), lambda i: (0, 0), pipeline_mode=pl.Buffered(1)),
            pl.BlockSpec((TM, RET_DK // 2), lambda i: (i, 0)),
            pl.BlockSpec((TM, RET_DK // 2), lambda i: (i, 0)),
        ],
        out_specs=pl.BlockSpec((TM, ncols), lambda i: (i, 0)),
        out_shape=jax.ShapeDtypeStruct((t, ncols), BF16),
        compiler_params=_cparams(("parallel",)),
        name="ret_proj",
    )(x, g, mods, w_bf, cos_t, sin_t)


def _log_sigmoid(x):
    return jnp.minimum(x, 0.0) - jnp.log1p(jnp.exp(-jnp.abs(x)))


def _ret_direction(q_ref, k_ref, v_ref, o_ref, s_ref, logit, forward, hh):
    c = RET_CHUNK
    lg_row = _log_sigmoid(logit + jnp.zeros((1, c), F32))
    lg = lg_row[:, 0:1]
    i_col = lax.broadcasted_iota(jnp.int32, (c, 1), 0).astype(F32)
    i_mat = lax.broadcasted_iota(jnp.int32, (c, c), 0).astype(F32)
    j_mat = lax.broadcasted_iota(jnp.int32, (c, c), 1).astype(F32)
    if forward:
        rel = i_mat - j_mat
        q_pow = i_col + 1.0
        k_pow = (c - 1.0) - i_col
    else:
        rel = j_mat - i_mat
        q_pow = c - i_col
        k_pow = i_col
    mask = jnp.where(rel >= 0, jnp.exp(lg_row * jnp.maximum(rel, 0.0)), 0.0)
    q_decay = jnp.exp(lg * q_pow)
    k_decay = jnp.exp(lg * k_pow)
    chunk_decay = jnp.exp(lg * float(c))

    q = q_ref[:, hh * RET_DK:(hh + 1) * RET_DK]
    k = k_ref[:, hh * RET_DK:(hh + 1) * RET_DK]
    v = v_ref[:, hh * RET_DV:(hh + 1) * RET_DV]
    s = s_ref[...]
    raw = lax.dot_general(q, k, (((1,), (1,)), ((), ())), preferred_element_type=F32)
    qs = (q.astype(F32) * q_decay).astype(BF16)
    inter = jnp.dot(qs, s.astype(BF16), preferred_element_type=F32)
    ks = (k.astype(F32) * k_decay).astype(BF16)
    upd = lax.dot_general(ks, v, (((0,), (0,)), ((), ())), preferred_element_type=F32)

    def finish():
        o = jnp.dot((raw * mask).astype(BF16), v, preferred_element_type=F32) + inter
        s_ref[...] = s * chunk_decay + upd
        o_ref[:, hh * RET_DV:(hh + 1) * RET_DV] = o.astype(BF16)

    return finish


def _ret_core_kernel(dl_ref, qf, kf, vf, qb, kb, vb, of_ref, ob_ref, *states):
    hblk = pl.program_id(1)

    @pl.when(pl.program_id(2) == 0)
    def _():
        for s_ref in states:
            s_ref[...] = jnp.zeros_like(s_ref)

    finishes = []
    for hh in range(RET_HP):
        hd = hblk * RET_HP + hh
        finishes.append(_ret_direction(qf, kf, vf, of_ref, states[2 * hh], dl_ref[0, hd], True, hh))
        finishes.append(_ret_direction(qb, kb, vb, ob_ref, states[2 * hh + 1], dl_ref[1, hd], False, hh))
    for fin in finishes:
        fin()


def _ret_core(p, decay_logit, batch, nb_tok):
    t = p.shape[0]
    nc = nb_tok // RET_CHUNK
    nctx = TM // RET_CHUNK
    qw, vw = RET_HP * RET_DK, RET_HP * RET_DV
    kq = RET_HEADS * RET_DK // qw
    kv = 2 * RET_HEADS * RET_DK // vw

    def cf(b, h, s):
        return b * nc + s

    def cb(b, h, s):
        return b * nc + jnp.where(s < nctx, nctx - 1 - s, nc - 1 + nctx - s)

    def spec(width, off, cmap):
        return pl.BlockSpec((RET_CHUNK, width), lambda b, h, s: (cmap(b, h, s), off + h))

    out_f = pl.BlockSpec((RET_CHUNK, vw), lambda b, h, s: (cf(b, h, s), h))
    out_b = pl.BlockSpec((RET_CHUNK, vw), lambda b, h, s: (cb(b, h, s), h))
    return pl.pallas_call(
        _ret_core_kernel,
        grid=(batch, RET_HEADS // RET_HP, nc),
        in_specs=[
            pl.BlockSpec(memory_space=pltpu.SMEM),
            spec(qw, 0, cf), spec(qw, kq, cf), spec(vw, kv, cf),
            spec(qw, 0, cb), spec(qw, kq, cb), spec(vw, kv, cb),
        ],
        out_specs=[out_f, out_b],
        out_shape=[jax.ShapeDtypeStruct((t, RET_HEADS * RET_DV), BF16)] * 2,
        scratch_shapes=[pltpu.VMEM((RET_DK, RET_DV), F32)] * (2 * RET_HP),
        compiler_params=_cparams(("parallel", "parallel", "arbitrary")),
        name="ret_core",
    )(decay_logit, p, p, p, p, p, p)


def _group_norm_head(o):
    mu = jnp.mean(o, axis=-1, keepdims=True)
    dlt = o - mu
    var = jnp.mean(dlt * dlt, axis=-1, keepdims=True)
    return dlt * lax.rsqrt(var + GN_EPS)


def _ret_out_kernel(x_ref, mod_ref, of_ref, ob_ref, gf_ref, gb_ref, gn_ref, w_ref, o_ref):
    acc = jnp.zeros(x_ref.shape, F32)
    for hd in range(RET_HEADS):
        sl = slice(hd * RET_DV, (hd + 1) * RET_DV)
        gf = gf_ref[:, sl].astype(F32)
        gb = gb_ref[:, sl].astype(F32)
        yf = _group_norm_head(of_ref[:, sl].astype(F32)) * gn_ref[0:1, sl]
        yb = _group_norm_head(ob_ref[:, sl].astype(F32)) * gn_ref[1:2, sl]
        y = gf * jax.nn.sigmoid(gf) * yf + gb * jax.nn.sigmoid(gb) * yb
        acc = acc + jnp.dot(y.astype(BF16), w_ref[sl, :], preferred_element_type=F32)
    o_ref[...] = x_ref[...] + mod_ref[2:3, :] * acc


def _ret_out(x, mods, o_f, o_b, p, gn, w_bf, tpb, nb):
    t, d = x.shape
    hv = RET_HEADS * RET_DV
    row = _mod_row_map(tpb, nb)
    gf_blk = (2 * RET_HEADS * RET_DK + hv) // hv
    return pl.pallas_call(
        _ret_out_kernel,
        grid=(t // TM,),
        in_specs=[
            pl.BlockSpec((TM, d), lambda i: (i, 0)),
            pl.BlockSpec((None, 6, d), lambda i: (row(i), 0, 0)),
            pl.BlockSpec((TM, hv), lambda i: (i, 0)),
            pl.BlockSpec((TM, hv), lambda i: (i, 0)),
            pl.BlockSpec((TM, hv), lambda i: (i, gf_blk)),
            pl.BlockSpec((TM, hv), lambda i: (i, gf_blk + 1)),
            pl.BlockSpec((2, hv), lambda i: (0, 0)),
            pl.BlockSpec((hv, d), lambda i: (0, 0)),
        ],
        out_specs=pl.BlockSpec((TM, d), lambda i: (i, 0)),
        out_shape=jax.ShapeDtypeStruct((t, d), F32),
        compiler_params=_cparams(("parallel",)),
        name="ret_out",
    )(x, mods, o_f, o_b, p, p, gn, w_bf)


def _mla_proj_kernel(x_ref, g_ref, mod_ref, wcq_ref, wckv_ref, wkr_ref, qn_ref, kvn_ref,
                     wuqt_ref, wuk_ref, wuvt_ref, c_ref, s1_ref, s2_ref, ct_ref, st_ref,
                     qt_out, k_out, vt_out):
    h = _norm_mod(x_ref[...], g_ref[...], mod_ref[0:1, :], mod_ref[1:2, :]).astype(BF16)
    cq = jnp.dot(h, wcq_ref[...], preferred_element_type=F32)
    ckv = jnp.dot(h, wckv_ref[...], preferred_element_type=F32)
    kr = jnp.dot(h, wkr_ref[...], preferred_element_type=F32)
    cqn = _rms(cq, qn_ref[...]).astype(BF16)
    ckvn = _rms(ckv, kvn_ref[...]).astype(BF16)
    nt = (((1,), (1,)), ((), ()))
    qt = lax.dot_general(wuqt_ref[...], cqn, nt, preferred_element_type=F32)
    vt = lax.dot_general(wuvt_ref[...], ckvn, nt, preferred_element_type=F32)
    kn = jnp.dot(ckvn, wuk_ref[...], preferred_element_type=F32)
    half = MLA_ROPE // 2
    krr = (kr * c_ref[...] + pltpu.roll(kr, 128 - half, 1) * s1_ref[...]
           + pltpu.roll(kr, half, 1) * s2_ref[...]).astype(BF16)
    ct = ct_ref[...]
    st = st_ref[...]
    for hd in range(MLA_HEADS):
        b0 = hd * MLA_HEAD_PAD
        r0 = b0 + MLA_NOPE
        qt_out[b0:r0, :] = (qt[b0:r0, :] * QK_SCALE).astype(BF16)
        x1 = qt[r0:r0 + half, :]
        x2 = qt[r0 + half:r0 + 2 * half, :]
        qt_out[r0:r0 + half, :] = ((x1 * ct - x2 * st) * QK_SCALE).astype(BF16)
        qt_out[r0 + half:r0 + 2 * half, :] = ((x1 * st + x2 * ct) * QK_SCALE).astype(BF16)
        qt_out[r0 + 2 * half:b0 + MLA_HEAD_PAD, :] = jnp.zeros(
            (MLA_HEAD_PAD - MLA_NOPE - MLA_ROPE, TM), BF16)
        k_out[:, b0:r0] = kn[:, hd * MLA_NOPE:(hd + 1) * MLA_NOPE].astype(BF16)
        k_out[:, r0:b0 + MLA_HEAD_PAD] = krr
    vt_out[...] = vt.astype(BF16)


def _mla_proj(x, g, mods, wts, tabs, tpb, nb):
    t, d = x.shape
    row = _mod_row_map(tpb, nb)
    wcq, wckv, wkr, qn, kvn, wuqt, wuk, wuvt = wts
    full = lambda a: pl.BlockSpec(a.shape, lambda i: (0,) * a.ndim)
    hq = MLA_HEADS * MLA_HEAD_PAD
    hv = MLA_HEADS * MLA_V
    tok_cols = lambda r: pl.BlockSpec((r, TM), lambda i: (0, i))
    return pl.pallas_call(
        _mla_proj_kernel,
        grid=(t // TM,),
        in_specs=[
            pl.BlockSpec((TM, d), lambda i: (i, 0)),
            pl.BlockSpec((1, d), lambda i: (0, 0)),
            pl.BlockSpec((None, 6, d), lambda i: (row(i), 0, 0)),
            full(wcq), full(wckv), full(wkr), full(qn), full(kvn), full(wuqt), full(wuk), full(wuvt),
            pl.BlockSpec((TM, 128), lambda i: (i, 0)),
            pl.BlockSpec((TM, 128), lambda i: (i, 0)),
            pl.BlockSpec((TM, 128), lambda i: (i, 0)),
            tok_cols(MLA_ROPE // 2), tok_cols(MLA_ROPE // 2),
        ],
        out_specs=[tok_cols(hq), pl.BlockSpec((TM, hq), lambda i: (i, 0)), tok_cols(hv)],
        out_shape=[
            jax.ShapeDtypeStruct((hq, t), BF16),
            jax.ShapeDtypeStruct((t, hq), BF16),
            jax.ShapeDtypeStruct((hv, t), BF16),
        ],
        compiler_params=_cparams(("parallel",)),
        name="mla_proj",
    )(x, g, mods, wcq, wckv, wkr, qn, kvn, wuqt, wuk, wuvt, *tabs)


def _scores(kt, qt):
    return jnp.dot(kt, qt, preferred_element_type=F32)


def _softmax_step(s, vtt, carry):
    m, l, acc = carry
    m_new = jnp.maximum(m, jnp.max(s, axis=0, keepdims=True))
    a = jnp.exp2(m - m_new)
    p = jnp.exp2(s - m_new)
    l = a * l + jnp.sum(p, axis=0, keepdims=True)
    acc = a * acc + jnp.dot(vtt, p.astype(BF16), preferred_element_type=F32)
    return m_new, l, acc


def _softmax_init():
    return (jnp.full((1, TM), -jnp.inf, F32), jnp.zeros((1, TM), F32), jnp.zeros((MLA_V, TM), F32))


def _attn_lat_kernel(*refs, n_lat_tiles):
    qt_refs = refs[:ATT_QT]
    k_ref, vt_ref, o_ref = refs[ATT_QT:ATT_QT + 3]
    scr = refs[ATT_QT + 3:]
    s_a, s_b, acc_s = scr[:ATT_QT], scr[ATT_QT:2 * ATT_QT], scr[2 * ATT_QT:]
    chains = range(ATT_QT)

    def k_tile(j):
        return k_ref[pl.ds(pl.multiple_of(TM + j * ATT_TK, ATT_TK // 2), ATT_TK), :]

    def vt_tile(j):
        return vt_ref[:, pl.ds(pl.multiple_of(TM + j * ATT_TK, ATT_TK // 2), ATT_TK)]

    def put_scores(j, dst):
        kt = k_tile(j)
        for c in chains:
            dst[c][...] = _scores(kt, qt_refs[c][...])

    def softmax(src, j, ml):
        vtt = vt_tile(j)
        out = []
        for c in chains:
            m, l, acc = _softmax_step(src[c][...], vtt, (ml[c][0], ml[c][1], acc_s[c][...]))
            acc_s[c][...] = acc
            out.append((m, l))
        return tuple(out)

    ctx_scores = tuple(_scores(k_ref[0:TM, :], qt_refs[c][...]) for c in chains)
    ml = []
    for c in chains:
        m, l, acc = _softmax_step(ctx_scores[c], vt_ref[:, 0:TM], _softmax_init())
        acc_s[c][...] = acc
        ml.append((m, l))
    ml = tuple(ml)

    put_scores(0, s_a)

    def body(i, ml):
        j = 2 * i
        put_scores(j + 1, s_b)
        ml = softmax(s_a, j, ml)
        put_scores(j + 2, s_a)
        return softmax(s_b, j + 1, ml)

    ml = lax.fori_loop(0, n_lat_tiles // 2 - 1, body, ml)
    put_scores(n_lat_tiles - 1, s_b)
    ml = softmax(s_a, n_lat_tiles - 2, ml)
    ml = softmax(s_b, n_lat_tiles - 1, ml)
    for c in chains:
        o_ref[c * TM:(c + 1) * TM, :] = (acc_s[c][...] / ml[c][1]).T.astype(BF16)


def _attn_ctx_kernel(qt_ref, k_ref, vt_ref, o_ref):
    m, l, acc = _softmax_step(_scores(k_ref[...], qt_ref[...]), vt_ref[...], _softmax_init())
    o_ref[...] = (acc / l).T.astype(BF16)


def _attention(qt, k, vt, batch, nb_tok):
    tpb = nb_tok // TM
    lat_tok = nb_tok - TM
    n_lat_tiles = lat_tok // ATT_TK
    qb = ATT_QT * TM
    nq = lat_tok // qb
    hv = MLA_HEADS * MLA_V

    def q_spec(c):
        return pl.BlockSpec((MLA_HEAD_PAD, TM), lambda b, h, i: (h, b * tpb + 1 + ATT_QT * i + c))

    o_lat = pl.pallas_call(
        functools.partial(_attn_lat_kernel, n_lat_tiles=n_lat_tiles),
        grid=(batch, MLA_HEADS, nq),
        in_specs=[q_spec(c) for c in range(ATT_QT)] + [
            pl.BlockSpec((nb_tok, MLA_HEAD_PAD), lambda b, h, i: (b, h)),
            pl.BlockSpec((MLA_V, nb_tok), lambda b, h, i: (h, b)),
        ],
        out_specs=pl.BlockSpec((qb, MLA_V), lambda b, h, i: (b * nq + i, h)),
        out_shape=jax.ShapeDtypeStruct((batch * lat_tok, hv), BF16),
        scratch_shapes=([pltpu.VMEM((ATT_TK, TM), F32)] * (2 * ATT_QT)
                        + [pltpu.VMEM((MLA_V, TM), F32)] * ATT_QT),
        compiler_params=_cparams(("parallel", "parallel", "arbitrary")),
        name="mla_attn",
    )(*([qt] * ATT_QT), k, vt)
    o_ctx = pl.pallas_call(
        _attn_ctx_kernel,
        grid=(batch, MLA_HEADS),
        in_specs=[
            pl.BlockSpec((MLA_HEAD_PAD, TM), lambda b, h: (h, b * tpb)),
            pl.BlockSpec((TM, MLA_HEAD_PAD), lambda b, h: (b * tpb, h)),
            pl.BlockSpec((MLA_V, TM), lambda b, h: (h, b * tpb)),
        ],
        out_specs=pl.BlockSpec((TM, MLA_V), lambda b, h: (b, h)),
        out_shape=jax.ShapeDtypeStruct((batch * TM, hv), BF16),
        compiler_params=_cparams(("parallel", "parallel")),
        name="mla_attn_ctx",
    )(qt, k, vt)
    return o_ctx, o_lat


def _attn_out_kernel(x_ref, mod_ref, oc_ref, ol_ref, w_ref, out_ref, *, tpb):
    is_ctx = pl.program_id(0) % tpb == 0

    @pl.when(is_ctx)
    def _():
        y = jnp.dot(oc_ref[...], w_ref[...], preferred_element_type=F32)
        out_ref[...] = x_ref[...] + mod_ref[2:3, :] * y

    @pl.when(jnp.logical_not(is_ctx))
    def _():
        y = jnp.dot(ol_ref[...], w_ref[...], preferred_element_type=F32)
        out_ref[...] = x_ref[...] + mod_ref[2:3, :] * y


def _attn_out(x, mods, o_ctx, o_lat, w_bf, tpb, nb):
    t, d = x.shape
    row = _mod_row_map(tpb, nb)
    hv = o_lat.shape[1]
    lt = tpb - 1
    return pl.pallas_call(
        functools.partial(_attn_out_kernel, tpb=tpb),
        grid=(t // TM,),
        in_specs=[
            pl.BlockSpec((TM, d), lambda i: (i, 0)),
            pl.BlockSpec((None, 6, d), lambda i: (row(i), 0, 0)),
            pl.BlockSpec((TM, hv), lambda i: (i // tpb, 0)),
            pl.BlockSpec((TM, hv), lambda i: ((i // tpb) * lt + jnp.maximum(i % tpb - 1, 0), 0)),
            pl.BlockSpec(w_bf.shape, lambda i: (0, 0)),
        ],
        out_specs=pl.BlockSpec((TM, d), lambda i: (i, 0)),
        out_shape=jax.ShapeDtypeStruct((t, d), F32),
        compiler_params=_cparams(("parallel",)),
        name="attn_out",
    )(x, mods, o_ctx, o_lat, w_bf)


def _router_kernel(x_ref, g_ref, mod_ref, rw_ref, rb_ref,
                   h_out, eid_out, gate_out, rank_out, cnt_out, carry):
    @pl.when(pl.program_id(0) == 0)
    def _():
        carry[...] = jnp.zeros_like(carry)

    h = _norm_mod(x_ref[...], g_ref[...], mod_ref[3:4, :], mod_ref[4:5, :])
    for j in range(ROW_TILES):
        h_out[pl.ds(j, TM, stride=ROW_TILES), :] = h[:, j * LANES:(j + 1) * LANES]
    logits = lax.dot_general(rw_ref[...], h, (((1,), (1,)), ((), ())), precision=HIGHEST,
                             preferred_element_type=F32) + rb_ref[...]
    e_iota = lax.broadcasted_iota(jnp.int32, (N_EXPERTS, TM), 0).astype(F32)
    work = logits
    vals, idxs, hots = [], [], []
    for _ in range(TOP_K):
        m = jnp.max(work, axis=0, keepdims=True)
        idx = jnp.min(jnp.where(work == m, e_iota, float(N_EXPERTS)), axis=0, keepdims=True)
        hot = e_iota == idx
        vals.append(m)
        idxs.append(idx)
        hots.append(hot)
        work = jnp.where(hot, -jnp.inf, work)
    exps = [jnp.exp(vv - vals[0]) for vv in vals]
    den = exps[0]
    for e in exps[1:]:
        den = den + e
    mask = jnp.zeros((N_EXPERTS, TM), F32)
    for hot in hots:
        mask = mask + jnp.where(hot, 1.0, 0.0)
    r_i = lax.broadcasted_iota(jnp.int32, (TM, TM), 0)
    c_i = lax.broadcasted_iota(jnp.int32, (TM, TM), 1)
    upper = jnp.where(r_i < c_i, 1.0, 0.0).astype(BF16)
    prefix = jnp.dot(mask.astype(BF16), upper, preferred_element_type=F32) + carry[:, 0:1]
    for kk in range(TOP_K):
        eid_out[kk:kk + 1, :] = idxs[kk].astype(jnp.int32)
        gate_out[kk:kk + 1, :] = exps[kk] / den
        rank = jnp.sum(jnp.where(hots[kk], prefix, 0.0), axis=0, keepdims=True)
        rank_out[kk:kk + 1, :] = rank.astype(jnp.int32)
    new_carry = carry[...] + jnp.sum(mask, axis=1, keepdims=True)
    carry[...] = new_carry
    cnt_out[...] = new_carry


def _router(x, g, mods, rw_t, rb, tpb, nb):
    t, d = x.shape
    row = _mod_row_map(tpb, nb)
    tok_row = lambda dt: jax.ShapeDtypeStruct((TOP_K, t), dt)
    return pl.pallas_call(
        _router_kernel,
        grid=(t // TM,),
        in_specs=[
            pl.BlockSpec((TM, d), lambda i: (i, 0)),
            pl.BlockSpec((1, d), lambda i: (0, 0)),
            pl.BlockSpec((None, 6, d), lambda i: (row(i), 0, 0)),
            pl.BlockSpec((N_EXPERTS, d), lambda i: (0, 0)),
            pl.BlockSpec((N_EXPERTS, 1), lambda i: (0, 0)),
        ],
        out_specs=[
            pl.BlockSpec((TM * ROW_TILES, LANES), lambda i: (i, 0)),
            pl.BlockSpec((TOP_K, TM), lambda i: (0, i)),
            pl.BlockSpec((TOP_K, TM), lambda i: (0, i)),
            pl.BlockSpec((TOP_K, TM), lambda i: (0, i)),
            pl.BlockSpec((N_EXPERTS, 128), lambda i: (0, 0)),
        ],
        out_shape=[
            jax.ShapeDtypeStruct((t * ROW_TILES, LANES), F32),
            tok_row(jnp.int32), tok_row(F32), tok_row(jnp.int32),
            jax.ShapeDtypeStruct((N_EXPERTS, 128), F32),
        ],
        scratch_shapes=[pltpu.VMEM((N_EXPERTS, 128), F32)],
        compiler_params=_cparams(("arbitrary",)),
        name="router",
    )(x, g, mods, rw_t, rb)


def _dispatch_kernel(pad_start, pad_len, pos_ref, h_ref, hs_out, zero_tile, sem, zsem):
    i = pl.program_id(0)
    last = pl.num_programs(0) - 1

    def row_copy(tok, kk):
        src = h_ref.at[pl.ds(pl.multiple_of(tok * ROW_TILES, ROW_TILES), ROW_TILES), :]
        return pltpu.make_async_copy(src, hs_out.at[pos_ref[0, kk * TM + tok]], sem)

    def pad_copy(e, r):
        return pltpu.make_async_copy(zero_tile, hs_out.at[pad_start[e] + r], zsem)

    @pl.when(i == 0)
    def _():
        zero_tile[...] = jnp.zeros_like(zero_tile)

        def start_e(e, _):
            lax.fori_loop(0, pad_len[e], lambda r, c: (pad_copy(e, r).start(), c)[1], 0)
            return 0

        lax.fori_loop(0, N_EXPERTS + 1, start_e, 0)

    def issue(tok, _):
        for kk in range(TOP_K):
            row_copy(tok, kk).start(priority=kk % 2)
        return 0

    lax.fori_loop(0, TM, issue, 0)

    def drain(tok, _):
        for kk in range(TOP_K):
            row_copy(tok, kk).wait()
        return 0

    lax.fori_loop(0, TM, drain, 0)

    @pl.when(i == last)
    def _():
        def wait_e(e, _):
            lax.fori_loop(0, pad_len[e], lambda r, c: (pad_copy(e, r).wait(), c)[1], 0)
            return 0

        lax.fori_loop(0, N_EXPERTS + 1, wait_e, 0)


def _dispatch(h, pos_tiles, pad_start, pad_len, s_pad):
    t = h.shape[0] // ROW_TILES
    grid_spec = pltpu.PrefetchScalarGridSpec(
        num_scalar_prefetch=2,
        grid=(t // TM,),
        in_specs=[
            pl.BlockSpec((None, 1, TOP_K * TM), lambda i, ps, pn: (i, 0, 0), memory_space=pltpu.SMEM),
            pl.BlockSpec((TM * ROW_TILES, LANES), lambda i, ps, pn: (i, 0)),
        ],
        out_specs=pl.BlockSpec(memory_space=pl.ANY),
        scratch_shapes=[pltpu.VMEM((ROW_TILES, LANES), F32), pltpu.SemaphoreType.DMA(()),
                        pltpu.SemaphoreType.DMA(())],
    )
    return pl.pallas_call(
        _dispatch_kernel,
        grid_spec=grid_spec,
        out_shape=jax.ShapeDtypeStruct((s_pad, ROW_TILES, LANES), F32),
        compiler_params=_cparams(("arbitrary",)),
        name="moe_dispatch",
    )(pad_start, pad_len, pos_tiles, h)


def _expert_kernel(te_ref, nu_ref, hs_ref, wu_ref, bu_ref, wd_ref, bd_ref, ys_ref, wu_bf, wd_bf):
    i = pl.program_id(0)
    dff = wd_ref.shape[0]
    cast_rows = 128

    @pl.when(jnp.logical_or(i == 0, te_ref[i] != te_ref[jnp.maximum(i - 1, 0)]))
    def _():
        def cast(r, _):
            rows = pl.ds(pl.multiple_of(r * cast_rows, cast_rows), cast_rows)
            wu_bf[rows, :] = wu_ref[rows, :].astype(BF16)
            wd_bf[rows, :] = wd_ref[rows, :].astype(BF16)
            return 0

        lax.fori_loop(0, dff // cast_rows, cast, 0)

    @pl.when(i < nu_ref[0])
    def _():
        h = jnp.concatenate([hs_ref[pl.ds(j, TE, stride=ROW_TILES), :] for j in range(ROW_TILES)],
                            axis=1).astype(BF16)
        u = jnp.dot(h, wu_bf[...], preferred_element_type=F32) + bu_ref[...]
        glu = jnp.minimum(u[:, :dff], SWIGLU_LIMIT)
        lin = jnp.clip(u[:, dff:], -SWIGLU_LIMIT, SWIGLU_LIMIT)
        act = glu * jax.nn.sigmoid(SWIGLU_ALPHA * glu) * (lin + 1.0)
        y = jnp.dot(act.astype(BF16), wd_bf[...], preferred_element_type=F32) + bd_ref[...]
        for j in range(ROW_TILES):
            ys_ref[pl.ds(j, TE, stride=ROW_TILES), :] = y[:, j * LANES:(j + 1) * LANES]

    @pl.when(i >= nu_ref[0])
    def _():
        ys_ref[...] = jnp.zeros_like(ys_ref)


def _experts(hs, tile_expert, n_used, layer, w_up, b_up, w_down, b_down):
    s_pad = hs.shape[0] // ROW_TILES
    d, dff2 = w_up.shape[2], w_up.shape[3]
    dff = dff2 // 2
    assert d == dff
    blk = lambda a, b: pl.BlockSpec((None, None, a, b), lambda i, te, nu: (layer, te[i], 0, 0))
    grid_spec = pltpu.PrefetchScalarGridSpec(
        num_scalar_prefetch=2,
        grid=(s_pad // TE,),
        in_specs=[
            pl.BlockSpec((TE * ROW_TILES, LANES), lambda i, te, nu: (jnp.minimum(i, nu[0] - 1), 0)),
            blk(d, dff2), blk(1, dff2), blk(dff, d), blk(1, d),
        ],
        out_specs=pl.BlockSpec((TE * ROW_TILES, LANES), lambda i, te, nu: (i, 0)),
        scratch_shapes=[pltpu.VMEM((d, dff2), BF16), pltpu.VMEM((dff, d), BF16)],
    )
    return pl.pallas_call(
        _expert_kernel,
        grid_spec=grid_spec,
        out_shape=jax.ShapeDtypeStruct((s_pad * ROW_TILES, LANES), F32),
        compiler_params=pltpu.CompilerParams(dimension_semantics=("arbitrary",),
                                             vmem_limit_bytes=EXPERT_VMEM_LIMIT),
        name="moe_experts",
    )(tile_expert, n_used, hs, w_up, b_up, w_down, b_down)


def _combine_kernel(pos_ref, x_ref, mod_ref, gate_ref, ys_ref, o_ref, ybuf, sem):
    def row_copy(tok, kk):
        slot = pos_ref[0, kk * TM + tok]
        dst = ybuf.at[pl.ds(pl.multiple_of((kk * TM + tok) * ROW_TILES, ROW_TILES), ROW_TILES), :]
        return pltpu.make_async_copy(ys_ref.at[slot], dst, sem)

    def issue(tok, _):
        for kk in range(TOP_K):
            row_copy(tok, kk).start(priority=kk % 2)
        return 0

    lax.fori_loop(0, TM, issue, 0)

    def drain(tok, _):
        for kk in range(TOP_K):
            row_copy(tok, kk).wait()
        return 0

    lax.fori_loop(0, TM, drain, 0)

    gates = [jnp.broadcast_to(gate_ref[:, kk:kk + 1], (TM, LANES)) for kk in range(TOP_K)]
    for j in range(ROW_TILES):
        cols = slice(j * LANES, (j + 1) * LANES)
        acc = gates[0] * ybuf[pl.ds(j, TM, stride=ROW_TILES), :]
        for kk in range(1, TOP_K):
            acc = acc + gates[kk] * ybuf[pl.ds(kk * TM * ROW_TILES + j, TM, stride=ROW_TILES), :]
        o_ref[:, cols] = x_ref[:, cols] + mod_ref[5:6, cols] * acc


def _combine(x, mods, gates_t, pos_tiles, ys, tpb, nb):
    t, d = x.shape
    row = _mod_row_map(tpb, nb)
    return pl.pallas_call(
        _combine_kernel,
        grid=(t // TM,),
        in_specs=[
            pl.BlockSpec((None, 1, TOP_K * TM), lambda i: (i, 0, 0), memory_space=pltpu.SMEM),
            pl.BlockSpec((TM, d), lambda i: (i, 0)),
            pl.BlockSpec((None, 6, d), lambda i: (row(i), 0, 0)),
            pl.BlockSpec((TM, TOP_K), lambda i: (i, 0)),
            pl.BlockSpec(memory_space=pl.ANY),
        ],
        out_specs=pl.BlockSpec((TM, d), lambda i: (i, 0)),
        out_shape=jax.ShapeDtypeStruct((t, d), F32),
        scratch_shapes=[pltpu.VMEM((TOP_K * TM * ROW_TILES, LANES), F32), pltpu.SemaphoreType.DMA(())],
        compiler_params=_cparams(("arbitrary",)),
        name="moe_combine",
    )(pos_tiles, x, mods, gates_t, ys)


def _moe_layer(x, g, mods, rw_t, rb, layer, w_up, b_up, w_down, b_down, tpb, nb):
    t, d = x.shape
    assert d == ROW_TILES * LANES
    n_tok_tiles = t // TM
    s_pad = t * TOP_K + N_EXPERTS * TE
    h, eid, gate, rank, cnt = _router(x, g, mods, rw_t, rb, tpb, nb)
    counts = cnt[:, 0].astype(jnp.int32)
    padded = ((counts + TE - 1) // TE) * TE
    ends = jnp.cumsum(padded)
    offs = ends - padded
    e_ids = jnp.arange(N_EXPERTS, dtype=jnp.int32)
    pos = jnp.sum(jnp.where(eid[..., None] == e_ids, offs, 0), axis=-1) + rank
    pos_tiles = pos.reshape(TOP_K, n_tok_tiles, TM).transpose(1, 0, 2).reshape(n_tok_tiles, 1, TOP_K * TM)
    tile_start = jnp.arange(s_pad // TE, dtype=jnp.int32) * TE
    tile_expert = jnp.sum((ends[None, :] <= tile_start[:, None]).astype(jnp.int32), axis=-1)
    tile_expert = jnp.minimum(tile_expert, N_EXPERTS - 1)
    n_used = (ends[-1] // TE).astype(jnp.int32).reshape(1)
    pad_start = jnp.concatenate([offs + counts, ends[-1:]])
    pad_len = jnp.concatenate([padded - counts, s_pad - ends[-1:]])
    hs = _dispatch(h, pos_tiles, pad_start, pad_len, s_pad)
    ys = _experts(hs.reshape(s_pad * ROW_TILES, LANES), tile_expert, n_used, layer, w_up, b_up, w_down, b_down)
    return _combine(x, mods, gate.T, pos_tiles, ys.reshape(s_pad, ROW_TILES, LANES), tpb, nb)


def _final_kernel(x_ref, g_ref, o_ref):
    o_ref[...] = _rms(x_ref[...], g_ref[...])


def _final_norm(x, g, batch, tpb):
    t, d = x.shape
    lt = tpb - 1
    return pl.pallas_call(
        _final_kernel,
        grid=(batch, lt),
        in_specs=[
            pl.BlockSpec((TM, d), lambda b, j: (b * tpb + 1 + j, 0)),
            pl.BlockSpec((1, d), lambda b, j: (0, 0)),
        ],
        out_specs=pl.BlockSpec((TM, d), lambda b, j: (b * lt + j, 0)),
        out_shape=jax.ShapeDtypeStruct((batch * lt * TM, d), F32),
        compiler_params=_cparams(("parallel", "parallel")),
        name="final_norm",
    )(x, g)


def _position_tables(batch, seq, ctx_len):
    rows = seq // GRID_W
    row = jnp.broadcast_to(jnp.arange(rows, dtype=F32)[:, None], (rows, GRID_W)).reshape(-1)
    col = jnp.broadcast_to(jnp.arange(GRID_W, dtype=F32)[None, :], (rows, GRID_W)).reshape(-1)
    n_ax = MLA_ROPE // 4
    ax_freq = ROPE_BASE ** (-jnp.arange(n_ax, dtype=F32) / n_ax)
    ang_mla = jnp.concatenate([row[:, None] * ax_freq, col[:, None] * ax_freq], axis=-1)
    ret_theta = 1.0 / (RET_THETA_BASE ** jnp.linspace(0.0, 1.0, RET_DK // 2, dtype=F32))
    ang_ret = jnp.arange(seq, dtype=F32)[:, None] * ret_theta[None, :]

    def with_ctx(lat, ctx_val):
        ctx = jnp.full((ctx_len, lat.shape[1]), ctx_val, F32)
        return jnp.tile(jnp.concatenate([ctx, lat], axis=0), (batch, 1))

    ret_cos = with_ctx(jnp.cos(ang_ret), 1.0)
    ret_sin = with_ctx(jnp.sin(ang_ret), 0.0)
    cm, sm = jnp.cos(ang_mla), jnp.sin(ang_mla)
    z = jnp.zeros_like(cm)
    mla_c = with_ctx(jnp.concatenate([cm, cm, z, z], axis=-1), 1.0)
    mla_s1 = with_ctx(jnp.concatenate([-sm, z, z, z], axis=-1), 0.0)
    mla_s2 = with_ctx(jnp.concatenate([z, sm, z, z], axis=-1), 0.0)
    mla_ct = with_ctx(cm, 1.0).T
    mla_st = with_ctx(sm, 0.0).T
    return ret_cos, ret_sin, (mla_c, mla_s1, mla_s2, mla_ct, mla_st)


def _mla_weights(w_in, q_norm, kv_norm, w_uq, w_ukv):
    d = w_in.shape[0]
    wcq = w_in[:, :MLA_Q_LORA].astype(BF16)
    wckv = w_in[:, MLA_Q_LORA:MLA_Q_LORA + MLA_KV_LORA].astype(BF16)
    wkr = jnp.pad(w_in[:, MLA_Q_LORA + MLA_KV_LORA:], ((0, 0), (0, 128 - MLA_ROPE))).astype(BF16)
    wuq = w_uq.reshape(MLA_Q_LORA, MLA_HEADS, MLA_NOPE + MLA_ROPE)
    wuq = jnp.pad(wuq, ((0, 0), (0, 0), (0, MLA_HEAD_PAD - MLA_NOPE - MLA_ROPE)))
    wuqt = wuq.reshape(MLA_Q_LORA, MLA_HEADS * MLA_HEAD_PAD).T.astype(BF16)
    wukv = w_ukv.reshape(MLA_KV_LORA, MLA_HEADS, MLA_NOPE + MLA_V)
    wuk = wukv[:, :, :MLA_NOPE].reshape(MLA_KV_LORA, MLA_HEADS * MLA_NOPE).astype(BF16)
    wuvt = wukv[:, :, MLA_NOPE:].reshape(MLA_KV_LORA, MLA_HEADS * MLA_V).T.astype(BF16)
    del d
    return (wcq, wckv, wkr, q_norm.reshape(1, -1), kv_norm.reshape(1, -1), wuqt, wuk, wuvt)


def kernel(x, c, ctx, c_ctx, ret_w_in, ret_decay_logit, ret_gn, ret_w_out, mla_w_in, mla_q_norm,
           mla_kv_norm, mla_w_uq, mla_w_ukv, mla_w_out, ada_w, ada_b, norm_mix, norm_ffn, router_w,
           router_b, exp_w_up, exp_b_up, exp_w_down, exp_b_down, final_norm):
    batch, seq, d = x.shape
    ctx_len = ctx.shape[1]
    depth = ada_w.shape[0]
    assert ctx_len == TM and seq % ATT_TK == 0 and seq % (ATT_QT * TM) == 0 and batch + 1 <= 8
    nb_tok = ctx_len + seq
    tpb = nb_tok // TM

    xs = jnp.concatenate([ctx, x], axis=1).reshape(batch * nb_tok, d)
    c_rows = jnp.concatenate([c, c_ctx[None, :], jnp.zeros((8 - batch - 1, d), F32)], axis=0)
    mods_all = _ada_mods(c_rows, ada_w, ada_b)
    ret_cos, ret_sin, mla_tabs = _position_tables(batch, seq, ctx_len)

    for layer in range(depth):
        j = layer // 2
        mods = mods_all[layer]
        g_mix = norm_mix[layer].reshape(1, d)
        if layer % 2 == 0:
            p = _ret_proj(xs, g_mix, mods, ret_w_in[j].astype(BF16), ret_cos, ret_sin, tpb, batch)
            o_f, o_b = _ret_core(p, ret_decay_logit[j], batch, nb_tok)
            xs = _ret_out(xs, mods, o_f, o_b, p, ret_gn[j], ret_w_out[j].astype(BF16), tpb, batch)
        else:
            wts = _mla_weights(mla_w_in[j], mla_q_norm[j], mla_kv_norm[j], mla_w_uq[j], mla_w_ukv[j])
            qt, k, vt = _mla_proj(xs, g_mix, mods, wts, mla_tabs, tpb, batch)
            o_ctx, o_lat = _attention(qt, k, vt, batch, nb_tok)
            xs = _attn_out(xs, mods, o_ctx, o_lat, mla_w_out[j].astype(BF16), tpb, batch)
        xs = _moe_layer(xs, norm_ffn[layer].reshape(1, d), mods, router_w[layer].T,
                        router_b[layer].reshape(-1, 1), layer, exp_w_up,
                        exp_b_up.reshape(depth, N_EXPERTS, 1, -1), exp_w_down,
                        exp_b_down.reshape(depth, N_EXPERTS, 1, -1), tpb, batch)

    out = _final_norm(xs, final_norm.reshape(1, d), batch, tpb)
    return out.reshape(batch, seq, d)
```

```python
import functools

import jax
import jax.numpy as jnp
import numpy as np
from jax import lax
from jax.experimental import pallas as pl
from jax.experimental.pallas import tpu as pltpu

F32 = jnp.float32
BF16 = jnp.bfloat16
HIGHEST = lax.Precision.HIGHEST

TM = 256
LANES = 128
ROW_TILES = 8
RET_HEADS = 4
RET_DK = 256
RET_DV = 512
RET_CHUNK = 128
RET_HP = 2
RET_THETA_BASE = 10000.0
MLA_HEADS = 8
MLA_NOPE = 128
MLA_ROPE = 64
MLA_V = 128
MLA_Q_LORA = 384
MLA_KV_LORA = 256
MLA_HEAD_PAD = 256
ROPE_BASE = 10000.0
GRID_W = 64
N_EXPERTS = 32
TOP_K = 4
SWIGLU_ALPHA = 1.702
SWIGLU_LIMIT = 7.0
EPS = 1e-6
GN_EPS = 1e-5
ATT_TK = 512
ATT_QT = 4
QK_SCALE = float((MLA_NOPE + MLA_ROPE) ** -0.5 * np.log2(np.e))
VMEM_LIMIT = 48 * 1024 * 1024
TE = 512
ZERO_ROWS = 256
EXPERT_VMEM_LIMIT = 58 * 1024 * 1024


def _cparams(sem):
    return pltpu.CompilerParams(dimension_semantics=sem, vmem_limit_bytes=VMEM_LIMIT)


def _norm_mod(x, g, shift, scale):
    ms = jnp.mean(x * x, axis=-1, keepdims=True)
    xn = x * lax.rsqrt(ms + EPS)
    return (xn * g) * (1.0 + scale) + shift


def _rms(x, g):
    ms = jnp.mean(x * x, axis=-1, keepdims=True)
    return x * lax.rsqrt(ms + EPS) * g


def _mod_row_map(tpb, nb):
    def f(i):
        return jnp.where(i % tpb == 0, nb, i // tpb)
    return f


def _ada_kernel(c_ref, w_ref, b_ref, o_ref):
    s = c_ref[...]
    s = s * jax.nn.sigmoid(s)
    o_ref[...] = jnp.dot(s, w_ref[...], precision=HIGHEST, preferred_element_type=F32) + b_ref[...]


def _ada_mods(c_rows, ada_w, ada_b):
    depth, d, d6 = ada_w.shape
    nt = d6 // d
    out = pl.pallas_call(
        _ada_kernel,
        grid=(depth, nt),
        in_specs=[
            pl.BlockSpec((8, d), lambda l, n: (0, 0)),
            pl.BlockSpec((None, d, d), lambda l, n: (l, 0, n)),
            pl.BlockSpec((None, 1, d), lambda l, n: (l, 0, n)),
        ],
        out_specs=pl.BlockSpec((None, 8, d), lambda l, n: (l, 0, n)),
        out_shape=jax.ShapeDtypeStruct((depth, 8, d6), F32),
        compiler_params=_cparams(("parallel", "parallel")),
        name="ada_mods",
    )(c_rows, ada_w, ada_b.reshape(depth, 1, d6))
    return out.reshape(depth, 8, nt, d)


def _ret_proj_kernel(x_ref, g_ref, mod_ref, w_ref, cos_ref, sin_ref, o_ref):
    h = _norm_mod(x_ref[...], g_ref[...], mod_ref[0:1, :], mod_ref[1:2, :]).astype(BF16)
    tn = 2 * RET_HEADS * RET_DK
    p = jnp.dot(h, w_ref[:, 0:tn], preferred_element_type=F32)
    cos = cos_ref[...]
    sin = sin_ref[...]
    half = RET_DK // 2
    for which in range(2):
        sc = 1.0 if which == 0 else RET_DK ** -0.5
        for hd in range(RET_HEADS):
            base = which * RET_HEADS * RET_DK + hd * RET_DK
            x1 = p[:, base:base + half]
            x2 = p[:, base + half:base + RET_DK]
            o_ref[:, base:base + half] = ((x1 * cos - x2 * sin) * sc).astype(BF16)
            o_ref[:, base + half:base + RET_DK] = ((x1 * sin + x2 * cos) * sc).astype(BF16)
    for n in range(1, w_ref.shape[1] // tn):
        cols = slice(n * tn, (n + 1) * tn)
        o_ref[:, cols] = jnp.dot(h, w_ref[:, cols], preferred_element_type=F32).astype(BF16)


def _ret_proj(x, g, mods, w_bf, cos_t, sin_t, tpb, nb):
    t, d = x.shape
    ncols = w_bf.shape[1]
    row = _mod_row_map(tpb, nb)
    return pl.pallas_call(
        _ret_proj_kernel,
        grid=(t // TM,),
        in_specs=[
            pl.BlockSpec((TM, d), lambda i: (i, 0)),
            pl.BlockSpec((1, d), lambda i: (0, 0)),
            pl.BlockSpec((None, 6, d), lambda i: (row(i), 0, 0)),
            pl.BlockSpec((d, ncols), lambda i: (0, 0), pipeline_mode=pl.Buffered(1)),
            pl.BlockSpec((TM, RET_DK // 2), lambda i: (i, 0)),
            pl.BlockSpec((TM, RET_DK // 2), lambda i: (i, 0)),
        ],
        out_specs=pl.BlockSpec((TM, ncols), lambda i: (i, 0)),
        out_shape=jax.ShapeDtypeStruct((t, ncols), BF16),
        compiler_params=_cparams(("parallel",)),
        name="ret_proj",
    )(x, g, mods, w_bf, cos_t, sin_t)


def _log_sigmoid(x):
    return jnp.minimum(x, 0.0) - jnp.log1p(jnp.exp(-jnp.abs(x)))


def _ret_direction(q_ref, k_ref, v_ref, o_ref, s_ref, logit, forward, hh):
    c = RET_CHUNK
    lg_row = _log_sigmoid(logit + jnp.zeros((1, c), F32))
    lg = lg_row[:, 0:1]
    i_col = lax.broadcasted_iota(jnp.int32, (c, 1), 0).astype(F32)
    i_mat = lax.broadcasted_iota(jnp.int32, (c, c), 0).astype(F32)
    j_mat = lax.broadcasted_iota(jnp.int32, (c, c), 1).astype(F32)
    if forward:
        rel = i_mat - j_mat
        q_pow = i_col + 1.0
        k_pow = (c - 1.0) - i_col
    else:
        rel = j_mat - i_mat
        q_pow = c - i_col
        k_pow = i_col
    mask = jnp.where(rel >= 0, jnp.exp(lg_row * jnp.maximum(rel, 0.0)), 0.0)
    q_decay = jnp.exp(lg * q_pow)
    k_decay = jnp.exp(lg * k_pow)
    chunk_decay = jnp.exp(lg * float(c))

    q = q_ref[:, hh * RET_DK:(hh + 1) * RET_DK]
    k = k_ref[:, hh * RET_DK:(hh + 1) * RET_DK]
    v = v_ref[:, hh * RET_DV:(hh + 1) * RET_DV]
    s = s_ref[...]
    raw = lax.dot_general(q, k, (((1,), (1,)), ((), ())), preferred_element_type=F32)
    qs = (q.astype(F32) * q_decay).astype(BF16)
    inter = jnp.dot(qs, s.astype(BF16), preferred_element_type=F32)
    ks = (k.astype(F32) * k_decay).astype(BF16)
    upd = lax.dot_general(ks, v, (((0,), (0,)), ((), ())), preferred_element_type=F32)

    def finish():
        o = jnp.dot((raw * mask).astype(BF16), v, preferred_element_type=F32) + inter
        s_ref[...] = s * chunk_decay + upd
        o_ref[:, hh * RET_DV:(hh + 1) * RET_DV] = o.astype(BF16)

    return finish


def _ret_core_kernel(dl_ref, qf, kf, vf, qb, kb, vb, of_ref, ob_ref, *states):
    hblk = pl.program_id(1)

    @pl.when(pl.program_id(2) == 0)
    def _():
        for s_ref in states:
            s_ref[...] = jnp.zeros_like(s_ref)

    finishes = []
    for hh in range(RET_HP):
        hd = hblk * RET_HP + hh
        finishes.append(_ret_direction(qf, kf, vf, of_ref, states[2 * hh], dl_ref[0, hd], True, hh))
        finishes.append(_ret_direction(qb, kb, vb, ob_ref, states[2 * hh + 1], dl_ref[1, hd], False, hh))
    for fin in finishes:
        fin()


def _ret_core(p, decay_logit, batch, nb_tok):
    t = p.shape[0]
    nc = nb_tok // RET_CHUNK
    nctx = TM // RET_CHUNK
    qw, vw = RET_HP * RET_DK, RET_HP * RET_DV
    kq = RET_HEADS * RET_DK // qw
    kv = 2 * RET_HEADS * RET_DK // vw

    def cf(b, h, s):
        return b * nc + s

    def cb(b, h, s):
        return b * nc + jnp.where(s < nctx, nctx - 1 - s, nc - 1 + nctx - s)

    def spec(width, off, cmap):
        return pl.BlockSpec((RET_CHUNK, width), lambda b, h, s: (cmap(b, h, s), off + h))

    out_f = pl.BlockSpec((RET_CHUNK, vw), lambda b, h, s: (cf(b, h, s), h))
    out_b = pl.BlockSpec((RET_CHUNK, vw), lambda b, h, s: (cb(b, h, s), h))
    return pl.pallas_call(
        _ret_core_kernel,
        grid=(batch, RET_HEADS // RET_HP, nc),
        in_specs=[
            pl.BlockSpec(memory_space=pltpu.SMEM),
            spec(qw, 0, cf), spec(qw, kq, cf), spec(vw, kv, cf),
            spec(qw, 0, cb), spec(qw, kq, cb), spec(vw, kv, cb),
        ],
        out_specs=[out_f, out_b],
        out_shape=[jax.ShapeDtypeStruct((t, RET_HEADS * RET_DV), BF16)] * 2,
        scratch_shapes=[pltpu.VMEM((RET_DK, RET_DV), F32)] * (2 * RET_HP),
        compiler_params=_cparams(("parallel", "parallel", "arbitrary")),
        name="ret_core",
    )(decay_logit, p, p, p, p, p, p)


def _group_norm_head(o):
    mu = jnp.mean(o, axis=-1, keepdims=True)
    dlt = o - mu
    var = jnp.mean(dlt * dlt, axis=-1, keepdims=True)
    return dlt * lax.rsqrt(var + GN_EPS)


def _ret_out_kernel(x_ref, mod_ref, of_ref, ob_ref, gf_ref, gb_ref, gn_ref, w_ref, o_ref):
    acc = jnp.zeros(x_ref.shape, F32)
    for hd in range(RET_HEADS):
        sl = slice(hd * RET_DV, (hd + 1) * RET_DV)
        gf = gf_ref[:, sl].astype(F32)
        gb = gb_ref[:, sl].astype(F32)
        yf = _group_norm_head(of_ref[:, sl].astype(F32)) * gn_ref[0:1, sl]
        yb = _group_norm_head(ob_ref[:, sl].astype(F32)) * gn_ref[1:2, sl]
        y = gf * jax.nn.sigmoid(gf) * yf + gb * jax.nn.sigmoid(gb) * yb
        acc = acc + jnp.dot(y.astype(BF16), w_ref[sl, :], preferred_element_type=F32)
    o_ref[...] = x_ref[...] + mod_ref[2:3, :] * acc


def _ret_out(x, mods, o_f, o_b, p, gn, w_bf, tpb, nb):
    t, d = x.shape
    hv = RET_HEADS * RET_DV
    row = _mod_row_map(tpb, nb)
    gf_blk = (2 * RET_HEADS * RET_DK + hv) // hv
    return pl.pallas_call(
        _ret_out_kernel,
        grid=(t // TM,),
        in_specs=[
            pl.BlockSpec((TM, d), lambda i: (i, 0)),
            pl.BlockSpec((None, 6, d), lambda i: (row(i), 0, 0)),
            pl.BlockSpec((TM, hv), lambda i: (i, 0)),
            pl.BlockSpec((TM, hv), lambda i: (i, 0)),
            pl.BlockSpec((TM, hv), lambda i: (i, gf_blk)),
            pl.BlockSpec((TM, hv), lambda i: (i, gf_blk + 1)),
            pl.BlockSpec((2, hv), lambda i: (0, 0)),
            pl.BlockSpec((hv, d), lambda i: (0, 0)),
        ],
        out_specs=pl.BlockSpec((TM, d), lambda i: (i, 0)),
        out_shape=jax.ShapeDtypeStruct((t, d), F32),
        compiler_params=_cparams(("parallel",)),
        name="ret_out",
    )(x, mods, o_f, o_b, p, p, gn, w_bf)


def _mla_proj_kernel(x_ref, g_ref, mod_ref, wcq_ref, wckv_ref, wkr_ref, qn_ref, kvn_ref,
                     wuqt_ref, wuk_ref, wuvt_ref, c_ref, s1_ref, s2_ref, ct_ref, st_ref,
                     qt_out, k_out, vt_out):
    h = _norm_mod(x_ref[...], g_ref[...], mod_ref[0:1, :], mod_ref[1:2, :]).astype(BF16)
    cq = jnp.dot(h, wcq_ref[...], preferred_element_type=F32)
    ckv = jnp.dot(h, wckv_ref[...], preferred_element_type=F32)
    kr = jnp.dot(h, wkr_ref[...], preferred_element_type=F32)
    cqn = _rms(cq, qn_ref[...]).astype(BF16)
    ckvn = _rms(ckv, kvn_ref[...]).astype(BF16)
    nt = (((1,), (1,)), ((), ()))
    qt = lax.dot_general(wuqt_ref[...], cqn, nt, preferred_element_type=F32)
    vt = lax.dot_general(wuvt_ref[...], ckvn, nt, preferred_element_type=F32)
    kn = jnp.dot(ckvn, wuk_ref[...], preferred_element_type=F32)
    half = MLA_ROPE // 2
    krr = (kr * c_ref[...] + pltpu.roll(kr, 128 - half, 1) * s1_ref[...]
           + pltpu.roll(kr, half, 1) * s2_ref[...]).astype(BF16)
    ct = ct_ref[...]
    st = st_ref[...]
    for hd in range(MLA_HEADS):
        b0 = hd * MLA_HEAD_PAD
        r0 = b0 + MLA_NOPE
        qt_out[b0:r0, :] = (qt[b0:r0, :] * QK_SCALE).astype(BF16)
        x1 = qt[r0:r0 + half, :]
        x2 = qt[r0 + half:r0 + 2 * half, :]
        qt_out[r0:r0 + half, :] = ((x1 * ct - x2 * st) * QK_SCALE).astype(BF16)
        qt_out[r0 + half:r0 + 2 * half, :] = ((x1 * st + x2 * ct) * QK_SCALE).astype(BF16)
        qt_out[r0 + 2 * half:b0 + MLA_HEAD_PAD, :] = jnp.zeros(
            (MLA_HEAD_PAD - MLA_NOPE - MLA_ROPE, TM), BF16)
        k_out[:, b0:r0] = kn[:, hd * MLA_NOPE:(hd + 1) * MLA_NOPE].astype(BF16)
        k_out[:, r0:b0 + MLA_HEAD_PAD] = krr
    vt_out[...] = vt.astype(BF16)


def _mla_proj(x, g, mods, wts, tabs, tpb, nb):
    t, d = x.shape
    row = _mod_row_map(tpb, nb)
    wcq, wckv, wkr, qn, kvn, wuqt, wuk, wuvt = wts
    full = lambda a: pl.BlockSpec(a.shape, lambda i: (0,) * a.ndim)
    hq = MLA_HEADS * MLA_HEAD_PAD
    hv = MLA_HEADS * MLA_V
    tok_cols = lambda r: pl.BlockSpec((r, TM), lambda i: (0, i))
    return pl.pallas_call(
        _mla_proj_kernel,
        grid=(t // TM,),
        in_specs=[
            pl.BlockSpec((TM, d), lambda i: (i, 0)),
            pl.BlockSpec((1, d), lambda i: (0, 0)),
            pl.BlockSpec((None, 6, d), lambda i: (row(i), 0, 0)),
            full(wcq), full(wckv), full(wkr), full(qn), full(kvn), full(wuqt), full(wuk), full(wuvt),
            pl.BlockSpec((TM, 128), lambda i: (i, 0)),
            pl.BlockSpec((TM, 128), lambda i: (i, 0)),
            pl.BlockSpec((TM, 128), lambda i: (i, 0)),
            tok_cols(MLA_ROPE // 2), tok_cols(MLA_ROPE // 2),
        ],
        out_specs=[tok_cols(hq), pl.BlockSpec((TM, hq), lambda i: (i, 0)), tok_cols(hv)],
        out_shape=[
            jax.ShapeDtypeStruct((hq, t), BF16),
            jax.ShapeDtypeStruct((t, hq), BF16),
            jax.ShapeDtypeStruct((hv, t), BF16),
        ],
        compiler_params=_cparams(("parallel",)),
        name="mla_proj",
    )(x, g, mods, wcq, wckv, wkr, qn, kvn, wuqt, wuk, wuvt, *tabs)


def _scores(kt, qt):
    return jnp.dot(kt, qt, preferred_element_type=F32)


def _softmax_step(s, vtt, carry):
    m, l, acc = carry
    m_new = jnp.maximum(m, jnp.max(s, axis=0, keepdims=True))
    a = jnp.exp2(m - m_new)
    p = jnp.exp2(s - m_new)
    l = a * l + jnp.sum(p, axis=0, keepdims=True)
    acc = a * acc + jnp.dot(vtt, p.astype(BF16), preferred_element_type=F32)
    return m_new, l, acc


def _softmax_init():
    return (jnp.full((1, TM), -jnp.inf, F32), jnp.zeros((1, TM), F32), jnp.zeros((MLA_V, TM), F32))


def _attn_lat_kernel(*refs, n_lat_tiles):
    qt_refs = refs[:ATT_QT]
    k_ref, vt_ref, o_ref = refs[ATT_QT:ATT_QT + 3]
    scr = refs[ATT_QT + 3:]
    s_a, s_b, acc_s = scr[:ATT_QT], scr[ATT_QT:2 * ATT_QT], scr[2 * ATT_QT:]
    chains = range(ATT_QT)

    def k_tile(j):
        return k_ref[pl.ds(pl.multiple_of(TM + j * ATT_TK, ATT_TK // 2), ATT_TK), :]

    def vt_tile(j):
        return vt_ref[:, pl.ds(pl.multiple_of(TM + j * ATT_TK, ATT_TK // 2), ATT_TK)]

    def put_scores(j, dst):
        kt = k_tile(j)
        for c in chains:
            dst[c][...] = _scores(kt, qt_refs[c][...])

    def softmax(src, j, ml):
        vtt = vt_tile(j)
        out = []
        for c in chains:
            m, l, acc = _softmax_step(src[c][...], vtt, (ml[c][0], ml[c][1], acc_s[c][...]))
            acc_s[c][...] = acc
            out.append((m, l))
        return tuple(out)

    ctx_scores = tuple(_scores(k_ref[0:TM, :], qt_refs[c][...]) for c in chains)
    ml = []
    for c in chains:
        m, l, acc = _softmax_step(ctx_scores[c], vt_ref[:, 0:TM], _softmax_init())
        acc_s[c][...] = acc
        ml.append((m, l))
    ml = tuple(ml)

    put_scores(0, s_a)

    def body(i, ml):
        j = 2 * i
        put_scores(j + 1, s_b)
        ml = softmax(s_a, j, ml)
        put_scores(j + 2, s_a)
        return softmax(s_b, j + 1, ml)

    ml = lax.fori_loop(0, n_lat_tiles // 2 - 1, body, ml)
    put_scores(n_lat_tiles - 1, s_b)
    ml = softmax(s_a, n_lat_tiles - 2, ml)
    ml = softmax(s_b, n_lat_tiles - 1, ml)
    for c in chains:
        o_ref[c * TM:(c + 1) * TM, :] = (acc_s[c][...] / ml[c][1]).T.astype(BF16)


def _attn_ctx_kernel(qt_ref, k_ref, vt_ref, o_ref):
    m, l, acc = _softmax_step(_scores(k_ref[...], qt_ref[...]), vt_ref[...], _softmax_init())
    o_ref[...] = (acc / l).T.astype(BF16)


def _attention(qt, k, vt, batch, nb_tok):
    tpb = nb_tok // TM
    lat_tok = nb_tok - TM
    n_lat_tiles = lat_tok // ATT_TK
    qb = ATT_QT * TM
    nq = lat_tok // qb
    hv = MLA_HEADS * MLA_V

    def q_spec(c):
        return pl.BlockSpec((MLA_HEAD_PAD, TM), lambda b, h, i: (h, b * tpb + 1 + ATT_QT * i + c))

    o_lat = pl.pallas_call(
        functools.partial(_attn_lat_kernel, n_lat_tiles=n_lat_tiles),
        grid=(batch, MLA_HEADS, nq),
        in_specs=[q_spec(c) for c in range(ATT_QT)] + [
            pl.BlockSpec((nb_tok, MLA_HEAD_PAD), lambda b, h, i: (b, h)),
            pl.BlockSpec((MLA_V, nb_tok), lambda b, h, i: (h, b)),
        ],
        out_specs=pl.BlockSpec((qb, MLA_V), lambda b, h, i: (b * nq + i, h)),
        out_shape=jax.ShapeDtypeStruct((batch * lat_tok, hv), BF16),
        scratch_shapes=([pltpu.VMEM((ATT_TK, TM), F32)] * (2 * ATT_QT)
                        + [pltpu.VMEM((MLA_V, TM), F32)] * ATT_QT),
        compiler_params=_cparams(("parallel", "parallel", "arbitrary")),
        name="mla_attn",
    )(*([qt] * ATT_QT), k, vt)
    o_ctx = pl.pallas_call(
        _attn_ctx_kernel,
        grid=(batch, MLA_HEADS),
        in_specs=[
            pl.BlockSpec((MLA_HEAD_PAD, TM), lambda b, h: (h, b * tpb)),
            pl.BlockSpec((TM, MLA_HEAD_PAD), lambda b, h: (b * tpb, h)),
            pl.BlockSpec((MLA_V, TM), lambda b, h: (h, b * tpb)),
        ],
        out_specs=pl.BlockSpec((TM, MLA_V), lambda b, h: (b, h)),
        out_shape=jax.ShapeDtypeStruct((batch * TM, hv), BF16),
        compiler_params=_cparams(("parallel", "parallel")),
        name="mla_attn_ctx",
    )(qt, k, vt)
    return o_ctx, o_lat


def _attn_out_kernel(x_ref, mod_ref, oc_ref, ol_ref, w_ref, out_ref, *, tpb):
    is_ctx = pl.program_id(0) % tpb == 0

    @pl.when(is_ctx)
    def _():
        y = jnp.dot(oc_ref[...], w_ref[...], preferred_element_type=F32)
        out_ref[...] = x_ref[...] + mod_ref[2:3, :] * y

    @pl.when(jnp.logical_not(is_ctx))
    def _():
        y = jnp.dot(ol_ref[...], w_ref[...], preferred_element_type=F32)
        out_ref[...] = x_ref[...] + mod_ref[2:3, :] * y


def _attn_out(x, mods, o_ctx, o_lat, w_bf, tpb, nb):
    t, d = x.shape
    row = _mod_row_map(tpb, nb)
    hv = o_lat.shape[1]
    lt = tpb - 1
    return pl.pallas_call(
        functools.partial(_attn_out_kernel, tpb=tpb),
        grid=(t // TM,),
        in_specs=[
            pl.BlockSpec((TM, d), lambda i: (i, 0)),
            pl.BlockSpec((None, 6, d), lambda i: (row(i), 0, 0)),
            pl.BlockSpec((TM, hv), lambda i: (i // tpb, 0)),
            pl.BlockSpec((TM, hv), lambda i: ((i // tpb) * lt + jnp.maximum(i % tpb - 1, 0), 0)),
            pl.BlockSpec(w_bf.shape, lambda i: (0, 0)),
        ],
        out_specs=pl.BlockSpec((TM, d), lambda i: (i, 0)),
        out_shape=jax.ShapeDtypeStruct((t, d), F32),
        compiler_params=_cparams(("parallel",)),
        name="attn_out",
    )(x, mods, o_ctx, o_lat, w_bf)


def _router_kernel(x_ref, g_ref, mod_ref, rw_ref, rb_ref,
                   h_out, eid_out, gate_out, rank_out, cnt_out, carry):
    @pl.when(pl.program_id(0) == 0)
    def _():
        carry[...] = jnp.zeros_like(carry)

    h = _norm_mod(x_ref[...], g_ref[...], mod_ref[3:4, :], mod_ref[4:5, :])
    for j in range(ROW_TILES):
        h_out[pl.ds(j, TM, stride=ROW_TILES), :] = h[:, j * LANES:(j + 1) * LANES]
    logits = lax.dot_general(rw_ref[...], h, (((1,), (1,)), ((), ())), precision=HIGHEST,
                             preferred_element_type=F32) + rb_ref[...]
    e_iota = lax.broadcasted_iota(jnp.int32, (N_EXPERTS, TM), 0).astype(F32)
    work = logits
    vals, idxs, hots = [], [], []
    for _ in range(TOP_K):
        m = jnp.max(work, axis=0, keepdims=True)
        idx = jnp.min(jnp.where(work == m, e_iota, float(N_EXPERTS)), axis=0, keepdims=True)
        hot = e_iota == idx
        vals.append(m)
        idxs.append(idx)
        hots.append(hot)
        work = jnp.where(hot, -jnp.inf, work)
    exps = [jnp.exp(vv - vals[0]) for vv in vals]
    den = exps[0]
    for e in exps[1:]:
        den = den + e
    mask = jnp.zeros((N_EXPERTS, TM), F32)
    for hot in hots:
        mask = mask + jnp.where(hot, 1.0, 0.0)
    r_i = lax.broadcasted_iota(jnp.int32, (TM, TM), 0)
    c_i = lax.broadcasted_iota(jnp.int32, (TM, TM), 1)
    upper = jnp.where(r_i < c_i, 1.0, 0.0).astype(BF16)
    prefix = jnp.dot(mask.astype(BF16), upper, preferred_element_type=F32) + carry[:, 0:1]
    for kk in range(TOP_K):
        eid_out[kk:kk + 1, :] = idxs[kk].astype(jnp.int32)
        gate_out[kk:kk + 1, :] = exps[kk] / den
        rank = jnp.sum(jnp.where(hots[kk], prefix, 0.0), axis=0, keepdims=True)
        rank_out[kk:kk + 1, :] = rank.astype(jnp.int32)
    new_carry = carry[...] + jnp.sum(mask, axis=1, keepdims=True)
    carry[...] = new_carry
    cnt_out[...] = new_carry


def _router(x, g, mods, rw_t, rb, tpb, nb):
    t, d = x.shape
    row = _mod_row_map(tpb, nb)
    tok_row = lambda dt: jax.ShapeDtypeStruct((TOP_K, t), dt)
    return pl.pallas_call(
        _router_kernel,
        grid=(t // TM,),
        in_specs=[
            pl.BlockSpec((TM, d), lambda i: (i, 0)),
            pl.BlockSpec((1, d), lambda i: (0, 0)),
            pl.BlockSpec((None, 6, d), lambda i: (row(i), 0, 0)),
            pl.BlockSpec((N_EXPERTS, d), lambda i: (0, 0)),
            pl.BlockSpec((N_EXPERTS, 1), lambda i: (0, 0)),
        ],
        out_specs=[
            pl.BlockSpec((TM * ROW_TILES, LANES), lambda i: (i, 0)),
            pl.BlockSpec((TOP_K, TM), lambda i: (0, i)),
            pl.BlockSpec((TOP_K, TM), lambda i: (0, i)),
            pl.BlockSpec((TOP_K, TM), lambda i: (0, i)),
            pl.BlockSpec((N_EXPERTS, 128), lambda i: (0, 0)),
        ],
        out_shape=[
            jax.ShapeDtypeStruct((t * ROW_TILES, LANES), F32),
            tok_row(jnp.int32), tok_row(F32), tok_row(jnp.int32),
            jax.ShapeDtypeStruct((N_EXPERTS, 128), F32),
        ],
        scratch_shapes=[pltpu.VMEM((N_EXPERTS, 128), F32)],
        compiler_params=_cparams(("arbitrary",)),
        name="router",
    )(x, g, mods, rw_t, rb)


def _dispatch_kernel(pad_start, pad_len, pos_ref, h_ref, hs_out, zero_blk, sem, zsem):
    i = pl.program_id(0)
    last = pl.num_programs(0) - 1

    def row_copy(tok, kk):
        src = h_ref.at[pl.ds(pl.multiple_of(tok * ROW_TILES, ROW_TILES), ROW_TILES), :]
        return pltpu.make_async_copy(src, hs_out.at[pos_ref[0, kk * TM + tok]], sem)

    def zero_fill(seg, start):
        base = pad_start[seg]
        n = pad_len[seg]
        n_full = n // ZERO_ROWS
        rem = n - n_full * ZERO_ROWS

        def go(copy):
            if start:
                copy.start()
            else:
                copy.wait()

        def full(c, _):
            go(pltpu.make_async_copy(zero_blk, hs_out.at[pl.ds(base + c * ZERO_ROWS, ZERO_ROWS)], zsem))
            return 0

        lax.fori_loop(0, n_full, full, 0)
        bit = ZERO_ROWS // 2
        while bit >= 1:
            above = rem - rem % (2 * bit)

            @pl.when(rem % (2 * bit) >= bit)
            def _(bit=bit, above=above):
                dst = hs_out.at[pl.ds(base + n_full * ZERO_ROWS + above, bit)]
                go(pltpu.make_async_copy(zero_blk.at[pl.ds(0, bit)], dst, zsem))

            bit //= 2

    @pl.when(i == 0)
    def _():
        zero_blk[...] = jnp.zeros_like(zero_blk)
        lax.fori_loop(0, N_EXPERTS + 1, lambda seg, c: (zero_fill(seg, True), c)[1], 0)

    def issue(tok, _):
        for kk in range(TOP_K):
            row_copy(tok, kk).start(priority=kk % 2)
        return 0

    lax.fori_loop(0, TM, issue, 0)

    def drain(tok, _):
        for kk in range(TOP_K):
            row_copy(tok, kk).wait()
        return 0

    lax.fori_loop(0, TM, drain, 0)

    @pl.when(i == last)
    def _():
        lax.fori_loop(0, N_EXPERTS + 1, lambda seg, c: (zero_fill(seg, False), c)[1], 0)


def _dispatch(h, pos_tiles, pad_start, pad_len, s_pad):
    t = h.shape[0] // ROW_TILES
    grid_spec = pltpu.PrefetchScalarGridSpec(
        num_scalar_prefetch=2,
        grid=(t // TM,),
        in_specs=[
            pl.BlockSpec((None, 1, TOP_K * TM), lambda i, ps, pn: (i, 0, 0), memory_space=pltpu.SMEM),
            pl.BlockSpec((TM * ROW_TILES, LANES), lambda i, ps, pn: (i, 0)),
        ],
        out_specs=pl.BlockSpec(memory_space=pl.ANY),
        scratch_shapes=[pltpu.VMEM((ZERO_ROWS, ROW_TILES, LANES), F32), pltpu.SemaphoreType.DMA(()),
                        pltpu.SemaphoreType.DMA(())],
    )
    return pl.pallas_call(
        _dispatch_kernel,
        grid_spec=grid_spec,
        out_shape=jax.ShapeDtypeStruct((s_pad, ROW_TILES, LANES), F32),
        compiler_params=_cparams(("arbitrary",)),
        name="moe_dispatch",
    )(pad_start, pad_len, pos_tiles, h)


def _expert_kernel(te_ref, nu_ref, hs_ref, wu_ref, bu_ref, wd_ref, bd_ref, ys_ref, wu_bf, wd_bf):
    i = pl.program_id(0)
    dff = wd_ref.shape[0]
    cast_rows = 128

    @pl.when(jnp.logical_or(i == 0, te_ref[i] != te_ref[jnp.maximum(i - 1, 0)]))
    def _():
        def cast(r, _):
            rows = pl.ds(pl.multiple_of(r * cast_rows, cast_rows), cast_rows)
            wu_bf[rows, :] = wu_ref[rows, :].astype(BF16)
            wd_bf[rows, :] = wd_ref[rows, :].astype(BF16)
            return 0

        lax.fori_loop(0, dff // cast_rows, cast, 0)

    @pl.when(i < nu_ref[0])
    def _():
        h = jnp.concatenate([hs_ref[pl.ds(j, TE, stride=ROW_TILES), :] for j in range(ROW_TILES)],
                            axis=1).astype(BF16)
        u = jnp.dot(h, wu_bf[...], preferred_element_type=F32) + bu_ref[...]
        glu = jnp.minimum(u[:, :dff], SWIGLU_LIMIT)
        lin = jnp.clip(u[:, dff:], -SWIGLU_LIMIT, SWIGLU_LIMIT)
        act = glu * jax.nn.sigmoid(SWIGLU_ALPHA * glu) * (lin + 1.0)
        y = jnp.dot(act.astype(BF16), wd_bf[...], preferred_element_type=F32) + bd_ref[...]
        for j in range(ROW_TILES):
            ys_ref[pl.ds(j, TE, stride=ROW_TILES), :] = y[:, j * LANES:(j + 1) * LANES]

    @pl.when(i >= nu_ref[0])
    def _():
        ys_ref[...] = jnp.zeros_like(ys_ref)


def _experts(hs, tile_expert, n_used, layer, w_up, b_up, w_down, b_down):
    s_pad = hs.shape[0] // ROW_TILES
    d, dff2 = w_up.shape[2], w_up.shape[3]
    dff = dff2 // 2
    assert d == dff
    blk = lambda a, b: pl.BlockSpec((None, None, a, b), lambda i, te, nu: (layer, te[i], 0, 0))
    grid_spec = pltpu.PrefetchScalarGridSpec(
        num_scalar_prefetch=2,
        grid=(s_pad // TE,),
        in_specs=[
            pl.BlockSpec((TE * ROW_TILES, LANES), lambda i, te, nu: (jnp.minimum(i, nu[0] - 1), 0)),
            blk(d, dff2), blk(1, dff2), blk(dff, d), blk(1, d),
        ],
        out_specs=pl.BlockSpec((TE * ROW_TILES, LANES), lambda i, te, nu: (i, 0)),
        scratch_shapes=[pltpu.VMEM((d, dff2), BF16), pltpu.VMEM((dff, d), BF16)],
    )
    return pl.pallas_call(
        _expert_kernel,
        grid_spec=grid_spec,
        out_shape=jax.ShapeDtypeStruct((s_pad * ROW_TILES, LANES), F32),
        compiler_params=pltpu.CompilerParams(dimension_semantics=("arbitrary",),
                                             vmem_limit_bytes=EXPERT_VMEM_LIMIT),
        name="moe_experts",
    )(tile_expert, n_used, hs, w_up, b_up, w_down, b_down)


def _combine_kernel(pos_ref, x_ref, mod_ref, gate_ref, ys_ref, o_ref, ybuf, sem):
    def row_copy(tok, kk):
        slot = pos_ref[0, kk * TM + tok]
        dst = ybuf.at[pl.ds(pl.multiple_of((kk * TM + tok) * ROW_TILES, ROW_TILES), ROW_TILES), :]
        return pltpu.make_async_copy(ys_ref.at[slot], dst, sem)

    def issue(tok, _):
        for kk in range(TOP_K):
            row_copy(tok, kk).start(priority=kk % 2)
        return 0

    lax.fori_loop(0, TM, issue, 0)

    def drain(tok, _):
        for kk in range(TOP_K):
            row_copy(tok, kk).wait()
        return 0

    lax.fori_loop(0, TM, drain, 0)

    gates = [jnp.broadcast_to(gate_ref[:, kk:kk + 1], (TM, LANES)) for kk in range(TOP_K)]
    for j in range(ROW_TILES):
        cols = slice(j * LANES, (j + 1) * LANES)
        acc = gates[0] * ybuf[pl.ds(j, TM, stride=ROW_TILES), :]
        for kk in range(1, TOP_K):
            acc = acc + gates[kk] * ybuf[pl.ds(kk * TM * ROW_TILES + j, TM, stride=ROW_TILES), :]
        o_ref[:, cols] = x_ref[:, cols] + mod_ref[5:6, cols] * acc


def _combine(x, mods, gates_t, pos_tiles, ys, tpb, nb):
    t, d = x.shape
    row = _mod_row_map(tpb, nb)
    return pl.pallas_call(
        _combine_kernel,
        grid=(t // TM,),
        in_specs=[
            pl.BlockSpec((None, 1, TOP_K * TM), lambda i: (i, 0, 0), memory_space=pltpu.SMEM),
            pl.BlockSpec((TM, d), lambda i: (i, 0)),
            pl.BlockSpec((None, 6, d), lambda i: (row(i), 0, 0)),
            pl.BlockSpec((TM, TOP_K), lambda i: (i, 0)),
            pl.BlockSpec(memory_space=pl.ANY),
        ],
        out_specs=pl.BlockSpec((TM, d), lambda i: (i, 0)),
        out_shape=jax.ShapeDtypeStruct((t, d), F32),
        scratch_shapes=[pltpu.VMEM((TOP_K * TM * ROW_TILES, LANES), F32), pltpu.SemaphoreType.DMA(())],
        compiler_params=_cparams(("arbitrary",)),
        name="moe_combine",
    )(pos_tiles, x, mods, gates_t, ys)


def _moe_layer(x, g, mods, rw_t, rb, layer, w_up, b_up, w_down, b_down, tpb, nb):
    t, d = x.shape
    assert d == ROW_TILES * LANES
    n_tok_tiles = t // TM
    s_pad = t * TOP_K + N_EXPERTS * TE
    h, eid, gate, rank, cnt = _router(x, g, mods, rw_t, rb, tpb, nb)
    counts = cnt[:, 0].astype(jnp.int32)
    padded = ((counts + TE - 1) // TE) * TE
    ends = jnp.cumsum(padded)
    offs = ends - padded
    e_ids = jnp.arange(N_EXPERTS, dtype=jnp.int32)
    pos = jnp.sum(jnp.where(eid[..., None] == e_ids, offs, 0), axis=-1) + rank
    pos_tiles = pos.reshape(TOP_K, n_tok_tiles, TM).transpose(1, 0, 2).reshape(n_tok_tiles, 1, TOP_K * TM)
    tile_start = jnp.arange(s_pad // TE, dtype=jnp.int32) * TE
    tile_expert = jnp.sum((ends[None, :] <= tile_start[:, None]).astype(jnp.int32), axis=-1)
    tile_expert = jnp.minimum(tile_expert, N_EXPERTS - 1)
    n_used = (ends[-1] // TE).astype(jnp.int32).reshape(1)
    pad_start = jnp.concatenate([offs + counts, ends[-1:]])
    pad_len = jnp.concatenate([padded - counts, s_pad - ends[-1:]])
    hs = _dispatch(h, pos_tiles, pad_start, pad_len, s_pad)
    ys = _experts(hs.reshape(s_pad * ROW_TILES, LANES), tile_expert, n_used, layer, w_up, b_up, w_down, b_down)
    return _combine(x, mods, gate.T, pos_tiles, ys.reshape(s_pad, ROW_TILES, LANES), tpb, nb)


def _final_kernel(x_ref, g_ref, o_ref):
    o_ref[...] = _rms(x_ref[...], g_ref[...])


def _final_norm(x, g, batch, tpb):
    t, d = x.shape
    lt = tpb - 1
    return pl.pallas_call(
        _final_kernel,
        grid=(batch, lt),
        in_specs=[
            pl.BlockSpec((TM, d), lambda b, j: (b * tpb + 1 + j, 0)),
            pl.BlockSpec((1, d), lambda b, j: (0, 0)),
        ],
        out_specs=pl.BlockSpec((TM, d), lambda b, j: (b * lt + j, 0)),
        out_shape=jax.ShapeDtypeStruct((batch * lt * TM, d), F32),
        compiler_params=_cparams(("parallel", "parallel")),
        name="final_norm",
    )(x, g)


def _position_tables(batch, seq, ctx_len):
    rows = seq // GRID_W
    row = jnp.broadcast_to(jnp.arange(rows, dtype=F32)[:, None], (rows, GRID_W)).reshape(-1)
    col = jnp.broadcast_to(jnp.arange(GRID_W, dtype=F32)[None, :], (rows, GRID_W)).reshape(-1)
    n_ax = MLA_ROPE // 4
    ax_freq = ROPE_BASE ** (-jnp.arange(n_ax, dtype=F32) / n_ax)
    ang_mla = jnp.concatenate([row[:, None] * ax_freq, col[:, None] * ax_freq], axis=-1)
    ret_theta = 1.0 / (RET_THETA_BASE ** jnp.linspace(0.0, 1.0, RET_DK // 2, dtype=F32))
    ang_ret = jnp.arange(seq, dtype=F32)[:, None] * ret_theta[None, :]

    def with_ctx(lat, ctx_val):
        ctx = jnp.full((ctx_len, lat.shape[1]), ctx_val, F32)
        return jnp.tile(jnp.concatenate([ctx, lat], axis=0), (batch, 1))

    ret_cos = with_ctx(jnp.cos(ang_ret), 1.0)
    ret_sin = with_ctx(jnp.sin(ang_ret), 0.0)
    cm, sm = jnp.cos(ang_mla), jnp.sin(ang_mla)
    z = jnp.zeros_like(cm)
    mla_c = with_ctx(jnp.concatenate([cm, cm, z, z], axis=-1), 1.0)
    mla_s1 = with_ctx(jnp.concatenate([-sm, z, z, z], axis=-1), 0.0)
    mla_s2 = with_ctx(jnp.concatenate([z, sm, z, z], axis=-1), 0.0)
    mla_ct = with_ctx(cm, 1.0).T
    mla_st = with_ctx(sm, 0.0).T
    return ret_cos, ret_sin, (mla_c, mla_s1, mla_s2, mla_ct, mla_st)


def _mla_weights(w_in, q_norm, kv_norm, w_uq, w_ukv):
    d = w_in.shape[0]
    wcq = w_in[:, :MLA_Q_LORA].astype(BF16)
    wckv = w_in[:, MLA_Q_LORA:MLA_Q_LORA + MLA_KV_LORA].astype(BF16)
    wkr = jnp.pad(w_in[:, MLA_Q_LORA + MLA_KV_LORA:], ((0, 0), (0, 128 - MLA_ROPE))).astype(BF16)
    wuq = w_uq.reshape(MLA_Q_LORA, MLA_HEADS, MLA_NOPE + MLA_ROPE)
    wuq = jnp.pad(wuq, ((0, 0), (0, 0), (0, MLA_HEAD_PAD - MLA_NOPE - MLA_ROPE)))
    wuqt = wuq.reshape(MLA_Q_LORA, MLA_HEADS * MLA_HEAD_PAD).T.astype(BF16)
    wukv = w_ukv.reshape(MLA_KV_LORA, MLA_HEADS, MLA_NOPE + MLA_V)
    wuk = wukv[:, :, :MLA_NOPE].reshape(MLA_KV_LORA, MLA_HEADS * MLA_NOPE).astype(BF16)
    wuvt = wukv[:, :, MLA_NOPE:].reshape(MLA_KV_LORA, MLA_HEADS * MLA_V).T.astype(BF16)
    del d
    return (wcq, wckv, wkr, q_norm.reshape(1, -1), kv_norm.reshape(1, -1), wuqt, wuk, wuvt)


def kernel(x, c, ctx, c_ctx, ret_w_in, ret_decay_logit, ret_gn, ret_w_out, mla_w_in, mla_q_norm,
           mla_kv_norm, mla_w_uq, mla_w_ukv, mla_w_out, ada_w, ada_b, norm_mix, norm_ffn, router_w,
           router_b, exp_w_up, exp_b_up, exp_w_down, exp_b_down, final_norm):
    batch, seq, d = x.shape
    ctx_len = ctx.shape[1]
    depth = ada_w.shape[0]
    assert ctx_len == TM and seq % ATT_TK == 0 and seq % (ATT_QT * TM) == 0 and batch + 1 <= 8
    nb_tok = ctx_len + seq
    tpb = nb_tok // TM

    xs = jnp.concatenate([ctx, x], axis=1).reshape(batch * nb_tok, d)
    c_rows = jnp.concatenate([c, c_ctx[None, :], jnp.zeros((8 - batch - 1, d), F32)], axis=0)
    mods_all = _ada_mods(c_rows, ada_w, ada_b)
    ret_cos, ret_sin, mla_tabs = _position_tables(batch, seq, ctx_len)

    for layer in range(depth):
        j = layer // 2
        mods = mods_all[layer]
        g_mix = norm_mix[layer].reshape(1, d)
        if layer % 2 == 0:
            p = _ret_proj(xs, g_mix, mods, ret_w_in[j].astype(BF16), ret_cos, ret_sin, tpb, batch)
            o_f, o_b = _ret_core(p, ret_decay_logit[j], batch, nb_tok)
            xs = _ret_out(xs, mods, o_f, o_b, p, ret_gn[j], ret_w_out[j].astype(BF16), tpb, batch)
        else:
            wts = _mla_weights(mla_w_in[j], mla_q_norm[j], mla_kv_norm[j], mla_w_uq[j], mla_w_ukv[j])
            qt, k, vt = _mla_proj(xs, g_mix, mods, wts, mla_tabs, tpb, batch)
            o_ctx, o_lat = _attention(qt, k, vt, batch, nb_tok)
            xs = _attn_out(xs, mods, o_ctx, o_lat, mla_w_out[j].astype(BF16), tpb, batch)
        xs = _moe_layer(xs, norm_ffn[layer].reshape(1, d), mods, router_w[layer].T,
                        router_b[layer].reshape(-1, 1), layer, exp_w_up,
                        exp_b_up.reshape(depth, N_EXPERTS, 1, -1), exp_w_down,
                        exp_b_down.reshape(depth, N_EXPERTS, 1, -1), tpb, batch)

    out = _final_norm(xs, final_norm.reshape(1, d), batch, tpb)
    return out.reshape(batch, seq, d)
```

```python
import functools

import jax
import jax.numpy as jnp
import numpy as np
from jax import lax
from jax.experimental import pallas as pl
from jax.experimental.pallas import tpu as pltpu

F32 = jnp.float32
BF16 = jnp.bfloat16
HIGHEST = lax.Precision.HIGHEST

TM = 256
LANES = 128
ROW_TILES = 8
RET_HEADS = 4
RET_DK = 256
RET_DV = 512
RET_CHUNK = 128
RET_HP = 2
RET_THETA_BASE = 10000.0
MLA_HEADS = 8
MLA_NOPE = 128
MLA_ROPE = 64
MLA_V = 128
MLA_Q_LORA = 384
MLA_KV_LORA = 256
MLA_HEAD_PAD = 256
ROPE_BASE = 10000.0
GRID_W = 64
N_EXPERTS = 32
TOP_K = 4
SWIGLU_ALPHA = 1.702
SWIGLU_LIMIT = 7.0
EPS = 1e-6
GN_EPS = 1e-5
ATT_TK = 512
ATT_QT = 8
QK_SCALE = float((MLA_NOPE + MLA_ROPE) ** -0.5 * np.log2(np.e))
VMEM_LIMIT = 48 * 1024 * 1024
TE = 512
ZERO_ROWS = 256
EXPERT_VMEM_LIMIT = 58 * 1024 * 1024


def _cparams(sem):
    return pltpu.CompilerParams(dimension_semantics=sem, vmem_limit_bytes=VMEM_LIMIT)


def _norm_mod(x, g, shift, scale):
    ms = jnp.mean(x * x, axis=-1, keepdims=True)
    xn = x * lax.rsqrt(ms + EPS)
    return (xn * g) * (1.0 + scale) + shift


def _rms(x, g):
    ms = jnp.mean(x * x, axis=-1, keepdims=True)
    return x * lax.rsqrt(ms + EPS) * g


def _mod_row_map(tpb, nb):
    def f(i):
        return jnp.where(i % tpb == 0, nb, i // tpb)
    return f


def _ada_kernel(c_ref, w_ref, b_ref, o_ref):
    s = c_ref[...]
    s = s * jax.nn.sigmoid(s)
    o_ref[...] = jnp.dot(s, w_ref[...], precision=HIGHEST, preferred_element_type=F32) + b_ref[...]


def _ada_mods(c_rows, ada_w, ada_b):
    depth, d, d6 = ada_w.shape
    nt = d6 // d
    out = pl.pallas_call(
        _ada_kernel,
        grid=(depth, nt),
        in_specs=[
            pl.BlockSpec((8, d), lambda l, n: (0, 0)),
            pl.BlockSpec((None, d, d), lambda l, n: (l, 0, n)),
            pl.BlockSpec((None, 1, d), lambda l, n: (l, 0, n)),
        ],
        out_specs=pl.BlockSpec((None, 8, d), lambda l, n: (l, 0, n)),
        out_shape=jax.ShapeDtypeStruct((depth, 8, d6), F32),
        compiler_params=_cparams(("parallel", "parallel")),
        name="ada_mods",
    )(c_rows, ada_w, ada_b.reshape(depth, 1, d6))
    return out.reshape(depth, 8, nt, d)


def _ret_proj_kernel(x_ref, g_ref, mod_ref, w_ref, cos_ref, sin_ref, o_ref):
    h = _norm_mod(x_ref[...], g_ref[...], mod_ref[0:1, :], mod_ref[1:2, :]).astype(BF16)
    tn = 2 * RET_HEADS * RET_DK
    p = jnp.dot(h, w_ref[:, 0:tn], preferred_element_type=F32)
    cos = cos_ref[...]
    sin = sin_ref[...]
    half = RET_DK // 2
    for which in range(2):
        sc = 1.0 if which == 0 else RET_DK ** -0.5
        for hd in range(RET_HEADS):
            base = which * RET_HEADS * RET_DK + hd * RET_DK
            x1 = p[:, base:base + half]
            x2 = p[:, base + half:base + RET_DK]
            o_ref[:, base:base + half] = ((x1 * cos - x2 * sin) * sc).astype(BF16)
            o_ref[:, base + half:base + RET_DK] = ((x1 * sin + x2 * cos) * sc).astype(BF16)
    for n in range(1, w_ref.shape[1] // tn):
        cols = slice(n * tn, (n + 1) * tn)
        o_ref[:, cols] = jnp.dot(h, w_ref[:, cols], preferred_element_type=F32).astype(BF16)


def _ret_proj(x, g, mods, w_bf, cos_t, sin_t, tpb, nb):
    t, d = x.shape
    ncols = w_bf.shape[1]
    row = _mod_row_map(tpb, nb)
    return pl.pallas_call(
        _ret_proj_kernel,
        grid=(t // TM,),
        in_specs=[
            pl.BlockSpec((TM, d), lambda i: (i, 0)),
            pl.BlockSpec((1, d), lambda i: (0, 0)),
            pl.BlockSpec((None, 6, d), lambda i: (row(i), 0, 0)),
            pl.BlockSpec((d, ncols), lambda i: (0, 0), pipeline_mode=pl.Buffered(1)),
            pl.BlockSpec((TM, RET_DK // 2), lambda i: (i, 0)),
            pl.BlockSpec((TM, RET_DK // 2), lambda i: (i, 0)),
        ],
        out_specs=pl.BlockSpec((TM, ncols), lambda i: (i, 0)),
        out_shape=jax.ShapeDtypeStruct((t, ncols), BF16),
        compiler_params=_cparams(("parallel",)),
        name="ret_proj",
    )(x, g, mods, w_bf, cos_t, sin_t)


def _log_sigmoid(x):
    return jnp.minimum(x, 0.0) - jnp.log1p(jnp.exp(-jnp.abs(x)))


def _ret_direction(q_ref, k_ref, v_ref, o_ref, s_ref, logit, forward, hh):
    c = RET_CHUNK
    lg_row = _log_sigmoid(logit + jnp.zeros((1, c), F32))
    lg = lg_row[:, 0:1]
    i_col = lax.broadcasted_iota(jnp.int32, (c, 1), 0).astype(F32)
    i_mat = lax.broadcasted_iota(jnp.int32, (c, c), 0).astype(F32)
    j_mat = lax.broadcasted_iota(jnp.int32, (c, c), 1).astype(F32)
    if forward:
        rel = i_mat - j_mat
        q_pow = i_col + 1.0
        k_pow = (c - 1.0) - i_col
    else:
        rel = j_mat - i_mat
        q_pow = c - i_col
        k_pow = i_col
    mask = jnp.where(rel >= 0, jnp.exp(lg_row * jnp.maximum(rel, 0.0)), 0.0)
    q_decay = jnp.exp(lg * q_pow)
    k_decay = jnp.exp(lg * k_pow)
    chunk_decay = jnp.exp(lg * float(c))

    q = q_ref[:, hh * RET_DK:(hh + 1) * RET_DK]
    k = k_ref[:, hh * RET_DK:(hh + 1) * RET_DK]
    v = v_ref[:, hh * RET_DV:(hh + 1) * RET_DV]
    s = s_ref[...]
    raw = lax.dot_general(q, k, (((1,), (1,)), ((), ())), preferred_element_type=F32)
    qs = (q.astype(F32) * q_decay).astype(BF16)
    inter = jnp.dot(qs, s.astype(BF16), preferred_element_type=F32)
    ks = (k.astype(F32) * k_decay).astype(BF16)
    upd = lax.dot_general(ks, v, (((0,), (0,)), ((), ())), preferred_element_type=F32)

    def finish():
        o = jnp.dot((raw * mask).astype(BF16), v, preferred_element_type=F32) + inter
        s_ref[...] = s * chunk_decay + upd
        o_ref[:, hh * RET_DV:(hh + 1) * RET_DV] = o.astype(BF16)

    return finish


def _ret_core_kernel(dl_ref, qf, kf, vf, qb, kb, vb, of_ref, ob_ref, *states):
    hblk = pl.program_id(1)

    @pl.when(pl.program_id(2) == 0)
    def _():
        for s_ref in states:
            s_ref[...] = jnp.zeros_like(s_ref)

    finishes = []
    for hh in range(RET_HP):
        hd = hblk * RET_HP + hh
        finishes.append(_ret_direction(qf, kf, vf, of_ref, states[2 * hh], dl_ref[0, hd], True, hh))
        finishes.append(_ret_direction(qb, kb, vb, ob_ref, states[2 * hh + 1], dl_ref[1, hd], False, hh))
    for fin in finishes:
        fin()


def _ret_core(p, decay_logit, batch, nb_tok):
    t = p.shape[0]
    nc = nb_tok // RET_CHUNK
    nctx = TM // RET_CHUNK
    qw, vw = RET_HP * RET_DK, RET_HP * RET_DV
    kq = RET_HEADS * RET_DK // qw
    kv = 2 * RET_HEADS * RET_DK // vw

    def cf(b, h, s):
        return b * nc + s

    def cb(b, h, s):
        return b * nc + jnp.where(s < nctx, nctx - 1 - s, nc - 1 + nctx - s)

    def spec(width, off, cmap):
        return pl.BlockSpec((RET_CHUNK, width), lambda b, h, s: (cmap(b, h, s), off + h))

    out_f = pl.BlockSpec((RET_CHUNK, vw), lambda b, h, s: (cf(b, h, s), h))
    out_b = pl.BlockSpec((RET_CHUNK, vw), lambda b, h, s: (cb(b, h, s), h))
    return pl.pallas_call(
        _ret_core_kernel,
        grid=(batch, RET_HEADS // RET_HP, nc),
        in_specs=[
            pl.BlockSpec(memory_space=pltpu.SMEM),
            spec(qw, 0, cf), spec(qw, kq, cf), spec(vw, kv, cf),
            spec(qw, 0, cb), spec(qw, kq, cb), spec(vw, kv, cb),
        ],
        out_specs=[out_f, out_b],
        out_shape=[jax.ShapeDtypeStruct((t, RET_HEADS * RET_DV), BF16)] * 2,
        scratch_shapes=[pltpu.VMEM((RET_DK, RET_DV), F32)] * (2 * RET_HP),
        compiler_params=_cparams(("parallel", "parallel", "arbitrary")),
        name="ret_core",
    )(decay_logit, p, p, p, p, p, p)


def _group_norm_head(o):
    mu = jnp.mean(o, axis=-1, keepdims=True)
    dlt = o - mu
    var = jnp.mean(dlt * dlt, axis=-1, keepdims=True)
    return dlt * lax.rsqrt(var + GN_EPS)


def _ret_out_kernel(x_ref, mod_ref, of_ref, ob_ref, gf_ref, gb_ref, gn_ref, w_ref, o_ref):
    acc = jnp.zeros(x_ref.shape, F32)
    for hd in range(RET_HEADS):
        sl = slice(hd * RET_DV, (hd + 1) * RET_DV)
        gf = gf_ref[:, sl].astype(F32)
        gb = gb_ref[:, sl].astype(F32)
        yf = _group_norm_head(of_ref[:, sl].astype(F32)) * gn_ref[0:1, sl]
        yb = _group_norm_head(ob_ref[:, sl].astype(F32)) * gn_ref[1:2, sl]
        y = gf * jax.nn.sigmoid(gf) * yf + gb * jax.nn.sigmoid(gb) * yb
        acc = acc + jnp.dot(y.astype(BF16), w_ref[sl, :], preferred_element_type=F32)
    o_ref[...] = x_ref[...] + mod_ref[2:3, :] * acc


def _ret_out(x, mods, o_f, o_b, p, gn, w_bf, tpb, nb):
    t, d = x.shape
    hv = RET_HEADS * RET_DV
    row = _mod_row_map(tpb, nb)
    gf_blk = (2 * RET_HEADS * RET_DK + hv) // hv
    return pl.pallas_call(
        _ret_out_kernel,
        grid=(t // TM,),
        in_specs=[
            pl.BlockSpec((TM, d), lambda i: (i, 0)),
            pl.BlockSpec((None, 6, d), lambda i: (row(i), 0, 0)),
            pl.BlockSpec((TM, hv), lambda i: (i, 0)),
            pl.BlockSpec((TM, hv), lambda i: (i, 0)),
            pl.BlockSpec((TM, hv), lambda i: (i, gf_blk)),
            pl.BlockSpec((TM, hv), lambda i: (i, gf_blk + 1)),
            pl.BlockSpec((2, hv), lambda i: (0, 0)),
            pl.BlockSpec((hv, d), lambda i: (0, 0)),
        ],
        out_specs=pl.BlockSpec((TM, d), lambda i: (i, 0)),
        out_shape=jax.ShapeDtypeStruct((t, d), F32),
        compiler_params=_cparams(("parallel",)),
        name="ret_out",
    )(x, mods, o_f, o_b, p, p, gn, w_bf)


def _mla_proj_kernel(x_ref, g_ref, mod_ref, wcq_ref, wckv_ref, wkr_ref, qn_ref, kvn_ref,
                     wuqt_ref, wuk_ref, wuvt_ref, c_ref, s1_ref, s2_ref, ct_ref, st_ref,
                     qt_out, k_out, vt_out):
    h = _norm_mod(x_ref[...], g_ref[...], mod_ref[0:1, :], mod_ref[1:2, :]).astype(BF16)
    cq = jnp.dot(h, wcq_ref[...], preferred_element_type=F32)
    ckv = jnp.dot(h, wckv_ref[...], preferred_element_type=F32)
    kr = jnp.dot(h, wkr_ref[...], preferred_element_type=F32)
    cqn = _rms(cq, qn_ref[...]).astype(BF16)
    ckvn = _rms(ckv, kvn_ref[...]).astype(BF16)
    nt = (((1,), (1,)), ((), ()))
    qt = lax.dot_general(wuqt_ref[...], cqn, nt, preferred_element_type=F32)
    vt = lax.dot_general(wuvt_ref[...], ckvn, nt, preferred_element_type=F32)
    kn = jnp.dot(ckvn, wuk_ref[...], preferred_element_type=F32)
    half = MLA_ROPE // 2
    krr = (kr * c_ref[...] + pltpu.roll(kr, 128 - half, 1) * s1_ref[...]
           + pltpu.roll(kr, half, 1) * s2_ref[...]).astype(BF16)
    ct = ct_ref[...]
    st = st_ref[...]
    for hd in range(MLA_HEADS):
        b0 = hd * MLA_HEAD_PAD
        r0 = b0 + MLA_NOPE
        qt_out[b0:r0, :] = (qt[b0:r0, :] * QK_SCALE).astype(BF16)
        x1 = qt[r0:r0 + half, :]
        x2 = qt[r0 + half:r0 + 2 * half, :]
        qt_out[r0:r0 + half, :] = ((x1 * ct - x2 * st) * QK_SCALE).astype(BF16)
        qt_out[r0 + half:r0 + 2 * half, :] = ((x1 * st + x2 * ct) * QK_SCALE).astype(BF16)
        qt_out[r0 + 2 * half:b0 + MLA_HEAD_PAD, :] = jnp.zeros(
            (MLA_HEAD_PAD - MLA_NOPE - MLA_ROPE, TM), BF16)
        k_out[:, b0:r0] = kn[:, hd * MLA_NOPE:(hd + 1) * MLA_NOPE].astype(BF16)
        k_out[:, r0:b0 + MLA_HEAD_PAD] = krr
    vt_out[...] = vt.astype(BF16)


def _mla_proj(x, g, mods, wts, tabs, tpb, nb):
    t, d = x.shape
    row = _mod_row_map(tpb, nb)
    wcq, wckv, wkr, qn, kvn, wuqt, wuk, wuvt = wts
    full = lambda a: pl.BlockSpec(a.shape, lambda i: (0,) * a.ndim)
    hq = MLA_HEADS * MLA_HEAD_PAD
    hv = MLA_HEADS * MLA_V
    tok_cols = lambda r: pl.BlockSpec((r, TM), lambda i: (0, i))
    return pl.pallas_call(
        _mla_proj_kernel,
        grid=(t // TM,),
        in_specs=[
            pl.BlockSpec((TM, d), lambda i: (i, 0)),
            pl.BlockSpec((1, d), lambda i: (0, 0)),
            pl.BlockSpec((None, 6, d), lambda i: (row(i), 0, 0)),
            full(wcq), full(wckv), full(wkr), full(qn), full(kvn), full(wuqt), full(wuk), full(wuvt),
            pl.BlockSpec((TM, 128), lambda i: (i, 0)),
            pl.BlockSpec((TM, 128), lambda i: (i, 0)),
            pl.BlockSpec((TM, 128), lambda i: (i, 0)),
            tok_cols(MLA_ROPE // 2), tok_cols(MLA_ROPE // 2),
        ],
        out_specs=[tok_cols(hq), pl.BlockSpec((TM, hq), lambda i: (i, 0)), tok_cols(hv)],
        out_shape=[
            jax.ShapeDtypeStruct((hq, t), BF16),
            jax.ShapeDtypeStruct((t, hq), BF16),
            jax.ShapeDtypeStruct((hv, t), BF16),
        ],
        compiler_params=_cparams(("parallel",)),
        name="mla_proj",
    )(x, g, mods, wcq, wckv, wkr, qn, kvn, wuqt, wuk, wuvt, *tabs)


def _scores(kt, qt):
    return jnp.dot(kt, qt, preferred_element_type=F32)


def _softmax_step(s, vtt, carry):
    m, l, acc = carry
    m_new = jnp.maximum(m, jnp.max(s, axis=0, keepdims=True))
    a = jnp.exp2(m - m_new)
    p = jnp.exp2(s - m_new)
    l = a * l + jnp.sum(p, axis=0, keepdims=True)
    acc = a * acc + jnp.dot(vtt, p.astype(BF16), preferred_element_type=F32)
    return m_new, l, acc


def _softmax_init():
    return (jnp.full((1, TM), -jnp.inf, F32), jnp.zeros((1, TM), F32), jnp.zeros((MLA_V, TM), F32))


def _attn_lat_kernel(*refs, n_lat_tiles):
    qt_refs = refs[:ATT_QT]
    k_ref, vt_ref, o_ref = refs[ATT_QT:ATT_QT + 3]
    scr = refs[ATT_QT + 3:]
    s_a, s_b, acc_s = scr[:ATT_QT], scr[ATT_QT:2 * ATT_QT], scr[2 * ATT_QT:]
    chains = range(ATT_QT)

    def k_tile(j):
        return k_ref[pl.ds(pl.multiple_of(TM + j * ATT_TK, ATT_TK // 2), ATT_TK), :]

    def vt_tile(j):
        return vt_ref[:, pl.ds(pl.multiple_of(TM + j * ATT_TK, ATT_TK // 2), ATT_TK)]

    def put_scores(j, dst):
        kt = k_tile(j)
        for c in chains:
            dst[c][...] = _scores(kt, qt_refs[c][...])

    def softmax(src, j, ml):
        vtt = vt_tile(j)
        out = []
        for c in chains:
            m, l, acc = _softmax_step(src[c][...], vtt, (ml[c][0], ml[c][1], acc_s[c][...]))
            acc_s[c][...] = acc
            out.append((m, l))
        return tuple(out)

    ctx_scores = tuple(_scores(k_ref[0:TM, :], qt_refs[c][...]) for c in chains)
    ml = []
    for c in chains:
        m, l, acc = _softmax_step(ctx_scores[c], vt_ref[:, 0:TM], _softmax_init())
        acc_s[c][...] = acc
        ml.append((m, l))
    ml = tuple(ml)

    put_scores(0, s_a)

    def body(i, ml):
        j = 2 * i
        put_scores(j + 1, s_b)
        ml = softmax(s_a, j, ml)
        put_scores(j + 2, s_a)
        return softmax(s_b, j + 1, ml)

    ml = lax.fori_loop(0, n_lat_tiles // 2 - 1, body, ml)
    put_scores(n_lat_tiles - 1, s_b)
    ml = softmax(s_a, n_lat_tiles - 2, ml)
    ml = softmax(s_b, n_lat_tiles - 1, ml)
    for c in chains:
        o_ref[c * TM:(c + 1) * TM, :] = (acc_s[c][...] / ml[c][1]).T.astype(BF16)


def _attn_ctx_kernel(qt_ref, k_ref, vt_ref, o_ref):
    m, l, acc = _softmax_step(_scores(k_ref[...], qt_ref[...]), vt_ref[...], _softmax_init())
    o_ref[...] = (acc / l).T.astype(BF16)


def _attention(qt, k, vt, batch, nb_tok):
    tpb = nb_tok // TM
    lat_tok = nb_tok - TM
    n_lat_tiles = lat_tok // ATT_TK
    qb = ATT_QT * TM
    nq = lat_tok // qb
    hv = MLA_HEADS * MLA_V

    def q_spec(c):
        return pl.BlockSpec((MLA_HEAD_PAD, TM), lambda b, h, i: (h, b * tpb + 1 + ATT_QT * i + c))

    o_lat = pl.pallas_call(
        functools.partial(_attn_lat_kernel, n_lat_tiles=n_lat_tiles),
        grid=(batch, MLA_HEADS, nq),
        in_specs=[q_spec(c) for c in range(ATT_QT)] + [
            pl.BlockSpec((nb_tok, MLA_HEAD_PAD), lambda b, h, i: (b, h)),
            pl.BlockSpec((MLA_V, nb_tok), lambda b, h, i: (h, b)),
        ],
        out_specs=pl.BlockSpec((qb, MLA_V), lambda b, h, i: (b * nq + i, h)),
        out_shape=jax.ShapeDtypeStruct((batch * lat_tok, hv), BF16),
        scratch_shapes=([pltpu.VMEM((ATT_TK, TM), F32)] * (2 * ATT_QT)
                        + [pltpu.VMEM((MLA_V, TM), F32)] * ATT_QT),
        compiler_params=_cparams(("parallel", "parallel", "arbitrary")),
        name="mla_attn",
    )(*([qt] * ATT_QT), k, vt)
    o_ctx = pl.pallas_call(
        _attn_ctx_kernel,
        grid=(batch, MLA_HEADS),
        in_specs=[
            pl.BlockSpec((MLA_HEAD_PAD, TM), lambda b, h: (h, b * tpb)),
            pl.BlockSpec((TM, MLA_HEAD_PAD), lambda b, h: (b * tpb, h)),
            pl.BlockSpec((MLA_V, TM), lambda b, h: (h, b * tpb)),
        ],
        out_specs=pl.BlockSpec((TM, MLA_V), lambda b, h: (b, h)),
        out_shape=jax.ShapeDtypeStruct((batch * TM, hv), BF16),
        compiler_params=_cparams(("parallel", "parallel")),
        name="mla_attn_ctx",
    )(qt, k, vt)
    return o_ctx, o_lat


def _attn_out_kernel(x_ref, mod_ref, oc_ref, ol_ref, w_ref, out_ref, *, tpb):
    is_ctx = pl.program_id(0) % tpb == 0

    @pl.when(is_ctx)
    def _():
        y = jnp.dot(oc_ref[...], w_ref[...], preferred_element_type=F32)
        out_ref[...] = x_ref[...] + mod_ref[2:3, :] * y

    @pl.when(jnp.logical_not(is_ctx))
    def _():
        y = jnp.dot(ol_ref[...], w_ref[...], preferred_element_type=F32)
        out_ref[...] = x_ref[...] + mod_ref[2:3, :] * y


def _attn_out(x, mods, o_ctx, o_lat, w_bf, tpb, nb):
    t, d = x.shape
    row = _mod_row_map(tpb, nb)
    hv = o_lat.shape[1]
    lt = tpb - 1
    return pl.pallas_call(
        functools.partial(_attn_out_kernel, tpb=tpb),
        grid=(t // TM,),
        in_specs=[
            pl.BlockSpec((TM, d), lambda i: (i, 0)),
            pl.BlockSpec((None, 6, d), lambda i: (row(i), 0, 0)),
            pl.BlockSpec((TM, hv), lambda i: (i // tpb, 0)),
            pl.BlockSpec((TM, hv), lambda i: ((i // tpb) * lt + jnp.maximum(i % tpb - 1, 0), 0)),
            pl.BlockSpec(w_bf.shape, lambda i: (0, 0)),
        ],
        out_specs=pl.BlockSpec((TM, d), lambda i: (i, 0)),
        out_shape=jax.ShapeDtypeStruct((t, d), F32),
        compiler_params=_cparams(("parallel",)),
        name="attn_out",
    )(x, mods, o_ctx, o_lat, w_bf)


def _router_kernel(x_ref, g_ref, mod_ref, rw_ref, rb_ref,
                   h_out, eid_out, gate_out, rank_out, cnt_out, carry):
    @pl.when(pl.program_id(0) == 0)
    def _():
        carry[...] = jnp.zeros_like(carry)

    h = _norm_mod(x_ref[...], g_ref[...], mod_ref[3:4, :], mod_ref[4:5, :])
    for j in range(ROW_TILES):
        h_out[pl.ds(j, TM, stride=ROW_TILES), :] = h[:, j * LANES:(j + 1) * LANES]
    logits = lax.dot_general(rw_ref[...], h, (((1,), (1,)), ((), ())), precision=HIGHEST,
                             preferred_element_type=F32) + rb_ref[...]
    e_iota = lax.broadcasted_iota(jnp.int32, (N_EXPERTS, TM), 0).astype(F32)
    work = logits
    vals, idxs, hots = [], [], []
    for _ in range(TOP_K):
        m = jnp.max(work, axis=0, keepdims=True)
        idx = jnp.min(jnp.where(work == m, e_iota, float(N_EXPERTS)), axis=0, keepdims=True)
        hot = e_iota == idx
        vals.append(m)
        idxs.append(idx)
        hots.append(hot)
        work = jnp.where(hot, -jnp.inf, work)
    exps = [jnp.exp(vv - vals[0]) for vv in vals]
    den = exps[0]
    for e in exps[1:]:
        den = den + e
    mask = jnp.zeros((N_EXPERTS, TM), F32)
    for hot in hots:
        mask = mask + jnp.where(hot, 1.0, 0.0)
    r_i = lax.broadcasted_iota(jnp.int32, (TM, TM), 0)
    c_i = lax.broadcasted_iota(jnp.int32, (TM, TM), 1)
    upper = jnp.where(r_i < c_i, 1.0, 0.0).astype(BF16)
    prefix = jnp.dot(mask.astype(BF16), upper, preferred_element_type=F32) + carry[:, 0:1]
    for kk in range(TOP_K):
        eid_out[kk:kk + 1, :] = idxs[kk].astype(jnp.int32)
        gate_out[kk:kk + 1, :] = exps[kk] / den
        rank = jnp.sum(jnp.where(hots[kk], prefix, 0.0), axis=0, keepdims=True)
        rank_out[kk:kk + 1, :] = rank.astype(jnp.int32)
    new_carry = carry[...] + jnp.sum(mask, axis=1, keepdims=True)
    carry[...] = new_carry
    cnt_out[...] = new_carry


def _router(x, g, mods, rw_t, rb, tpb, nb):
    t, d = x.shape
    row = _mod_row_map(tpb, nb)
    tok_row = lambda dt: jax.ShapeDtypeStruct((TOP_K, t), dt)
    return pl.pallas_call(
        _router_kernel,
        grid=(t // TM,),
        in_specs=[
            pl.BlockSpec((TM, d), lambda i: (i, 0)),
            pl.BlockSpec((1, d), lambda i: (0, 0)),
            pl.BlockSpec((None, 6, d), lambda i: (row(i), 0, 0)),
            pl.BlockSpec((N_EXPERTS, d), lambda i: (0, 0)),
            pl.BlockSpec((N_EXPERTS, 1), lambda i: (0, 0)),
        ],
        out_specs=[
            pl.BlockSpec((TM * ROW_TILES, LANES), lambda i: (i, 0)),
            pl.BlockSpec((TOP_K, TM), lambda i: (0, i)),
            pl.BlockSpec((TOP_K, TM), lambda i: (0, i)),
            pl.BlockSpec((TOP_K, TM), lambda i: (0, i)),
            pl.BlockSpec((N_EXPERTS, 128), lambda i: (0, 0)),
        ],
        out_shape=[
            jax.ShapeDtypeStruct((t * ROW_TILES, LANES), F32),
            tok_row(jnp.int32), tok_row(F32), tok_row(jnp.int32),
            jax.ShapeDtypeStruct((N_EXPERTS, 128), F32),
        ],
        scratch_shapes=[pltpu.VMEM((N_EXPERTS, 128), F32)],
        compiler_params=_cparams(("arbitrary",)),
        name="router",
    )(x, g, mods, rw_t, rb)


def _dispatch_kernel(pad_start, pad_len, pos_ref, h_ref, hs_out, zero_blk, sem, zsem):
    i = pl.program_id(0)
    last = pl.num_programs(0) - 1

    def row_copy(tok, kk):
        src = h_ref.at[pl.ds(pl.multiple_of(tok * ROW_TILES, ROW_TILES), ROW_TILES), :]
        return pltpu.make_async_copy(src, hs_out.at[pos_ref[0, kk * TM + tok]], sem)

    def zero_fill(seg, start):
        base = pad_start[seg]
        n = pad_len[seg]
        n_full = n // ZERO_ROWS
        rem = n - n_full * ZERO_ROWS

        def go(copy):
            if start:
                copy.start()
            else:
                copy.wait()

        def full(c, _):
            go(pltpu.make_async_copy(zero_blk, hs_out.at[pl.ds(base + c * ZERO_ROWS, ZERO_ROWS)], zsem))
            return 0

        lax.fori_loop(0, n_full, full, 0)
        bit = ZERO_ROWS // 2
        while bit >= 1:
            above = rem - rem % (2 * bit)

            @pl.when(rem % (2 * bit) >= bit)
            def _(bit=bit, above=above):
                dst = hs_out.at[pl.ds(base + n_full * ZERO_ROWS + above, bit)]
                go(pltpu.make_async_copy(zero_blk.at[pl.ds(0, bit)], dst, zsem))

            bit //= 2

    @pl.when(i == 0)
    def _():
        zero_blk[...] = jnp.zeros_like(zero_blk)
        lax.fori_loop(0, N_EXPERTS + 1, lambda seg, c: (zero_fill(seg, True), c)[1], 0)

    def issue(tok, _):
        for kk in range(TOP_K):
            row_copy(tok, kk).start(priority=kk % 2)
        return 0

    lax.fori_loop(0, TM, issue, 0)

    def drain(tok, _):
        for kk in range(TOP_K):
            row_copy(tok, kk).wait()
        return 0

    lax.fori_loop(0, TM, drain, 0)

    @pl.when(i == last)
    def _():
        lax.fori_loop(0, N_EXPERTS + 1, lambda seg, c: (zero_fill(seg, False), c)[1], 0)


def _dispatch(h, pos_tiles, pad_start, pad_len, s_pad):
    t = h.shape[0] // ROW_TILES
    grid_spec = pltpu.PrefetchScalarGridSpec(
        num_scalar_prefetch=2,
        grid=(t // TM,),
        in_specs=[
            pl.BlockSpec((None, 1, TOP_K * TM), lambda i, ps, pn: (i, 0, 0), memory_space=pltpu.SMEM),
            pl.BlockSpec((TM * ROW_TILES, LANES), lambda i, ps, pn: (i, 0)),
        ],
        out_specs=pl.BlockSpec(memory_space=pl.ANY),
        scratch_shapes=[pltpu.VMEM((ZERO_ROWS, ROW_TILES, LANES), F32), pltpu.SemaphoreType.DMA(()),
                        pltpu.SemaphoreType.DMA(())],
    )
    return pl.pallas_call(
        _dispatch_kernel,
        grid_spec=grid_spec,
        out_shape=jax.ShapeDtypeStruct((s_pad, ROW_TILES, LANES), F32),
        compiler_params=_cparams(("arbitrary",)),
        name="moe_dispatch",
    )(pad_start, pad_len, pos_tiles, h)


def _expert_kernel(te_ref, nu_ref, hs_ref, wu_ref, bu_ref, wd_ref, bd_ref, ys_ref, wu_bf, wd_bf):
    i = pl.program_id(0)
    dff = wd_ref.shape[0]
    cast_rows = 128

    @pl.when(jnp.logical_or(i == 0, te_ref[i] != te_ref[jnp.maximum(i - 1, 0)]))
    def _():
        def cast(r, _):
            rows = pl.ds(pl.multiple_of(r * cast_rows, cast_rows), cast_rows)
            wu_bf[rows, :] = wu_ref[rows, :].astype(BF16)
            wd_bf[rows, :] = wd_ref[rows, :].astype(BF16)
            return 0

        lax.fori_loop(0, dff // cast_rows, cast, 0)

    @pl.when(i < nu_ref[0])
    def _():
        h = jnp.concatenate([hs_ref[pl.ds(j, TE, stride=ROW_TILES), :] for j in range(ROW_TILES)],
                            axis=1).astype(BF16)
        u = jnp.dot(h, wu_bf[...], preferred_element_type=F32) + bu_ref[...]
        glu = jnp.minimum(u[:, :dff], SWIGLU_LIMIT)
        lin = jnp.clip(u[:, dff:], -SWIGLU_LIMIT, SWIGLU_LIMIT)
        act = glu * jax.nn.sigmoid(SWIGLU_ALPHA * glu) * (lin + 1.0)
        y = jnp.dot(act.astype(BF16), wd_bf[...], preferred_element_type=F32) + bd_ref[...]
        for j in range(ROW_TILES):
            ys_ref[pl.ds(j, TE, stride=ROW_TILES), :] = y[:, j * LANES:(j + 1) * LANES]

    @pl.when(i >= nu_ref[0])
    def _():
        ys_ref[...] = jnp.zeros_like(ys_ref)


def _experts(hs, tile_expert, n_used, layer, w_up, b_up, w_down, b_down):
    s_pad = hs.shape[0] // ROW_TILES
    d, dff2 = w_up.shape[2], w_up.shape[3]
    dff = dff2 // 2
    assert d == dff
    blk = lambda a, b: pl.BlockSpec((None, None, a, b), lambda i, te, nu: (layer, te[i], 0, 0))
    grid_spec = pltpu.PrefetchScalarGridSpec(
        num_scalar_prefetch=2,
        grid=(s_pad // TE,),
        in_specs=[
            pl.BlockSpec((TE * ROW_TILES, LANES), lambda i, te, nu: (jnp.minimum(i, nu[0] - 1), 0)),
            blk(d, dff2), blk(1, dff2), blk(dff, d), blk(1, d),
        ],
        out_specs=pl.BlockSpec((TE * ROW_TILES, LANES), lambda i, te, nu: (i, 0)),
        scratch_shapes=[pltpu.VMEM((d, dff2), BF16), pltpu.VMEM((dff, d), BF16)],
    )
    return pl.pallas_call(
        _expert_kernel,
        grid_spec=grid_spec,
        out_shape=jax.ShapeDtypeStruct((s_pad * ROW_TILES, LANES), F32),
        compiler_params=pltpu.CompilerParams(dimension_semantics=("arbitrary",),
                                             vmem_limit_bytes=EXPERT_VMEM_LIMIT),
        name="moe_experts",
    )(tile_expert, n_used, hs, w_up, b_up, w_down, b_down)


def _combine_kernel(pos_ref, x_ref, mod_ref, gate_ref, ys_ref, o_ref, ybuf, sem):
    def row_copy(tok, kk):
        slot = pos_ref[0, kk * TM + tok]
        dst = ybuf.at[pl.ds(pl.multiple_of((kk * TM + tok) * ROW_TILES, ROW_TILES), ROW_TILES), :]
        return pltpu.make_async_copy(ys_ref.at[slot], dst, sem)

    def issue(tok, _):
        for kk in range(TOP_K):
            row_copy(tok, kk).start(priority=kk % 2)
        return 0

    lax.fori_loop(0, TM, issue, 0)

    def drain(tok, _):
        for kk in range(TOP_K):
            row_copy(tok, kk).wait()
        return 0

    lax.fori_loop(0, TM, drain, 0)

    gates = [jnp.broadcast_to(gate_ref[:, kk:kk + 1], (TM, LANES)) for kk in range(TOP_K)]
    for j in range(ROW_TILES):
        cols = slice(j * LANES, (j + 1) * LANES)
        acc = gates[0] * ybuf[pl.ds(j, TM, stride=ROW_TILES), :]
        for kk in range(1, TOP_K):
            acc = acc + gates[kk] * ybuf[pl.ds(kk * TM * ROW_TILES + j, TM, stride=ROW_TILES), :]
        o_ref[:, cols] = x_ref[:, cols] + mod_ref[5:6, cols] * acc


def _combine(x, mods, gates_t, pos_tiles, ys, tpb, nb):
    t, d = x.shape
    row = _mod_row_map(tpb, nb)
    return pl.pallas_call(
        _combine_kernel,
        grid=(t // TM,),
        in_specs=[
            pl.BlockSpec((None, 1, TOP_K * TM), lambda i: (i, 0, 0), memory_space=pltpu.SMEM),
            pl.BlockSpec((TM, d), lambda i: (i, 0)),
            pl.BlockSpec((None, 6, d), lambda i: (row(i), 0, 0)),
            pl.BlockSpec((TM, TOP_K), lambda i: (i, 0)),
            pl.BlockSpec(memory_space=pl.ANY),
        ],
        out_specs=pl.BlockSpec((TM, d), lambda i: (i, 0)),
        out_shape=jax.ShapeDtypeStruct((t, d), F32),
        scratch_shapes=[pltpu.VMEM((TOP_K * TM * ROW_TILES, LANES), F32), pltpu.SemaphoreType.DMA(())],
        compiler_params=_cparams(("arbitrary",)),
        name="moe_combine",
    )(pos_tiles, x, mods, gates_t, ys)


def _moe_layer(x, g, mods, rw_t, rb, layer, w_up, b_up, w_down, b_down, tpb, nb):
    t, d = x.shape
    assert d == ROW_TILES * LANES
    n_tok_tiles = t // TM
    s_pad = t * TOP_K + N_EXPERTS * TE
    h, eid, gate, rank, cnt = _router(x, g, mods, rw_t, rb, tpb, nb)
    counts = cnt[:, 0].astype(jnp.int32)
    padded = ((counts + TE - 1) // TE) * TE
    ends = jnp.cumsum(padded)
    offs = ends - padded
    e_ids = jnp.arange(N_EXPERTS, dtype=jnp.int32)
    pos = jnp.sum(jnp.where(eid[..., None] == e_ids, offs, 0), axis=-1) + rank
    pos_tiles = pos.reshape(TOP_K, n_tok_tiles, TM).transpose(1, 0, 2).reshape(n_tok_tiles, 1, TOP_K * TM)
    tile_start = jnp.arange(s_pad // TE, dtype=jnp.int32) * TE
    tile_expert = jnp.sum((ends[None, :] <= tile_start[:, None]).astype(jnp.int32), axis=-1)
    tile_expert = jnp.minimum(tile_expert, N_EXPERTS - 1)
    n_used = (ends[-1] // TE).astype(jnp.int32).reshape(1)
    pad_start = jnp.concatenate([offs + counts, ends[-1:]])
    pad_len = jnp.concatenate([padded - counts, s_pad - ends[-1:]])
    hs = _dispatch(h, pos_tiles, pad_start, pad_len, s_pad)
    ys = _experts(hs.reshape(s_pad * ROW_TILES, LANES), tile_expert, n_used, layer, w_up, b_up, w_down, b_down)
    return _combine(x, mods, gate.T, pos_tiles, ys.reshape(s_pad, ROW_TILES, LANES), tpb, nb)


def _final_kernel(x_ref, g_ref, o_ref):
    o_ref[...] = _rms(x_ref[...], g_ref[...])


def _final_norm(x, g, batch, tpb):
    t, d = x.shape
    lt = tpb - 1
    return pl.pallas_call(
        _final_kernel,
        grid=(batch, lt),
        in_specs=[
            pl.BlockSpec((TM, d), lambda b, j: (b * tpb + 1 + j, 0)),
            pl.BlockSpec((1, d), lambda b, j: (0, 0)),
        ],
        out_specs=pl.BlockSpec((TM, d), lambda b, j: (b * lt + j, 0)),
        out_shape=jax.ShapeDtypeStruct((batch * lt * TM, d), F32),
        compiler_params=_cparams(("parallel", "parallel")),
        name="final_norm",
    )(x, g)


def _position_tables(batch, seq, ctx_len):
    rows = seq // GRID_W
    row = jnp.broadcast_to(jnp.arange(rows, dtype=F32)[:, None], (rows, GRID_W)).reshape(-1)
    col = jnp.broadcast_to(jnp.arange(GRID_W, dtype=F32)[None, :], (rows, GRID_W)).reshape(-1)
    n_ax = MLA_ROPE // 4
    ax_freq = ROPE_BASE ** (-jnp.arange(n_ax, dtype=F32) / n_ax)
    ang_mla = jnp.concatenate([row[:, None] * ax_freq, col[:, None] * ax_freq], axis=-1)
    ret_theta = 1.0 / (RET_THETA_BASE ** jnp.linspace(0.0, 1.0, RET_DK // 2, dtype=F32))
    ang_ret = jnp.arange(seq, dtype=F32)[:, None] * ret_theta[None, :]

    def with_ctx(lat, ctx_val):
        ctx = jnp.full((ctx_len, lat.shape[1]), ctx_val, F32)
        return jnp.tile(jnp.concatenate([ctx, lat], axis=0), (batch, 1))

    ret_cos = with_ctx(jnp.cos(ang_ret), 1.0)
    ret_sin = with_ctx(jnp.sin(ang_ret), 0.0)
    cm, sm = jnp.cos(ang_mla), jnp.sin(ang_mla)
    z = jnp.zeros_like(cm)
    mla_c = with_ctx(jnp.concatenate([cm, cm, z, z], axis=-1), 1.0)
    mla_s1 = with_ctx(jnp.concatenate([-sm, z, z, z], axis=-1), 0.0)
    mla_s2 = with_ctx(jnp.concatenate([z, sm, z, z], axis=-1), 0.0)
    mla_ct = with_ctx(cm, 1.0).T
    mla_st = with_ctx(sm, 0.0).T
    return ret_cos, ret_sin, (mla_c, mla_s1, mla_s2, mla_ct, mla_st)


def _mla_weights(w_in, q_norm, kv_norm, w_uq, w_ukv):
    d = w_in.shape[0]
    wcq = w_in[:, :MLA_Q_LORA].astype(BF16)
    wckv = w_in[:, MLA_Q_LORA:MLA_Q_LORA + MLA_KV_LORA].astype(BF16)
    wkr = jnp.pad(w_in[:, MLA_Q_LORA + MLA_KV_LORA:], ((0, 0), (0, 128 - MLA_ROPE))).astype(BF16)
    wuq = w_uq.reshape(MLA_Q_LORA, MLA_HEADS, MLA_NOPE + MLA_ROPE)
    wuq = jnp.pad(wuq, ((0, 0), (0, 0), (0, MLA_HEAD_PAD - MLA_NOPE - MLA_ROPE)))
    wuqt = wuq.reshape(MLA_Q_LORA, MLA_HEADS * MLA_HEAD_PAD).T.astype(BF16)
    wukv = w_ukv.reshape(MLA_KV_LORA, MLA_HEADS, MLA_NOPE + MLA_V)
    wuk = wukv[:, :, :MLA_NOPE].reshape(MLA_KV_LORA, MLA_HEADS * MLA_NOPE).astype(BF16)
    wuvt = wukv[:, :, MLA_NOPE:].reshape(MLA_KV_LORA, MLA_HEADS * MLA_V).T.astype(BF16)
    del d
    return (wcq, wckv, wkr, q_norm.reshape(1, -1), kv_norm.reshape(1, -1), wuqt, wuk, wuvt)


def kernel(x, c, ctx, c_ctx, ret_w_in, ret_decay_logit, ret_gn, ret_w_out, mla_w_in, mla_q_norm,
           mla_kv_norm, mla_w_uq, mla_w_ukv, mla_w_out, ada_w, ada_b, norm_mix, norm_ffn, router_w,
           router_b, exp_w_up, exp_b_up, exp_w_down, exp_b_down, final_norm):
    batch, seq, d = x.shape
    ctx_len = ctx.shape[1]
    depth = ada_w.shape[0]
    assert ctx_len == TM and seq % ATT_TK == 0 and seq % (ATT_QT * TM) == 0 and batch + 1 <= 8
    nb_tok = ctx_len + seq
    tpb = nb_tok // TM

    xs = jnp.concatenate([ctx, x], axis=1).reshape(batch * nb_tok, d)
    c_rows = jnp.concatenate([c, c_ctx[None, :], jnp.zeros((8 - batch - 1, d), F32)], axis=0)
    mods_all = _ada_mods(c_rows, ada_w, ada_b)
    ret_cos, ret_sin, mla_tabs = _position_tables(batch, seq, ctx_len)

    for layer in range(depth):
        j = layer // 2
        mods = mods_all[layer]
        g_mix = norm_mix[layer].reshape(1, d)
        if layer % 2 == 0:
            p = _ret_proj(xs, g_mix, mods, ret_w_in[j].astype(BF16), ret_cos, ret_sin, tpb, batch)
            o_f, o_b = _ret_core(p, ret_decay_logit[j], batch, nb_tok)
            xs = _ret_out(xs, mods, o_f, o_b, p, ret_gn[j], ret_w_out[j].astype(BF16), tpb, batch)
        else:
            wts = _mla_weights(mla_w_in[j], mla_q_norm[j], mla_kv_norm[j], mla_w_uq[j], mla_w_ukv[j])
            qt, k, vt = _mla_proj(xs, g_mix, mods, wts, mla_tabs, tpb, batch)
            o_ctx, o_lat = _attention(qt, k, vt, batch, nb_tok)
            xs = _attn_out(xs, mods, o_ctx, o_lat, mla_w_out[j].astype(BF16), tpb, batch)
        xs = _moe_layer(xs, norm_ffn[layer].reshape(1, d), mods, router_w[layer].T,
                        router_b[layer].reshape(-1, 1), layer, exp_w_up,
                        exp_b_up.reshape(depth, N_EXPERTS, 1, -1), exp_w_down,
                        exp_b_down.reshape(depth, N_EXPERTS, 1, -1), tpb, batch)

    out = _final_norm(xs, final_norm.reshape(1, d), batch, tpb)
    return out.reshape(batch, seq, d)
```

```python
import functools

import jax
import jax.numpy as jnp
import numpy as np
from jax import lax
from jax.experimental import pallas as pl
from jax.experimental.pallas import tpu as pltpu

F32 = jnp.float32
BF16 = jnp.bfloat16
HIGHEST = lax.Precision.HIGHEST

TM = 256
LANES = 128
ROW_TILES = 8
RET_HEADS = 4
RET_DK = 256
RET_DV = 512
RET_CHUNK = 128
RET_HP = 4
RET_THETA_BASE = 10000.0
MLA_HEADS = 8
MLA_NOPE = 128
MLA_ROPE = 64
MLA_V = 128
MLA_Q_LORA = 384
MLA_KV_LORA = 256
MLA_HEAD_PAD = 256
ROPE_BASE = 10000.0
GRID_W = 64
N_EXPERTS = 32
TOP_K = 4
SWIGLU_ALPHA = 1.702
SWIGLU_LIMIT = 7.0
EPS = 1e-6
GN_EPS = 1e-5
ATT_TK = 512
ATT_QT = 8
QK_SCALE = float((MLA_NOPE + MLA_ROPE) ** -0.5 * np.log2(np.e))
VMEM_LIMIT = 48 * 1024 * 1024
TE = 512
ZERO_ROWS = 256
EXPERT_VMEM_LIMIT = 58 * 1024 * 1024


def _cparams(sem):
    return pltpu.CompilerParams(dimension_semantics=sem, vmem_limit_bytes=VMEM_LIMIT)


def _norm_mod(x, g, shift, scale):
    ms = jnp.mean(x * x, axis=-1, keepdims=True)
    xn = x * lax.rsqrt(ms + EPS)
    return (xn * g) * (1.0 + scale) + shift


def _rms(x, g):
    ms = jnp.mean(x * x, axis=-1, keepdims=True)
    return x * lax.rsqrt(ms + EPS) * g


def _mod_row_map(tpb, nb):
    def f(i):
        return jnp.where(i % tpb == 0, nb, i // tpb)
    return f


def _ada_kernel(c_ref, w_ref, b_ref, o_ref):
    s = c_ref[...]
    s = s * jax.nn.sigmoid(s)
    o_ref[...] = jnp.dot(s, w_ref[...], precision=HIGHEST, preferred_element_type=F32) + b_ref[...]


def _ada_mods(c_rows, ada_w, ada_b):
    depth, d, d6 = ada_w.shape
    nt = d6 // d
    out = pl.pallas_call(
        _ada_kernel,
        grid=(depth, nt),
        in_specs=[
            pl.BlockSpec((8, d), lambda l, n: (0, 0)),
            pl.BlockSpec((None, d, d), lambda l, n: (l, 0, n)),
            pl.BlockSpec((None, 1, d), lambda l, n: (l, 0, n)),
        ],
        out_specs=pl.BlockSpec((None, 8, d), lambda l, n: (l, 0, n)),
        out_shape=jax.ShapeDtypeStruct((depth, 8, d6), F32),
        compiler_params=_cparams(("parallel", "parallel")),
        name="ada_mods",
    )(c_rows, ada_w, ada_b.reshape(depth, 1, d6))
    return out.reshape(depth, 8, nt, d)


def _ret_proj_kernel(x_ref, g_ref, mod_ref, w_ref, cos_ref, sin_ref, o_ref):
    h = _norm_mod(x_ref[...], g_ref[...], mod_ref[0:1, :], mod_ref[1:2, :]).astype(BF16)
    tn = 2 * RET_HEADS * RET_DK
    p = jnp.dot(h, w_ref[:, 0:tn], preferred_element_type=F32)
    cos = cos_ref[...]
    sin = sin_ref[...]
    half = RET_DK // 2
    for which in range(2):
        sc = 1.0 if which == 0 else RET_DK ** -0.5
        for hd in range(RET_HEADS):
            base = which * RET_HEADS * RET_DK + hd * RET_DK
            x1 = p[:, base:base + half]
            x2 = p[:, base + half:base + RET_DK]
            o_ref[:, base:base + half] = ((x1 * cos - x2 * sin) * sc).astype(BF16)
            o_ref[:, base + half:base + RET_DK] = ((x1 * sin + x2 * cos) * sc).astype(BF16)
    for n in range(1, w_ref.shape[1] // tn):
        cols = slice(n * tn, (n + 1) * tn)
        o_ref[:, cols] = jnp.dot(h, w_ref[:, cols], preferred_element_type=F32).astype(BF16)


def _ret_proj(x, g, mods, w_bf, cos_t, sin_t, tpb, nb):
    t, d = x.shape
    ncols = w_bf.shape[1]
    row = _mod_row_map(tpb, nb)
    return pl.pallas_call(
        _ret_proj_kernel,
        grid=(t // TM,),
        in_specs=[
            pl.BlockSpec((TM, d), lambda i: (i, 0)),
            pl.BlockSpec((1, d), lambda i: (0, 0)),
            pl.BlockSpec((None, 6, d), lambda i: (row(i), 0, 0)),
            pl.BlockSpec((d, ncols), lambda i: (0, 0), pipeline_mode=pl.Buffered(1)),
            pl.BlockSpec((TM, RET_DK // 2), lambda i: (i, 0)),
            pl.BlockSpec((TM, RET_DK // 2), lambda i: (i, 0)),
        ],
        out_specs=pl.BlockSpec((TM, ncols), lambda i: (i, 0)),
        out_shape=jax.ShapeDtypeStruct((t, ncols), BF16),
        compiler_params=_cparams(("parallel",)),
        name="ret_proj",
    )(x, g, mods, w_bf, cos_t, sin_t)


def _log_sigmoid(x):
    return jnp.minimum(x, 0.0) - jnp.log1p(jnp.exp(-jnp.abs(x)))


def _ret_direction(q_ref, k_ref, v_ref, o_ref, s_ref, logit, forward, hh):
    c = RET_CHUNK
    lg_row = _log_sigmoid(logit + jnp.zeros((1, c), F32))
    lg = lg_row[:, 0:1]
    i_col = lax.broadcasted_iota(jnp.int32, (c, 1), 0).astype(F32)
    i_mat = lax.broadcasted_iota(jnp.int32, (c, c), 0).astype(F32)
    j_mat = lax.broadcasted_iota(jnp.int32, (c, c), 1).astype(F32)
    if forward:
        rel = i_mat - j_mat
        q_pow = i_col + 1.0
        k_pow = (c - 1.0) - i_col
    else:
        rel = j_mat - i_mat
        q_pow = c - i_col
        k_pow = i_col
    mask = jnp.where(rel >= 0, jnp.exp(lg_row * jnp.maximum(rel, 0.0)), 0.0)
    q_decay = jnp.exp(lg * q_pow)
    k_decay = jnp.exp(lg * k_pow)
    chunk_decay = jnp.exp(lg * float(c))

    q = q_ref[:, hh * RET_DK:(hh + 1) * RET_DK]
    k = k_ref[:, hh * RET_DK:(hh + 1) * RET_DK]
    v = v_ref[:, hh * RET_DV:(hh + 1) * RET_DV]
    s = s_ref[...]
    raw = lax.dot_general(q, k, (((1,), (1,)), ((), ())), preferred_element_type=F32)
    qs = (q.astype(F32) * q_decay).astype(BF16)
    inter = jnp.dot(qs, s.astype(BF16), preferred_element_type=F32)
    ks = (k.astype(F32) * k_decay).astype(BF16)
    upd = lax.dot_general(ks, v, (((0,), (0,)), ((), ())), preferred_element_type=F32)

    def finish():
        o = jnp.dot((raw * mask).astype(BF16), v, preferred_element_type=F32) + inter
        s_ref[...] = s * chunk_decay + upd
        o_ref[:, hh * RET_DV:(hh + 1) * RET_DV] = o.astype(BF16)

    return finish


def _ret_core_kernel(dl_ref, qf, kf, vf, qb, kb, vb, of_ref, ob_ref, *states):
    hblk = pl.program_id(1)

    @pl.when(pl.program_id(2) == 0)
    def _():
        for s_ref in states:
            s_ref[...] = jnp.zeros_like(s_ref)

    finishes = []
    for hh in range(RET_HP):
        hd = hblk * RET_HP + hh
        finishes.append(_ret_direction(qf, kf, vf, of_ref, states[2 * hh], dl_ref[0, hd], True, hh))
        finishes.append(_ret_direction(qb, kb, vb, ob_ref, states[2 * hh + 1], dl_ref[1, hd], False, hh))
    for fin in finishes:
        fin()


def _ret_core(p, decay_logit, batch, nb_tok):
    t = p.shape[0]
    nc = nb_tok // RET_CHUNK
    nctx = TM // RET_CHUNK
    qw, vw = RET_HP * RET_DK, RET_HP * RET_DV
    kq = RET_HEADS * RET_DK // qw
    kv = 2 * RET_HEADS * RET_DK // vw

    def cf(b, h, s):
        return b * nc + s

    def cb(b, h, s):
        return b * nc + jnp.where(s < nctx, nctx - 1 - s, nc - 1 + nctx - s)

    def spec(width, off, cmap):
        return pl.BlockSpec((RET_CHUNK, width), lambda b, h, s: (cmap(b, h, s), off + h))

    out_f = pl.BlockSpec((RET_CHUNK, vw), lambda b, h, s: (cf(b, h, s), h))
    out_b = pl.BlockSpec((RET_CHUNK, vw), lambda b, h, s: (cb(b, h, s), h))
    return pl.pallas_call(
        _ret_core_kernel,
        grid=(batch, RET_HEADS // RET_HP, nc),
        in_specs=[
            pl.BlockSpec(memory_space=pltpu.SMEM),
            spec(qw, 0, cf), spec(qw, kq, cf), spec(vw, kv, cf),
            spec(qw, 0, cb), spec(qw, kq, cb), spec(vw, kv, cb),
        ],
        out_specs=[out_f, out_b],
        out_shape=[jax.ShapeDtypeStruct((t, RET_HEADS * RET_DV), BF16)] * 2,
        scratch_shapes=[pltpu.VMEM((RET_DK, RET_DV), F32)] * (2 * RET_HP),
        compiler_params=_cparams(("parallel", "parallel", "arbitrary")),
        name="ret_core",
    )(decay_logit, p, p, p, p, p, p)


def _group_norm_head(o):
    mu = jnp.mean(o, axis=-1, keepdims=True)
    dlt = o - mu
    var = jnp.mean(dlt * dlt, axis=-1, keepdims=True)
    return dlt * lax.rsqrt(var + GN_EPS)


def _ret_out_kernel(x_ref, mod_ref, of_ref, ob_ref, gf_ref, gb_ref, gn_ref, w_ref, o_ref):
    acc = jnp.zeros(x_ref.shape, F32)
    for hd in range(RET_HEADS):
        sl = slice(hd * RET_DV, (hd + 1) * RET_DV)
        gf = gf_ref[:, sl].astype(F32)
        gb = gb_ref[:, sl].astype(F32)
        yf = _group_norm_head(of_ref[:, sl].astype(F32)) * gn_ref[0:1, sl]
        yb = _group_norm_head(ob_ref[:, sl].astype(F32)) * gn_ref[1:2, sl]
        y = gf * jax.nn.sigmoid(gf) * yf + gb * jax.nn.sigmoid(gb) * yb
        acc = acc + jnp.dot(y.astype(BF16), w_ref[sl, :], preferred_element_type=F32)
    o_ref[...] = x_ref[...] + mod_ref[2:3, :] * acc


def _ret_out(x, mods, o_f, o_b, p, gn, w_bf, tpb, nb):
    t, d = x.shape
    hv = RET_HEADS * RET_DV
    row = _mod_row_map(tpb, nb)
    gf_blk = (2 * RET_HEADS * RET_DK + hv) // hv
    return pl.pallas_call(
        _ret_out_kernel,
        grid=(t // TM,),
        in_specs=[
            pl.BlockSpec((TM, d), lambda i: (i, 0)),
            pl.BlockSpec((None, 6, d), lambda i: (row(i), 0, 0)),
            pl.BlockSpec((TM, hv), lambda i: (i, 0)),
            pl.BlockSpec((TM, hv), lambda i: (i, 0)),
            pl.BlockSpec((TM, hv), lambda i: (i, gf_blk)),
            pl.BlockSpec((TM, hv), lambda i: (i, gf_blk + 1)),
            pl.BlockSpec((2, hv), lambda i: (0, 0)),
            pl.BlockSpec((hv, d), lambda i: (0, 0)),
        ],
        out_specs=pl.BlockSpec((TM, d), lambda i: (i, 0)),
        out_shape=jax.ShapeDtypeStruct((t, d), F32),
        compiler_params=_cparams(("parallel",)),
        name="ret_out",
    )(x, mods, o_f, o_b, p, p, gn, w_bf)


def _mla_proj_kernel(x_ref, g_ref, mod_ref, wcq_ref, wckv_ref, wkr_ref, qn_ref, kvn_ref,
                     wuqt_ref, wuk_ref, wuvt_ref, c_ref, s1_ref, s2_ref, ct_ref, st_ref,
                     qt_out, k_out, vt_out):
    h = _norm_mod(x_ref[...], g_ref[...], mod_ref[0:1, :], mod_ref[1:2, :]).astype(BF16)
    cq = jnp.dot(h, wcq_ref[...], preferred_element_type=F32)
    ckv = jnp.dot(h, wckv_ref[...], preferred_element_type=F32)
    kr = jnp.dot(h, wkr_ref[...], preferred_element_type=F32)
    cqn = _rms(cq, qn_ref[...]).astype(BF16)
    ckvn = _rms(ckv, kvn_ref[...]).astype(BF16)
    nt = (((1,), (1,)), ((), ()))
    qt = lax.dot_general(wuqt_ref[...], cqn, nt, preferred_element_type=F32)
    vt = lax.dot_general(wuvt_ref[...], ckvn, nt, preferred_element_type=F32)
    kn = jnp.dot(ckvn, wuk_ref[...], preferred_element_type=F32)
    half = MLA_ROPE // 2
    krr = (kr * c_ref[...] + pltpu.roll(kr, 128 - half, 1) * s1_ref[...]
           + pltpu.roll(kr, half, 1) * s2_ref[...]).astype(BF16)
    ct = ct_ref[...]
    st = st_ref[...]
    for hd in range(MLA_HEADS):
        b0 = hd * MLA_HEAD_PAD
        r0 = b0 + MLA_NOPE
        qt_out[b0:r0, :] = (qt[b0:r0, :] * QK_SCALE).astype(BF16)
        x1 = qt[r0:r0 + half, :]
        x2 = qt[r0 + half:r0 + 2 * half, :]
        qt_out[r0:r0 + half, :] = ((x1 * ct - x2 * st) * QK_SCALE).astype(BF16)
        qt_out[r0 + half:r0 + 2 * half, :] = ((x1 * st + x2 * ct) * QK_SCALE).astype(BF16)
        qt_out[r0 + 2 * half:b0 + MLA_HEAD_PAD, :] = jnp.zeros(
            (MLA_HEAD_PAD - MLA_NOPE - MLA_ROPE, TM), BF16)
        k_out[:, b0:r0] = kn[:, hd * MLA_NOPE:(hd + 1) * MLA_NOPE].astype(BF16)
        k_out[:, r0:b0 + MLA_HEAD_PAD] = krr
    vt_out[...] = vt.astype(BF16)


def _mla_proj(x, g, mods, wts, tabs, tpb, nb):
    t, d = x.shape
    row = _mod_row_map(tpb, nb)
    wcq, wckv, wkr, qn, kvn, wuqt, wuk, wuvt = wts
    full = lambda a: pl.BlockSpec(a.shape, lambda i: (0,) * a.ndim)
    hq = MLA_HEADS * MLA_HEAD_PAD
    hv = MLA_HEADS * MLA_V
    tok_cols = lambda r: pl.BlockSpec((r, TM), lambda i: (0, i))
    return pl.pallas_call(
        _mla_proj_kernel,
        grid=(t // TM,),
        in_specs=[
            pl.BlockSpec((TM, d), lambda i: (i, 0)),
            pl.BlockSpec((1, d), lambda i: (0, 0)),
            pl.BlockSpec((None, 6, d), lambda i: (row(i), 0, 0)),
            full(wcq), full(wckv), full(wkr), full(qn), full(kvn), full(wuqt), full(wuk), full(wuvt),
            pl.BlockSpec((TM, 128), lambda i: (i, 0)),
            pl.BlockSpec((TM, 128), lambda i: (i, 0)),
            pl.BlockSpec((TM, 128), lambda i: (i, 0)),
            tok_cols(MLA_ROPE // 2), tok_cols(MLA_ROPE // 2),
        ],
        out_specs=[tok_cols(hq), pl.BlockSpec((TM, hq), lambda i: (i, 0)), tok_cols(hv)],
        out_shape=[
            jax.ShapeDtypeStruct((hq, t), BF16),
            jax.ShapeDtypeStruct((t, hq), BF16),
            jax.ShapeDtypeStruct((hv, t), BF16),
        ],
        compiler_params=_cparams(("parallel",)),
        name="mla_proj",
    )(x, g, mods, wcq, wckv, wkr, qn, kvn, wuqt, wuk, wuvt, *tabs)


def _scores(kt, qt):
    return jnp.dot(kt, qt, preferred_element_type=F32)


def _softmax_step(s, vtt, carry):
    m, l, acc = carry
    m_new = jnp.maximum(m, jnp.max(s, axis=0, keepdims=True))
    a = jnp.exp2(m - m_new)
    p = jnp.exp2(s - m_new)
    l = a * l + jnp.sum(p, axis=0, keepdims=True)
    acc = a * acc + jnp.dot(vtt, p.astype(BF16), preferred_element_type=F32)
    return m_new, l, acc


def _softmax_init():
    return (jnp.full((1, TM), -jnp.inf, F32), jnp.zeros((1, TM), F32), jnp.zeros((MLA_V, TM), F32))


def _attn_lat_kernel(*refs, n_lat_tiles):
    qt_refs = refs[:ATT_QT]
    k_ref, vt_ref, o_ref = refs[ATT_QT:ATT_QT + 3]
    scr = refs[ATT_QT + 3:]
    s_a, s_b, acc_s = scr[:ATT_QT], scr[ATT_QT:2 * ATT_QT], scr[2 * ATT_QT:]
    chains = range(ATT_QT)

    def k_tile(j):
        return k_ref[pl.ds(pl.multiple_of(TM + j * ATT_TK, ATT_TK // 2), ATT_TK), :]

    def vt_tile(j):
        return vt_ref[:, pl.ds(pl.multiple_of(TM + j * ATT_TK, ATT_TK // 2), ATT_TK)]

    def put_scores(j, dst):
        kt = k_tile(j)
        for c in chains:
            dst[c][...] = _scores(kt, qt_refs[c][...])

    def softmax(src, j, ml):
        vtt = vt_tile(j)
        out = []
        for c in chains:
            m, l, acc = _softmax_step(src[c][...], vtt, (ml[c][0], ml[c][1], acc_s[c][...]))
            acc_s[c][...] = acc
            out.append((m, l))
        return tuple(out)

    ctx_scores = tuple(_scores(k_ref[0:TM, :], qt_refs[c][...]) for c in chains)
    ml = []
    for c in chains:
        m, l, acc = _softmax_step(ctx_scores[c], vt_ref[:, 0:TM], _softmax_init())
        acc_s[c][...] = acc
        ml.append((m, l))
    ml = tuple(ml)

    put_scores(0, s_a)

    def body(i, ml):
        j = 2 * i
        put_scores(j + 1, s_b)
        ml = softmax(s_a, j, ml)
        put_scores(j + 2, s_a)
        return softmax(s_b, j + 1, ml)

    ml = lax.fori_loop(0, n_lat_tiles // 2 - 1, body, ml)
    put_scores(n_lat_tiles - 1, s_b)
    ml = softmax(s_a, n_lat_tiles - 2, ml)
    ml = softmax(s_b, n_lat_tiles - 1, ml)
    for c in chains:
        o_ref[c * TM:(c + 1) * TM, :] = (acc_s[c][...] / ml[c][1]).T.astype(BF16)


def _attn_ctx_kernel(qt_ref, k_ref, vt_ref, o_ref):
    m, l, acc = _softmax_step(_scores(k_ref[...], qt_ref[...]), vt_ref[...], _softmax_init())
    o_ref[...] = (acc / l).T.astype(BF16)


def _attention(qt, k, vt, batch, nb_tok):
    tpb = nb_tok // TM
    lat_tok = nb_tok - TM
    n_lat_tiles = lat_tok // ATT_TK
    qb = ATT_QT * TM
    nq = lat_tok // qb
    hv = MLA_HEADS * MLA_V

    def q_spec(c):
        return pl.BlockSpec((MLA_HEAD_PAD, TM), lambda b, h, i: (h, b * tpb + 1 + ATT_QT * i + c))

    o_lat = pl.pallas_call(
        functools.partial(_attn_lat_kernel, n_lat_tiles=n_lat_tiles),
        grid=(batch, MLA_HEADS, nq),
        in_specs=[q_spec(c) for c in range(ATT_QT)] + [
            pl.BlockSpec((nb_tok, MLA_HEAD_PAD), lambda b, h, i: (b, h)),
            pl.BlockSpec((MLA_V, nb_tok), lambda b, h, i: (h, b)),
        ],
        out_specs=pl.BlockSpec((qb, MLA_V), lambda b, h, i: (b * nq + i, h)),
        out_shape=jax.ShapeDtypeStruct((batch * lat_tok, hv), BF16),
        scratch_shapes=([pltpu.VMEM((ATT_TK, TM), F32)] * (2 * ATT_QT)
                        + [pltpu.VMEM((MLA_V, TM), F32)] * ATT_QT),
        compiler_params=_cparams(("parallel", "parallel", "arbitrary")),
        name="mla_attn",
    )(*([qt] * ATT_QT), k, vt)
    o_ctx = pl.pallas_call(
        _attn_ctx_kernel,
        grid=(batch, MLA_HEADS),
        in_specs=[
            pl.BlockSpec((MLA_HEAD_PAD, TM), lambda b, h: (h, b * tpb)),
            pl.BlockSpec((TM, MLA_HEAD_PAD), lambda b, h: (b * tpb, h)),
            pl.BlockSpec((MLA_V, TM), lambda b, h: (h, b * tpb)),
        ],
        out_specs=pl.BlockSpec((TM, MLA_V), lambda b, h: (b, h)),
        out_shape=jax.ShapeDtypeStruct((batch * TM, hv), BF16),
        compiler_params=_cparams(("parallel", "parallel")),
        name="mla_attn_ctx",
    )(qt, k, vt)
    return o_ctx, o_lat


def _attn_out_kernel(x_ref, mod_ref, oc_ref, ol_ref, w_ref, out_ref, *, tpb):
    is_ctx = pl.program_id(0) % tpb == 0

    @pl.when(is_ctx)
    def _():
        y = jnp.dot(oc_ref[...], w_ref[...], preferred_element_type=F32)
        out_ref[...] = x_ref[...] + mod_ref[2:3, :] * y

    @pl.when(jnp.logical_not(is_ctx))
    def _():
        y = jnp.dot(ol_ref[...], w_ref[...], preferred_element_type=F32)
        out_ref[...] = x_ref[...] + mod_ref[2:3, :] * y


def _attn_out(x, mods, o_ctx, o_lat, w_bf, tpb, nb):
    t, d = x.shape
    row = _mod_row_map(tpb, nb)
    hv = o_lat.shape[1]
    lt = tpb - 1
    return pl.pallas_call(
        functools.partial(_attn_out_kernel, tpb=tpb),
        grid=(t // TM,),
        in_specs=[
            pl.BlockSpec((TM, d), lambda i: (i, 0)),
            pl.BlockSpec((None, 6, d), lambda i: (row(i), 0, 0)),
            pl.BlockSpec((TM, hv), lambda i: (i // tpb, 0)),
            pl.BlockSpec((TM, hv), lambda i: ((i // tpb) * lt + jnp.maximum(i % tpb - 1, 0), 0)),
            pl.BlockSpec(w_bf.shape, lambda i: (0, 0)),
        ],
        out_specs=pl.BlockSpec((TM, d), lambda i: (i, 0)),
        out_shape=jax.ShapeDtypeStruct((t, d), F32),
        compiler_params=_cparams(("parallel",)),
        name="attn_out",
    )(x, mods, o_ctx, o_lat, w_bf)


def _router_kernel(x_ref, g_ref, mod_ref, rw_ref, rb_ref,
                   h_out, eid_out, gate_out, rank_out, cnt_out, carry):
    @pl.when(pl.program_id(0) == 0)
    def _():
        carry[...] = jnp.zeros_like(carry)

    h = _norm_mod(x_ref[...], g_ref[...], mod_ref[3:4, :], mod_ref[4:5, :])
    for j in range(ROW_TILES):
        h_out[pl.ds(j, TM, stride=ROW_TILES), :] = h[:, j * LANES:(j + 1) * LANES]
    logits = lax.dot_general(rw_ref[...], h, (((1,), (1,)), ((), ())), precision=HIGHEST,
                             preferred_element_type=F32) + rb_ref[...]
    e_iota = lax.broadcasted_iota(jnp.int32, (N_EXPERTS, TM), 0).astype(F32)
    work = logits
    vals, idxs, hots = [], [], []
    for _ in range(TOP_K):
        m = jnp.max(work, axis=0, keepdims=True)
        idx = jnp.min(jnp.where(work == m, e_iota, float(N_EXPERTS)), axis=0, keepdims=True)
        hot = e_iota == idx
        vals.append(m)
        idxs.append(idx)
        hots.append(hot)
        work = jnp.where(hot, -jnp.inf, work)
    exps = [jnp.exp(vv - vals[0]) for vv in vals]
    den = exps[0]
    for e in exps[1:]:
        den = den + e
    mask = jnp.zeros((N_EXPERTS, TM), F32)
    for hot in hots:
        mask = mask + jnp.where(hot, 1.0, 0.0)
    r_i = lax.broadcasted_iota(jnp.int32, (TM, TM), 0)
    c_i = lax.broadcasted_iota(jnp.int32, (TM, TM), 1)
    upper = jnp.where(r_i < c_i, 1.0, 0.0).astype(BF16)
    prefix = jnp.dot(mask.astype(BF16), upper, preferred_element_type=F32) + carry[:, 0:1]
    for kk in range(TOP_K):
        eid_out[kk:kk + 1, :] = idxs[kk].astype(jnp.int32)
        gate_out[kk:kk + 1, :] = exps[kk] / den
        rank = jnp.sum(jnp.where(hots[kk], prefix, 0.0), axis=0, keepdims=True)
        rank_out[kk:kk + 1, :] = rank.astype(jnp.int32)
    new_carry = carry[...] + jnp.sum(mask, axis=1, keepdims=True)
    carry[...] = new_carry
    cnt_out[...] = new_carry


def _router(x, g, mods, rw_t, rb, tpb, nb):
    t, d = x.shape
    row = _mod_row_map(tpb, nb)
    tok_row = lambda dt: jax.ShapeDtypeStruct((TOP_K, t), dt)
    return pl.pallas_call(
        _router_kernel,
        grid=(t // TM,),
        in_specs=[
            pl.BlockSpec((TM, d), lambda i: (i, 0)),
            pl.BlockSpec((1, d), lambda i: (0, 0)),
            pl.BlockSpec((None, 6, d), lambda i: (row(i), 0, 0)),
            pl.BlockSpec((N_EXPERTS, d), lambda i: (0, 0)),
            pl.BlockSpec((N_EXPERTS, 1), lambda i: (0, 0)),
        ],
        out_specs=[
            pl.BlockSpec((TM * ROW_TILES, LANES), lambda i: (i, 0)),
            pl.BlockSpec((TOP_K, TM), lambda i: (0, i)),
            pl.BlockSpec((TOP_K, TM), lambda i: (0, i)),
            pl.BlockSpec((TOP_K, TM), lambda i: (0, i)),
            pl.BlockSpec((N_EXPERTS, 128), lambda i: (0, 0)),
        ],
        out_shape=[
            jax.ShapeDtypeStruct((t * ROW_TILES, LANES), F32),
            tok_row(jnp.int32), tok_row(F32), tok_row(jnp.int32),
            jax.ShapeDtypeStruct((N_EXPERTS, 128), F32),
        ],
        scratch_shapes=[pltpu.VMEM((N_EXPERTS, 128), F32)],
        compiler_params=_cparams(("arbitrary",)),
        name="router",
    )(x, g, mods, rw_t, rb)


def _dispatch_kernel(pad_start, pad_len, pos_ref, h_ref, hs_out, zero_blk, sem, zsem):
    i = pl.program_id(0)
    last = pl.num_programs(0) - 1

    def row_copy(tok, kk):
        src = h_ref.at[pl.ds(pl.multiple_of(tok * ROW_TILES, ROW_TILES), ROW_TILES), :]
        return pltpu.make_async_copy(src, hs_out.at[pos_ref[0, kk * TM + tok]], sem)

    def zero_fill(seg, start):
        base = pad_start[seg]
        n = pad_len[seg]
        n_full = n // ZERO_ROWS
        rem = n - n_full * ZERO_ROWS

        def go(copy):
            if start:
                copy.start()
            else:
                copy.wait()

        def full(c, _):
            go(pltpu.make_async_copy(zero_blk, hs_out.at[pl.ds(base + c * ZERO_ROWS, ZERO_ROWS)], zsem))
            return 0

        lax.fori_loop(0, n_full, full, 0)
        bit = ZERO_ROWS // 2
        while bit >= 1:
            above = rem - rem % (2 * bit)

            @pl.when(rem % (2 * bit) >= bit)
            def _(bit=bit, above=above):
                dst = hs_out.at[pl.ds(base + n_full * ZERO_ROWS + above, bit)]
                go(pltpu.make_async_copy(zero_blk.at[pl.ds(0, bit)], dst, zsem))

            bit //= 2

    @pl.when(i == 0)
    def _():
        zero_blk[...] = jnp.zeros_like(zero_blk)
        lax.fori_loop(0, N_EXPERTS + 1, lambda seg, c: (zero_fill(seg, True), c)[1], 0)

    def issue(tok, _):
        for kk in range(TOP_K):
            row_copy(tok, kk).start(priority=kk % 2)
        return 0

    lax.fori_loop(0, TM, issue, 0)

    def drain(tok, _):
        for kk in range(TOP_K):
            row_copy(tok, kk).wait()
        return 0

    lax.fori_loop(0, TM, drain, 0)

    @pl.when(i == last)
    def _():
        lax.fori_loop(0, N_EXPERTS + 1, lambda seg, c: (zero_fill(seg, False), c)[1], 0)


def _dispatch(h, pos_tiles, pad_start, pad_len, s_pad):
    t = h.shape[0] // ROW_TILES
    grid_spec = pltpu.PrefetchScalarGridSpec(
        num_scalar_prefetch=2,
        grid=(t // TM,),
        in_specs=[
            pl.BlockSpec((None, 1, TOP_K * TM), lambda i, ps, pn: (i, 0, 0), memory_space=pltpu.SMEM),
            pl.BlockSpec((TM * ROW_TILES, LANES), lambda i, ps, pn: (i, 0)),
        ],
        out_specs=pl.BlockSpec(memory_space=pl.ANY),
        scratch_shapes=[pltpu.VMEM((ZERO_ROWS, ROW_TILES, LANES), F32), pltpu.SemaphoreType.DMA(()),
                        pltpu.SemaphoreType.DMA(())],
    )
    return pl.pallas_call(
        _dispatch_kernel,
        grid_spec=grid_spec,
        out_shape=jax.ShapeDtypeStruct((s_pad, ROW_TILES, LANES), F32),
        compiler_params=_cparams(("arbitrary",)),
        name="moe_dispatch",
    )(pad_start, pad_len, pos_tiles, h)


def _expert_kernel(te_ref, nu_ref, hs_ref, wu_ref, bu_ref, wd_ref, bd_ref, ys_ref, wu_bf, wd_bf):
    i = pl.program_id(0)
    dff = wd_ref.shape[0]
    cast_rows = 128

    @pl.when(jnp.logical_or(i == 0, te_ref[i] != te_ref[jnp.maximum(i - 1, 0)]))
    def _():
        def cast(r, _):
            rows = pl.ds(pl.multiple_of(r * cast_rows, cast_rows), cast_rows)
            wu_bf[rows, :] = wu_ref[rows, :].astype(BF16)
            wd_bf[rows, :] = wd_ref[rows, :].astype(BF16)
            return 0

        lax.fori_loop(0, dff // cast_rows, cast, 0)

    @pl.when(i < nu_ref[0])
    def _():
        h = jnp.concatenate([hs_ref[pl.ds(j, TE, stride=ROW_TILES), :] for j in range(ROW_TILES)],
                            axis=1).astype(BF16)
        u = jnp.dot(h, wu_bf[...], preferred_element_type=F32) + bu_ref[...]
        glu = jnp.minimum(u[:, :dff], SWIGLU_LIMIT)
        lin = jnp.clip(u[:, dff:], -SWIGLU_LIMIT, SWIGLU_LIMIT)
        act = glu * jax.nn.sigmoid(SWIGLU_ALPHA * glu) * (lin + 1.0)
        y = jnp.dot(act.astype(BF16), wd_bf[...], preferred_element_type=F32) + bd_ref[...]
        for j in range(ROW_TILES):
            ys_ref[pl.ds(j, TE, stride=ROW_TILES), :] = y[:, j * LANES:(j + 1) * LANES]

    @pl.when(i >= nu_ref[0])
    def _():
        ys_ref[...] = jnp.zeros_like(ys_ref)


def _experts(hs, tile_expert, n_used, layer, w_up, b_up, w_down, b_down):
    s_pad = hs.shape[0] // ROW_TILES
    d, dff2 = w_up.shape[2], w_up.shape[3]
    dff = dff2 // 2
    assert d == dff
    blk = lambda a, b: pl.BlockSpec((None, None, a, b), lambda i, te, nu: (layer, te[i], 0, 0))
    grid_spec = pltpu.PrefetchScalarGridSpec(
        num_scalar_prefetch=2,
        grid=(s_pad // TE,),
        in_specs=[
            pl.BlockSpec((TE * ROW_TILES, LANES), lambda i, te, nu: (jnp.minimum(i, nu[0] - 1), 0)),
            blk(d, dff2), blk(1, dff2), blk(dff, d), blk(1, d),
        ],
        out_specs=pl.BlockSpec((TE * ROW_TILES, LANES), lambda i, te, nu: (i, 0)),
        scratch_shapes=[pltpu.VMEM((d, dff2), BF16), pltpu.VMEM((dff, d), BF16)],
    )
    return pl.pallas_call(
        _expert_kernel,
        grid_spec=grid_spec,
        out_shape=jax.ShapeDtypeStruct((s_pad * ROW_TILES, LANES), F32),
        compiler_params=pltpu.CompilerParams(dimension_semantics=("arbitrary",),
                                             vmem_limit_bytes=EXPERT_VMEM_LIMIT),
        name="moe_experts",
    )(tile_expert, n_used, hs, w_up, b_up, w_down, b_down)


def _combine_kernel(pos_ref, x_ref, mod_ref, gate_ref, ys_ref, o_ref, ybuf, sem):
    def row_copy(tok, kk):
        slot = pos_ref[0, kk * TM + tok]
        dst = ybuf.at[pl.ds(pl.multiple_of((kk * TM + tok) * ROW_TILES, ROW_TILES), ROW_TILES), :]
        return pltpu.make_async_copy(ys_ref.at[slot], dst, sem)

    def issue(tok, _):
        for kk in range(TOP_K):
            row_copy(tok, kk).start(priority=kk % 2)
        return 0

    lax.fori_loop(0, TM, issue, 0)

    def drain(tok, _):
        for kk in range(TOP_K):
            row_copy(tok, kk).wait()
        return 0

    lax.fori_loop(0, TM, drain, 0)

    gates = [jnp.broadcast_to(gate_ref[:, kk:kk + 1], (TM, LANES)) for kk in range(TOP_K)]
    for j in range(ROW_TILES):
        cols = slice(j * LANES, (j + 1) * LANES)
        acc = gates[0] * ybuf[pl.ds(j, TM, stride=ROW_TILES), :]
        for kk in range(1, TOP_K):
            acc = acc + gates[kk] * ybuf[pl.ds(kk * TM * ROW_TILES + j, TM, stride=ROW_TILES), :]
        o_ref[:, cols] = x_ref[:, cols] + mod_ref[5:6, cols] * acc


def _combine(x, mods, gates_t, pos_tiles, ys, tpb, nb):
    t, d = x.shape
    row = _mod_row_map(tpb, nb)
    return pl.pallas_call(
        _combine_kernel,
        grid=(t // TM,),
        in_specs=[
            pl.BlockSpec((None, 1, TOP_K * TM), lambda i: (i, 0, 0), memory_space=pltpu.SMEM),
            pl.BlockSpec((TM, d), lambda i: (i, 0)),
            pl.BlockSpec((None, 6, d), lambda i: (row(i), 0, 0)),
            pl.BlockSpec((TM, TOP_K), lambda i: (i, 0)),
            pl.BlockSpec(memory_space=pl.ANY),
        ],
        out_specs=pl.BlockSpec((TM, d), lambda i: (i, 0)),
        out_shape=jax.ShapeDtypeStruct((t, d), F32),
        scratch_shapes=[pltpu.VMEM((TOP_K * TM * ROW_TILES, LANES), F32), pltpu.SemaphoreType.DMA(())],
        compiler_params=_cparams(("arbitrary",)),
        name="moe_combine",
    )(pos_tiles, x, mods, gates_t, ys)


def _moe_layer(x, g, mods, rw_t, rb, layer, w_up, b_up, w_down, b_down, tpb, nb):
    t, d = x.shape
    assert d == ROW_TILES * LANES
    n_tok_tiles = t // TM
    s_pad = t * TOP_K + N_EXPERTS * TE
    h, eid, gate, rank, cnt = _router(x, g, mods, rw_t, rb, tpb, nb)
    counts = cnt[:, 0].astype(jnp.int32)
    padded = ((counts + TE - 1) // TE) * TE
    ends = jnp.cumsum(padded)
    offs = ends - padded
    e_ids = jnp.arange(N_EXPERTS, dtype=jnp.int32)
    pos = jnp.sum(jnp.where(eid[..., None] == e_ids, offs, 0), axis=-1) + rank
    pos_tiles = pos.reshape(TOP_K, n_tok_tiles, TM).transpose(1, 0, 2).reshape(n_tok_tiles, 1, TOP_K * TM)
    tile_start = jnp.arange(s_pad // TE, dtype=jnp.int32) * TE
    tile_expert = jnp.sum((ends[None, :] <= tile_start[:, None]).astype(jnp.int32), axis=-1)
    tile_expert = jnp.minimum(tile_expert, N_EXPERTS - 1)
    n_used = (ends[-1] // TE).astype(jnp.int32).reshape(1)
    pad_start = jnp.concatenate([offs + counts, ends[-1:]])
    pad_len = jnp.concatenate([padded - counts, s_pad - ends[-1:]])
    hs = _dispatch(h, pos_tiles, pad_start, pad_len, s_pad)
    ys = _experts(hs.reshape(s_pad * ROW_TILES, LANES), tile_expert, n_used, layer, w_up, b_up, w_down, b_down)
    return _combine(x, mods, gate.T, pos_tiles, ys.reshape(s_pad, ROW_TILES, LANES), tpb, nb)


def _final_kernel(x_ref, g_ref, o_ref):
    o_ref[...] = _rms(x_ref[...], g_ref[...])


def _final_norm(x, g, batch, tpb):
    t, d = x.shape
    lt = tpb - 1
    return pl.pallas_call(
        _final_kernel,
        grid=(batch, lt),
        in_specs=[
            pl.BlockSpec((TM, d), lambda b, j: (b * tpb + 1 + j, 0)),
            pl.BlockSpec((1, d), lambda b, j: (0, 0)),
        ],
        out_specs=pl.BlockSpec((TM, d), lambda b, j: (b * lt + j, 0)),
        out_shape=jax.ShapeDtypeStruct((batch * lt * TM, d), F32),
        compiler_params=_cparams(("parallel", "parallel")),
        name="final_norm",
    )(x, g)


def _position_tables(batch, seq, ctx_len):
    rows = seq // GRID_W
    row = jnp.broadcast_to(jnp.arange(rows, dtype=F32)[:, None], (rows, GRID_W)).reshape(-1)
    col = jnp.broadcast_to(jnp.arange(GRID_W, dtype=F32)[None, :], (rows, GRID_W)).reshape(-1)
    n_ax = MLA_ROPE // 4
    ax_freq = ROPE_BASE ** (-jnp.arange(n_ax, dtype=F32) / n_ax)
    ang_mla = jnp.concatenate([row[:, None] * ax_freq, col[:, None] * ax_freq], axis=-1)
    ret_theta = 1.0 / (RET_THETA_BASE ** jnp.linspace(0.0, 1.0, RET_DK // 2, dtype=F32))
    ang_ret = jnp.arange(seq, dtype=F32)[:, None] * ret_theta[None, :]

    def with_ctx(lat, ctx_val):
        ctx = jnp.full((ctx_len, lat.shape[1]), ctx_val, F32)
        return jnp.tile(jnp.concatenate([ctx, lat], axis=0), (batch, 1))

    ret_cos = with_ctx(jnp.cos(ang_ret), 1.0)
    ret_sin = with_ctx(jnp.sin(ang_ret), 0.0)
    cm, sm = jnp.cos(ang_mla), jnp.sin(ang_mla)
    z = jnp.zeros_like(cm)
    mla_c = with_ctx(jnp.concatenate([cm, cm, z, z], axis=-1), 1.0)
    mla_s1 = with_ctx(jnp.concatenate([-sm, z, z, z], axis=-1), 0.0)
    mla_s2 = with_ctx(jnp.concatenate([z, sm, z, z], axis=-1), 0.0)
    mla_ct = with_ctx(cm, 1.0).T
    mla_st = with_ctx(sm, 0.0).T
    return ret_cos, ret_sin, (mla_c, mla_s1, mla_s2, mla_ct, mla_st)


def _mla_weights(w_in, q_norm, kv_norm, w_uq, w_ukv):
    d = w_in.shape[0]
    wcq = w_in[:, :MLA_Q_LORA].astype(BF16)
    wckv = w_in[:, MLA_Q_LORA:MLA_Q_LORA + MLA_KV_LORA].astype(BF16)
    wkr = jnp.pad(w_in[:, MLA_Q_LORA + MLA_KV_LORA:], ((0, 0), (0, 128 - MLA_ROPE))).astype(BF16)
    wuq = w_uq.reshape(MLA_Q_LORA, MLA_HEADS, MLA_NOPE + MLA_ROPE)
    wuq = jnp.pad(wuq, ((0, 0), (0, 0), (0, MLA_HEAD_PAD - MLA_NOPE - MLA_ROPE)))
    wuqt = wuq.reshape(MLA_Q_LORA, MLA_HEADS * MLA_HEAD_PAD).T.astype(BF16)
    wukv = w_ukv.reshape(MLA_KV_LORA, MLA_HEADS, MLA_NOPE + MLA_V)
    wuk = wukv[:, :, :MLA_NOPE].reshape(MLA_KV_LORA, MLA_HEADS * MLA_NOPE).astype(BF16)
    wuvt = wukv[:, :, MLA_NOPE:].reshape(MLA_KV_LORA, MLA_HEADS * MLA_V).T.astype(BF16)
    del d
    return (wcq, wckv, wkr, q_norm.reshape(1, -1), kv_norm.reshape(1, -1), wuqt, wuk, wuvt)


def kernel(x, c, ctx, c_ctx, ret_w_in, ret_decay_logit, ret_gn, ret_w_out, mla_w_in, mla_q_norm,
           mla_kv_norm, mla_w_uq, mla_w_ukv, mla_w_out, ada_w, ada_b, norm_mix, norm_ffn, router_w,
           router_b, exp_w_up, exp_b_up, exp_w_down, exp_b_down, final_norm):
    batch, seq, d = x.shape
    ctx_len = ctx.shape[1]
    depth = ada_w.shape[0]
    assert ctx_len == TM and seq % ATT_TK == 0 and seq % (ATT_QT * TM) == 0 and batch + 1 <= 8
    nb_tok = ctx_len + seq
    tpb = nb_tok // TM

    xs = jnp.concatenate([ctx, x], axis=1).reshape(batch * nb_tok, d)
    c_rows = jnp.concatenate([c, c_ctx[None, :], jnp.zeros((8 - batch - 1, d), F32)], axis=0)
    mods_all = _ada_mods(c_rows, ada_w, ada_b)
    ret_cos, ret_sin, mla_tabs = _position_tables(batch, seq, ctx_len)

    for layer in range(depth):
        j = layer // 2
        mods = mods_all[layer]
        g_mix = norm_mix[layer].reshape(1, d)
        if layer % 2 == 0:
            p = _ret_proj(xs, g_mix, mods, ret_w_in[j].astype(BF16), ret_cos, ret_sin, tpb, batch)
            o_f, o_b = _ret_core(p, ret_decay_logit[j], batch, nb_tok)
            xs = _ret_out(xs, mods, o_f, o_b, p, ret_gn[j], ret_w_out[j].astype(BF16), tpb, batch)
        else:
            wts = _mla_weights(mla_w_in[j], mla_q_norm[j], mla_kv_norm[j], mla_w_uq[j], mla_w_ukv[j])
            qt, k, vt = _mla_proj(xs, g_mix, mods, wts, mla_tabs, tpb, batch)
            o_ctx, o_lat = _attention(qt, k, vt, batch, nb_tok)
            xs = _attn_out(xs, mods, o_ctx, o_lat, mla_w_out[j].astype(BF16), tpb, batch)
        xs = _moe_layer(xs, norm_ffn[layer].reshape(1, d), mods, router_w[layer].T,
                        router_b[layer].reshape(-1, 1), layer, exp_w_up,
                        exp_b_up.reshape(depth, N_EXPERTS, 1, -1), exp_w_down,
                        exp_b_down.reshape(depth, N_EXPERTS, 1, -1), tpb, batch)

    out = _final_norm(xs, final_norm.reshape(1, d), batch, tpb)
    return out.reshape(batch, seq, d)
```
